```python
import jax
import jax.numpy as jnp
from jax import lax
import numpy as np

D_MODEL = 4096
BATCH = 1
SEQ = 16384
DEPTH = 4

N_MIXERS = 3
N_A_LAYERS = (DEPTH + 2) // 3
N_B_LAYERS = (DEPTH + 1) // 3
N_C_LAYERS = DEPTH // 3
RMS_EPS = 1e-6
D_FF = 4 * D_MODEL
PLE_DIM = 256

A_HEAD_DIM = 128
A_HEADS_PER_GROUP = D_MODEL // 256
A_PATTERNS = ((128, 1), (512, 4), (2048, 16))
A_N_GROUPS = 3
A_BLOCK = 128
A_MAX_DILATION = 16
A_PAD_MULT = A_BLOCK * A_MAX_DILATION
ROPE_THETA = 10000.0
A_IN_WIDTH = 3 * A_N_GROUPS * A_HEADS_PER_GROUP * A_HEAD_DIM
A_OUT_WIDTH = A_HEADS_PER_GROUP * A_HEAD_DIM

B_HEADS = 8
B_QK_DIM = D_MODEL // (2 * B_HEADS)
B_V_DIM = D_MODEL // B_HEADS
B_CHUNK = 64
B_IN_WIDTH = 2 * B_HEADS * B_QK_DIM + B_HEADS * B_V_DIM + D_MODEL + 2 * B_HEADS

C_WINDOWS = (2, 4, 8, 16)
C_N_GROUPS = 4
C_GROUP_DIM = D_MODEL // C_N_GROUPS

kernel_name = 'hybrid_dilated_mlstm_pool_trunk'


def rms_norm_f32(x, gain):
    x32 = x.astype(jnp.float32)
    inv = lax.rsqrt(jnp.mean(x32 * x32, axis=-1, keepdims=True) + RMS_EPS)
    return x32 * inv * gain.astype(jnp.float32)


def rms_norm(x, gain):
    return rms_norm_f32(x, gain).astype(x.dtype)


def rotary(x, positions):
    d = x.shape[-1]
    half = d // 2
    inv_freq = ROPE_THETA ** (-jnp.arange(half, dtype=jnp.float32) * 2.0 / d)
    ang = positions.astype(jnp.float32)[:, None] * inv_freq[None, :]
    cos = jnp.cos(ang)[None, :, None, :]
    sin = jnp.sin(ang)[None, :, None, :]
    x1, x2 = x[..., :half], x[..., half:]
    return jnp.concatenate([x1 * cos - x2 * sin, x2 * cos + x1 * sin], axis=-1)


def dilated_window_attention(q, k, v, window, dilation):
    bsz, s_pad, heads, hd = q.shape
    steps = window // dilation
    sub_len = s_pad // dilation
    nb = sub_len // A_BLOCK

    def to_blocks(t):
        t = t.reshape(bsz, sub_len, dilation, heads, hd).transpose(0, 2, 1, 3, 4)
        return t.reshape(bsz, dilation, nb, A_BLOCK, heads, hd)

    def with_prev(t):
        prev = jnp.pad(t[:, :, :-1], ((0, 0), (0, 0), (1, 0), (0, 0), (0, 0), (0, 0)))
        return jnp.concatenate([prev, t], axis=3)

    qb = to_blocks(q)
    kw = with_prev(to_blocks(k))
    vw = with_prev(to_blocks(v))
    scores = jnp.einsum('brnqhd,brnkhd->brnhqk', qb, kw) * hd ** -0.5
    qi = jnp.arange(A_BLOCK)[:, None]
    ki = jnp.arange(2 * A_BLOCK)[None, :]
    dist = qi + A_BLOCK - ki
    band = (dist >= 0) & (dist <= steps)
    exists = (jnp.arange(nb)[:, None, None] > 0) | (ki >= A_BLOCK)[None]
    mask = band[None] & exists
    scores = jnp.where(mask[None, None, :, None], scores, -jnp.inf)
    mx = jnp.max(scores, axis=-1, keepdims=True)
    e = jnp.exp(scores - mx)
    den = jnp.sum(e, axis=-1, keepdims=True)
    o = jnp.einsum('brnhqk,brnkhd->brnqhd', e / den, vw)
    lse = (mx + jnp.log(den))[..., 0]
    o = o.reshape(bsz, dilation, sub_len, heads, hd).transpose(0, 2, 1, 3, 4)
    o = o.reshape(bsz, s_pad, heads, hd)
    lse = lse.transpose(0, 1, 2, 4, 3).reshape(bsz, dilation, sub_len, heads)
    lse = lse.transpose(0, 2, 1, 3).reshape(bsz, s_pad, heads)
    return o, lse


def dilated_attention_mixer(xn, w_in, q_gain, k_gain, w_out):
    bsz, seq, _ = xn.shape
    n_heads = A_N_GROUPS * A_HEADS_PER_GROUP
    qkv = jnp.matmul(xn, w_in).reshape(bsz, seq, 3, n_heads, A_HEAD_DIM)
    positions = jnp.arange(seq)
    q = rotary(rms_norm_f32(qkv[:, :, 0], q_gain), positions)
    k = rotary(rms_norm_f32(qkv[:, :, 1], k_gain), positions)
    v = qkv[:, :, 2].astype(jnp.float32)
    s_pad = -(-seq // A_PAD_MULT) * A_PAD_MULT
    pad = ((0, 0), (0, s_pad - seq), (0, 0), (0, 0))
    q, k, v = jnp.pad(q, pad), jnp.pad(k, pad), jnp.pad(v, pad)
    outs, lses = [], []
    for g, (window, dilation) in enumerate(A_PATTERNS):
        hs = slice(g * A_HEADS_PER_GROUP, (g + 1) * A_HEADS_PER_GROUP)
        o, lse = dilated_window_attention(q[:, :, hs], k[:, :, hs], v[:, :, hs], window, dilation)
        outs.append(o[:, :seq])
        lses.append(lse[:, :seq])
    weights = jax.nn.softmax(jnp.stack(lses, axis=0), axis=0)
    merged = jnp.sum(weights[..., None] * jnp.stack(outs, axis=0), axis=0)
    return jnp.matmul(merged.reshape(bsz, seq, A_OUT_WIDTH).astype(xn.dtype), w_out)


def mlstm_chunk_step(carry, inp):
    c_state, n_state, m_state = carry
    q, k, v, ig, lf = inp
    b = jnp.cumsum(lf, axis=-1)
    causal = jnp.tril(jnp.ones((B_CHUNK, B_CHUNK), dtype=bool))
    log_d = jnp.where(causal, b[..., :, None] - b[..., None, :] + ig[..., None, :], -jnp.inf)
    log_inter = b + m_state[..., None]
    m_t = jnp.maximum(log_inter, jnp.max(log_d, axis=-1))
    w = jnp.einsum('bhtd,bhsd->bhts', q, k) * jnp.exp(log_d - m_t[..., None])
    decay = jnp.exp(log_inter - m_t)
    num = decay[..., None] * jnp.einsum('bhtd,bhde->bhte', q, c_state) + jnp.einsum('bhts,bhse->bhte', w, v)
    qn = decay * jnp.einsum('bhtd,bhd->bht', q, n_state) + jnp.sum(w, axis=-1)
    h = num / jnp.maximum(jnp.abs(qn), jnp.exp(-m_t))[..., None]
    b_last = b[..., -1]
    log_w = b_last[..., None] - b + ig
    m_new = jnp.maximum(b_last + m_state, jnp.max(log_w, axis=-1))
    w_s = jnp.exp(log_w - m_new[..., None])
    carry_decay = jnp.exp(b_last + m_state - m_new)
    c_new = carry_decay[..., None, None] * c_state + jnp.einsum('bhs,bhsd,bhse->bhde', w_s, k, v)
    n_new = carry_decay[..., None] * n_state + jnp.einsum('bhs,bhsd->bhd', w_s, k)
    return (c_new, n_new, m_new), h


def mlstm_mixer(xn, w_in, gate_bias, h_gain, w_out):
    bsz, seq, _ = xn.shape
    proj = jnp.matmul(xn, w_in)
    qk_w = B_HEADS * B_QK_DIM
    v_w = B_HEADS * B_V_DIM
    o0 = 2 * qk_w + v_w
    q = proj[..., :qk_w].astype(jnp.float32).reshape(bsz, seq, B_HEADS, B_QK_DIM) * B_QK_DIM ** -0.5
    k = proj[..., qk_w:2 * qk_w].astype(jnp.float32).reshape(bsz, seq, B_HEADS, B_QK_DIM)
    v = proj[..., 2 * qk_w:o0].astype(jnp.float32).reshape(bsz, seq, B_HEADS, B_V_DIM)
    og = proj[..., o0:o0 + D_MODEL].astype(jnp.float32)
    gates = proj[..., o0 + D_MODEL:].astype(jnp.float32) + gate_bias.astype(jnp.float32)
    i_pre = gates[..., :B_HEADS]
    log_f = jax.nn.log_sigmoid(gates[..., B_HEADS:])
    n_chunks = seq // B_CHUNK

    def chunks(t):
        t = t.reshape((bsz, n_chunks, B_CHUNK) + t.shape[2:])
        return t.transpose((1, 0, 3, 2) + tuple(range(4, t.ndim)))

    init = (jnp.zeros((bsz, B_HEADS, B_QK_DIM, B_V_DIM), jnp.float32),
            jnp.zeros((bsz, B_HEADS, B_QK_DIM), jnp.float32),
            jnp.zeros((bsz, B_HEADS), jnp.float32))
    _, h = lax.scan(mlstm_chunk_step, init, (chunks(q), chunks(k), chunks(v), chunks(i_pre), chunks(log_f)))
    h = h.transpose(1, 0, 3, 2, 4).reshape(bsz, seq, B_HEADS, B_V_DIM)
    h = rms_norm_f32(h, h_gain.reshape(B_HEADS, B_V_DIM)).reshape(bsz, seq, D_MODEL)
    out = h * jax.nn.sigmoid(og)
    return jnp.matmul(out.astype(xn.dtype), w_out)


def pooling_mixer(xn, w_in, w_group, scale, w_out):
    bsz, seq, _ = xn.shape
    u = jnp.matmul(xn, w_in).astype(jnp.float32).reshape(bsz, seq, C_N_GROUPS, C_GROUP_DIM)
    csum = jnp.concatenate([jnp.zeros_like(u[:, :1]), jnp.cumsum(u, axis=1)], axis=1)
    end = jnp.arange(1, seq + 1)
    pooled = []
    for g, window in enumerate(C_WINDOWS):
        start = jnp.maximum(end - window, 0)
        total = csum[:, 1:, g] - csum[:, start, g]
        count = (end - start).astype(jnp.float32)[None, :, None]
        pooled.append(total / count - u[:, :, g])
    pooled = jnp.stack(pooled, axis=2)
    y = jnp.einsum('bsgc,gcd->bsgd', pooled, w_group.astype(jnp.float32))
    y = y * scale.astype(jnp.float32).reshape(C_N_GROUPS, C_GROUP_DIM)
    return jnp.matmul(y.reshape(bsz, seq, D_MODEL).astype(xn.dtype), w_out)


def squared_relu_mlp(xn, w_in, w_out):
    hdn = jnp.square(jax.nn.relu(jnp.matmul(xn, w_in)))
    return jnp.matmul(hdn, w_out)


def setup_inputs(seed: int = 0) -> dict:
    key = jax.random.key(seed)
    ks = jax.random.split(key, 24)

    def nrm(k, shape, scale):
        return jax.random.normal(k, shape, jnp.float32) * scale

    def gain(k, shape, noise=0.01):
        return 1.0 + noise * jax.random.normal(k, shape, jnp.float32)

    f_bias = jnp.linspace(3.0, 6.0, B_HEADS, dtype=jnp.float32)
    b_gate_bias = jnp.concatenate(
        [nrm(ks[14], (N_B_LAYERS, B_HEADS), 0.1),
         f_bias[None, :] + nrm(ks[15], (N_B_LAYERS, B_HEADS), 0.1)], axis=-1)
    return {
        'x': nrm(ks[0], (BATCH, SEQ, D_MODEL), 1.0),
        'p': nrm(ks[1], (DEPTH, BATCH, SEQ, PLE_DIM), 1.0),
        'norm_mix': gain(ks[2], (DEPTH, D_MODEL)),
        'norm_mlp': gain(ks[3], (DEPTH, D_MODEL)),
        'norm_ple': gain(ks[4], (DEPTH, D_MODEL)),
        'w_ple': nrm(ks[5], (DEPTH, PLE_DIM, D_MODEL), PLE_DIM ** -0.5),
        'w_ple_gate': nrm(ks[6], (DEPTH, D_MODEL, D_MODEL), D_MODEL ** -0.5),
        'w_mlp_in': nrm(ks[7], (DEPTH, D_MODEL, D_FF), D_MODEL ** -0.5),
        'w_mlp_out': nrm(ks[8], (DEPTH, D_FF, D_MODEL), D_FF ** -0.5),
        'a_w_in': nrm(ks[9], (N_A_LAYERS, D_MODEL, A_IN_WIDTH), D_MODEL ** -0.5),
        'a_q_norm': gain(ks[10], (N_A_LAYERS, A_HEAD_DIM)),
        'a_k_norm': gain(ks[11], (N_A_LAYERS, A_HEAD_DIM)),
        'a_w_out': nrm(ks[12], (N_A_LAYERS, A_OUT_WIDTH, D_MODEL), A_OUT_WIDTH ** -0.5),
        'b_w_in': nrm(ks[13], (N_B_LAYERS, D_MODEL, B_IN_WIDTH), D_MODEL ** -0.5),
        'b_gate_bias': b_gate_bias,
        'b_h_norm': gain(ks[16], (N_B_LAYERS, D_MODEL)),
        'b_w_out': nrm(ks[17], (N_B_LAYERS, D_MODEL, D_MODEL), D_MODEL ** -0.5),
        'c_w_in': nrm(ks[18], (N_C_LAYERS, D_MODEL, D_MODEL), D_MODEL ** -0.5),
        'c_w_group': nrm(ks[19], (N_C_LAYERS, C_N_GROUPS, C_GROUP_DIM, C_GROUP_DIM), C_GROUP_DIM ** -0.5),
        'c_scale': gain(ks[20], (N_C_LAYERS, D_MODEL), 0.1),
        'c_w_out': nrm(ks[21], (N_C_LAYERS, D_MODEL, D_MODEL), D_MODEL ** -0.5),
    }


def reference(x, p, norm_mix, norm_mlp, norm_ple, w_ple, w_ple_gate, w_mlp_in, w_mlp_out,
              a_w_in, a_q_norm, a_k_norm, a_w_out, b_w_in, b_gate_bias, b_h_norm, b_w_out,
              c_w_in, c_w_group, c_scale, c_w_out):
    h = x
    for i in range(DEPTH):
        kind = i % N_MIXERS
        j = i // N_MIXERS
        xn = rms_norm(h, norm_mix[i])
        if kind == 0:
            mix = dilated_attention_mixer(xn, a_w_in[j], a_q_norm[j], a_k_norm[j], a_w_out[j])
        elif kind == 1:
            mix = mlstm_mixer(xn, b_w_in[j], b_gate_bias[j], b_h_norm[j], b_w_out[j])
        else:
            mix = pooling_mixer(xn, c_w_in[j], c_w_group[j], c_scale[j], c_w_out[j])
        h = h + mix.astype(h.dtype)
        h = h + squared_relu_mlp(rms_norm(h, norm_mlp[i]), w_mlp_in[i], w_mlp_out[i]).astype(h.dtype)
        gate = jax.nn.sigmoid(jnp.matmul(rms_norm(h, norm_ple[i]), w_ple_gate[i]).astype(jnp.float32))
        emb = jnp.matmul(p[i], w_ple[i]).astype(jnp.float32)
        h = h + (gate * emb).astype(h.dtype)
    return h
```

```python
import functools
import math

import jax
import jax.numpy as jnp
from jax import lax
from jax.experimental import pallas as pl
from jax.experimental.pallas import tpu as pltpu

F32 = jnp.float32
BF16 = jnp.bfloat16

RMS_EPS = 1e-6
ROPE_THETA = 10000.0
PLE_DIM = 256

A_HEAD_DIM = 128
A_HEADS_PER_GROUP = 16
A_PATTERNS = ((128, 1), (512, 4), (2048, 16))
A_BLOCK = 128
A_GROUP_WIDTH = A_HEADS_PER_GROUP * A_HEAD_DIM

B_HEADS = 8
B_QK_DIM = 256
B_V_DIM = 512
B_CHUNK = 256
B_GATE_PAD = 128

C_WINDOWS = (2, 4, 8, 16)
C_GROUP_DIM = 1024
C_HALO = 128

LANES = 128
VMEM_LIMIT_BYTES = 56 * 1024 * 1024


def _params(n_grid_axes):
    return pltpu.CompilerParams(
        dimension_semantics=("arbitrary",) * n_grid_axes,
        vmem_limit_bytes=VMEM_LIMIT_BYTES)


def _rmsnorm_kernel(x_ref, g_ref, o_ref):
    x = x_ref[...]
    inv = lax.rsqrt(jnp.mean(x * x, axis=-1, keepdims=True) + RMS_EPS)
    o_ref[...] = (x * inv * g_ref[...]).astype(o_ref.dtype)


def rmsnorm_bf16(h, gain, tm=256):
    m, d = h.shape
    return pl.pallas_call(
        _rmsnorm_kernel,
        out_shape=jax.ShapeDtypeStruct((m, d), BF16),
        grid=(m // tm,),
        in_specs=[pl.BlockSpec((tm, d), lambda i: (i, 0)),
                  pl.BlockSpec((1, d), lambda i: (0, 0))],
        out_specs=pl.BlockSpec((tm, d), lambda i: (i, 0)),
        compiler_params=_params(1),
        name="rmsnorm",
    )(h, gain.reshape(1, d))


def _mm_kernel(x_ref, w_ref, *rest, epilogue, n_extra):
    extra_refs = rest[:n_extra]
    o_ref = rest[n_extra]
    acc = jnp.dot(x_ref[...], w_ref[...], preferred_element_type=F32)
    o_ref[...] = epilogue(acc, *extra_refs).astype(o_ref.dtype)


def fused_matmul(x, w, epilogue, *, out_dtype, tm, tn, extras=(), name):
    m, k = x.shape
    n = w.shape[1]
    tm, tn = min(tm, m), min(tn, n)
    assert m % tm == 0 and n % tn == 0, (m, n, tm, tn)
    in_specs = [pl.BlockSpec((tm, k), lambda i, j: (i, 0)),
                pl.BlockSpec((k, tn), lambda i, j: (0, j))]
    in_specs += [pl.BlockSpec(bs, im) for _, bs, im in extras]
    return pl.pallas_call(
        functools.partial(_mm_kernel, epilogue=epilogue, n_extra=len(extras)),
        out_shape=jax.ShapeDtypeStruct((m, n), out_dtype),
        grid=(m // tm, n // tn),
        in_specs=in_specs,
        out_specs=pl.BlockSpec((tm, tn), lambda i, j: (i, j)),
        compiler_params=_params(2),
        name=name,
    )(x, w, *[a for a, _, _ in extras])


def _ep_identity(acc):
    return acc


def _ep_relu_sq(acc):
    r = jnp.maximum(acc, 0.0)
    return r * r


def _ep_residual(acc, res_ref):
    return res_ref[...] + acc


def matmul_residual(x, w, res, *, tm=1024, tn=512, name):
    tm, tn = min(tm, x.shape[0]), min(tn, w.shape[1])
    return fused_matmul(x, w, _ep_residual, out_dtype=F32, tm=tm, tn=tn,
                        extras=[(res, (tm, tn), lambda i, j: (i, j))], name=name)


def _mm_kgrid_kernel(x_ref, w_ref, res_ref, o_ref, acc_ref):
    kk = pl.program_id(2)

    @pl.when(kk == 0)
    def _():
        acc_ref[...] = jnp.zeros_like(acc_ref)

    acc_ref[...] += jnp.dot(x_ref[...], w_ref[...], preferred_element_type=F32)

    @pl.when(kk == pl.num_programs(2) - 1)
    def _():
        o_ref[...] = res_ref[...] + acc_ref[...]


def matmul_kgrid_residual(x, w, res, *, tm=1024, tn=1024, tk=2048, name):
    m, k = x.shape
    n = w.shape[1]
    tm, tn, tk = min(tm, m), min(tn, n), min(tk, k)
    assert m % tm == 0 and n % tn == 0 and k % tk == 0
    return pl.pallas_call(
        _mm_kgrid_kernel,
        out_shape=jax.ShapeDtypeStruct((m, n), F32),
        grid=(m // tm, n // tn, k // tk),
        in_specs=[pl.BlockSpec((tm, tk), lambda i, j, kk: (i, kk)),
                  pl.BlockSpec((tk, tn), lambda i, j, kk: (kk, j)),
                  pl.BlockSpec((tm, tn), lambda i, j, kk: (i, j))],
        out_specs=pl.BlockSpec((tm, tn), lambda i, j, kk: (i, j)),
        scratch_shapes=[pltpu.VMEM((tm, tn), F32)],
        compiler_params=_params(3),
        name=name,
    )(x, w, res)


def _ep_ple(acc, p_ref, wp_ref, res_ref):
    emb = jnp.dot(p_ref[...], wp_ref[...], preferred_element_type=F32)
    return res_ref[...] + jax.nn.sigmoid(acc) * emb


def ple_update(xn, w_gate, p, w_ple, res, *, tm=1024, tn=512):
    tm, tn = min(tm, xn.shape[0]), min(tn, w_gate.shape[1])
    kp = p.shape[1]
    return fused_matmul(
        xn, w_gate, _ep_ple, out_dtype=F32, tm=tm, tn=tn,
        extras=[(p, (tm, kp), lambda i, j: (i, 0)),
                (w_ple, (kp, tn), lambda i, j: (0, j)),
                (res, (tm, tn), lambda i, j: (i, j))],
        name="ple_gate")


def _rope_table_kernel(cos_ref, sin_ref):
    tm = cos_ref.shape[0]
    half = A_HEAD_DIM // 2
    pos = (pl.program_id(0) * tm + lax.broadcasted_iota(jnp.int32, (tm, A_HEAD_DIM), 0)).astype(F32)
    lane = lax.broadcasted_iota(jnp.int32, (tm, A_HEAD_DIM), 1)
    j = jnp.where(lane >= half, lane - half, lane).astype(F32)
    inv_freq = jnp.exp(j * (-2.0 * math.log(ROPE_THETA) / A_HEAD_DIM))
    ang = pos * inv_freq
    cos_ref[...] = jnp.cos(ang)
    sin_ref[...] = jnp.where(lane >= half, 1.0, -1.0) * jnp.sin(ang)


def rope_tables(seq, tm=512):
    tm = min(tm, seq)
    spec = pl.BlockSpec((tm, A_HEAD_DIM), lambda i: (i, 0))
    shape = jax.ShapeDtypeStruct((seq, A_HEAD_DIM), F32)
    return pl.pallas_call(
        _rope_table_kernel, out_shape=(shape, shape), grid=(seq // tm,),
        in_specs=[], out_specs=(spec, spec), compiler_params=_params(1),
        name="rope_tables")()


def _ep_qk_norm_rope(acc, gain_ref, cos_ref, sin_ref):
    cos, sin = cos_ref[...], sin_ref[...]
    outs = []
    for hh in range(acc.shape[1] // A_HEAD_DIM):
        a = acc[:, hh * A_HEAD_DIM:(hh + 1) * A_HEAD_DIM]
        inv = lax.rsqrt(jnp.mean(a * a, axis=-1, keepdims=True) + RMS_EPS)
        a = a * inv * gain_ref[:, hh * A_HEAD_DIM:(hh + 1) * A_HEAD_DIM]
        outs.append(a * cos + pltpu.roll(a, A_HEAD_DIM // 2, 1) * sin)
    return jnp.concatenate(outs, axis=1)


def _attn_kernel(q_ref, kc_ref, kp_ref, vc_ref, vp_ref, o_ref, lse_ref):
    n = pl.program_id(1)
    qi = lax.broadcasted_iota(jnp.int32, (A_BLOCK, A_BLOCK), 0)
    ki = lax.broadcasted_iota(jnp.int32, (A_BLOCK, A_BLOCK), 1)
    cur_ok = ki <= qi
    prev_ok = (ki >= qi) & (n > 0)
    lane = lax.broadcasted_iota(jnp.int32, (A_BLOCK, LANES), 1)
    scale = A_HEAD_DIM ** -0.5
    nt = (((1,), (1,)), ((), ()))
    lse_tile = jnp.zeros((A_BLOCK, LANES), F32)
    for hh in range(A_HEADS_PER_GROUP):
        hs = slice(hh * A_HEAD_DIM, (hh + 1) * A_HEAD_DIM)
        q = q_ref[:, hs]
        s_c = lax.dot_general(q, kc_ref[:, hs], nt, preferred_element_type=F32) * scale
        s_p = lax.dot_general(q, kp_ref[:, hs], nt, preferred_element_type=F32) * scale
        s_c = jnp.where(cur_ok, s_c, -jnp.inf)
        s_p = jnp.where(prev_ok, s_p, -jnp.inf)
        mx = jnp.maximum(jnp.max(s_c, axis=-1, keepdims=True), jnp.max(s_p, axis=-1, keepdims=True))
        e_c = jnp.exp(s_c - mx)
        e_p = jnp.exp(s_p - mx)
        den = jnp.sum(e_c, axis=-1, keepdims=True) + jnp.sum(e_p, axis=-1, keepdims=True)
        pv = (jnp.dot((e_c / den).astype(BF16), vc_ref[:, hs], preferred_element_type=F32)
              + jnp.dot((e_p / den).astype(BF16), vp_ref[:, hs], preferred_element_type=F32))
        o_ref[:, hs] = pv.astype(o_ref.dtype)
        lse_tile = jnp.where(lane == hh, mx + jnp.log(den), lse_tile)
    lse_ref[...] = lse_tile


def dilated_attention(qk, v, group, dilation):
    seq = qk.shape[0]
    sub_len = seq // dilation
    nb = sub_len // A_BLOCK
    gw = A_GROUP_WIDTH
    n_groups = len(A_PATTERNS)
    qk_v = qk.reshape(sub_len, dilation * 2 * n_groups * gw)
    v_v = v.reshape(sub_len, dilation * n_groups * gw)
    blk = (A_BLOCK, gw)
    prev = lambda n: jnp.maximum(n - 1, 0)
    o, lse = pl.pallas_call(
        _attn_kernel,
        out_shape=(jax.ShapeDtypeStruct((sub_len, dilation * gw), BF16),
                   jax.ShapeDtypeStruct((sub_len, dilation * LANES), F32)),
        grid=(dilation, nb),
        in_specs=[pl.BlockSpec(blk, lambda r, n: (n, r * 2 * n_groups + group)),
                  pl.BlockSpec(blk, lambda r, n: (n, r * 2 * n_groups + n_groups + group)),
                  pl.BlockSpec(blk, lambda r, n: (prev(n), r * 2 * n_groups + n_groups + group)),
                  pl.BlockSpec(blk, lambda r, n: (n, r * n_groups + group)),
                  pl.BlockSpec(blk, lambda r, n: (prev(n), r * n_groups + group))],
        out_specs=(pl.BlockSpec(blk, lambda r, n: (n, r)),
                   pl.BlockSpec((A_BLOCK, LANES), lambda r, n: (n, r))),
        compiler_params=_params(2),
        name=f"dilated_attn_g{group}",
    )(qk_v, qk_v, qk_v, v_v, v_v)
    return o.reshape(seq, gw), lse.reshape(seq, LANES)


def _attn_merge_kernel(o0_ref, o1_ref, o2_ref, l0_ref, l1_ref, l2_ref, out_ref):
    l0, l1, l2 = l0_ref[...], l1_ref[...], l2_ref[...]
    mx = jnp.maximum(jnp.maximum(l0, l1), l2)
    w0, w1, w2 = jnp.exp(l0 - mx), jnp.exp(l1 - mx), jnp.exp(l2 - mx)
    tot = w0 + w1 + w2
    w0, w1, w2 = w0 / tot, w1 / tot, w2 / tot
    for hh in range(A_HEADS_PER_GROUP):
        hs = slice(hh * A_HEAD_DIM, (hh + 1) * A_HEAD_DIM)
        merged = (w0[:, hh:hh + 1] * o0_ref[:, hs].astype(F32)
                  + w1[:, hh:hh + 1] * o1_ref[:, hs].astype(F32)
                  + w2[:, hh:hh + 1] * o2_ref[:, hs].astype(F32))
        out_ref[:, hs] = merged.astype(out_ref.dtype)


def attention_merge(outs, lses, tm=256):
    seq, gw = outs[0].shape
    tm = min(tm, seq)
    o_spec = pl.BlockSpec((tm, gw), lambda i: (i, 0))
    l_spec = pl.BlockSpec((tm, LANES), lambda i: (i, 0))
    return pl.pallas_call(
        _attn_merge_kernel,
        out_shape=jax.ShapeDtypeStruct((seq, gw), BF16),
        grid=(seq // tm,),
        in_specs=[o_spec] * 3 + [l_spec] * 3,
        out_specs=o_spec,
        compiler_params=_params(1),
        name="attn_merge",
    )(*outs, *lses)


def dilated_attention_mixer(xn, h, w_in, q_gain, k_gain, w_out, rope):
    seq = xn.shape[0]
    n_heads = len(A_PATTERNS) * A_HEADS_PER_GROUP
    qk_width = 2 * n_heads * A_HEAD_DIM
    cos, sin = rope
    gain = jnp.concatenate([jnp.tile(q_gain, n_heads), jnp.tile(k_gain, n_heads)]).reshape(1, qk_width)
    tm, tn = min(1024, seq), 1024
    qk = fused_matmul(
        xn, w_in[:, :qk_width].astype(BF16), _ep_qk_norm_rope, out_dtype=BF16, tm=tm, tn=tn,
        extras=[(gain, (1, tn), lambda i, j: (0, j)),
                (cos, (tm, A_HEAD_DIM), lambda i, j: (i, 0)),
                (sin, (tm, A_HEAD_DIM), lambda i, j: (i, 0))],
        name="attn_qk_proj")
    v = fused_matmul(xn, w_in[:, qk_width:].astype(BF16), _ep_identity, out_dtype=BF16,
                     tm=tm, tn=tn, name="attn_v_proj")
    outs, lses = [], []
    for g, (window, dilation) in enumerate(A_PATTERNS):
        assert window // dilation == A_BLOCK and seq % (A_BLOCK * dilation) == 0
        o, lse = dilated_attention(qk, v, g, dilation)
        outs.append(o)
        lses.append(lse)
    merged = attention_merge(outs, lses)
    return matmul_residual(merged, w_out.astype(BF16), h, name="attn_out_proj")


def _ep_gates(acc, bias_ref):
    g = acc + bias_ref[...]
    lane = lax.broadcasted_iota(jnp.int32, g.shape, 1)
    return jnp.where(lane >= B_HEADS, jax.nn.log_sigmoid(g), g)


def _mlstm_kernel(q_ref, k_ref, v_ref, og_ref, gcol_ref, grow_ref, gain_ref, o_ref,
                  c_ref, n_ref, m_ref):
    hd = pl.program_id(0)
    L = q_ref.shape[0]
    hi = lax.Precision.HIGHEST

    @pl.when(pl.program_id(1) == 0)
    def _():
        c_ref[...] = jnp.zeros_like(c_ref)
        n_ref[...] = jnp.zeros_like(n_ref)
        m_ref[...] = jnp.zeros_like(m_ref)

    gcol = gcol_ref[...]
    lane = lax.broadcasted_iota(jnp.int32, gcol.shape, 1)
    ig_col = jnp.sum(jnp.where(lane == hd, gcol, 0.0), axis=-1, keepdims=True)
    lf_col = jnp.sum(jnp.where(lane == hd + B_HEADS, gcol, 0.0), axis=-1, keepdims=True)
    ig_row = grow_ref[pl.ds(hd, 1), :]
    lf_row = grow_ref[pl.ds(hd + B_HEADS, 1), :]

    t_i = lax.broadcasted_iota(jnp.int32, (L, L), 0)
    s_i = lax.broadcasted_iota(jnp.int32, (L, L), 1)
    causal = s_i <= t_i
    lower = causal.astype(F32)
    upper = (t_i <= s_i).astype(F32)
    b_col = jnp.dot(lower, jnp.broadcast_to(lf_col, (L, LANES)), precision=hi,
                    preferred_element_type=F32)[:, :1]
    b_row = jnp.dot(jnp.broadcast_to(lf_row, (8, L)), upper, precision=hi,
                    preferred_element_type=F32)[:1, :]

    m_prev = m_ref[:1, :1]
    c_prev = c_ref[...]
    n_prev = n_ref[...]

    q = q_ref[...] * jnp.asarray(B_QK_DIM ** -0.5, q_ref.dtype)
    k = k_ref[...]
    v = v_ref[...]

    log_d = jnp.where(causal, b_col - b_row + ig_row, -jnp.inf)
    log_inter = b_col + m_prev
    m_t = jnp.maximum(log_inter, jnp.max(log_d, axis=-1, keepdims=True))
    scores = lax.dot_general(q, k, (((1,), (1,)), ((), ())), preferred_element_type=F32)
    w = scores * jnp.exp(log_d - m_t)
    decay = jnp.exp(log_inter - m_t)
    num = (decay * jnp.dot(q, c_prev.astype(BF16), preferred_element_type=F32)
           + jnp.dot(w.astype(BF16), v, preferred_element_type=F32))
    qn = (decay * jnp.sum(q.astype(F32) * n_prev, axis=-1, keepdims=True)
          + jnp.sum(w, axis=-1, keepdims=True))
    hval = num / jnp.maximum(jnp.abs(qn), jnp.exp(-m_t))

    inv = lax.rsqrt(jnp.mean(hval * hval, axis=-1, keepdims=True) + RMS_EPS)
    o_ref[...] = (hval * inv * gain_ref[...] * jax.nn.sigmoid(og_ref[...].astype(F32))).astype(o_ref.dtype)

    b_last = b_col[L - 1:L, :]
    log_w = b_last - b_col + ig_col
    m_new = jnp.maximum(b_last + m_prev, jnp.max(log_w, axis=0, keepdims=True))
    w_s = jnp.exp(log_w - m_new)
    carry = jnp.exp(b_last + m_prev - m_new)
    k_w = k.astype(F32) * w_s
    c_ref[...] = carry * c_prev + lax.dot_general(
        k_w.astype(BF16), v, (((0,), (0,)), ((), ())), preferred_element_type=F32)
    n_ref[...] = carry * n_prev + jnp.sum(k_w, axis=0, keepdims=True)
    m_ref[...] = jnp.broadcast_to(m_new, m_ref.shape)


def mlstm_mixer(xn, h, w_in, gate_bias, h_gain, w_out):
    seq, d = xn.shape
    qk_w = B_HEADS * B_QK_DIM
    v_w = B_HEADS * B_V_DIM
    main_w = 2 * qk_w + v_w + d
    tm = min(1024, seq)
    proj = fused_matmul(xn, w_in[:, :main_w].astype(BF16), _ep_identity, out_dtype=BF16,
                        tm=tm, tn=1024, name="mlstm_in_proj")
    n_gate = 2 * B_HEADS
    w_gate = jnp.pad(w_in[:, main_w:], ((0, 0), (0, B_GATE_PAD - n_gate))).astype(BF16)
    bias = jnp.pad(gate_bias, (0, B_GATE_PAD - n_gate)).reshape(1, B_GATE_PAD)
    gates = fused_matmul(xn, w_gate, _ep_gates, out_dtype=F32, tm=tm, tn=B_GATE_PAD,
                         extras=[(bias, (1, B_GATE_PAD), lambda i, j: (0, 0))],
                         name="mlstm_gate_proj")
    gates_t = gates[:, :n_gate].T
    L = min(B_CHUNK, seq)
    nq = qk_w // B_QK_DIM
    nv = v_w // B_V_DIM
    hb = pl.pallas_call(
        _mlstm_kernel,
        out_shape=jax.ShapeDtypeStruct((seq, v_w), BF16),
        grid=(B_HEADS, seq // L),
        in_specs=[pl.BlockSpec((L, B_QK_DIM), lambda hd, c: (c, hd)),
                  pl.BlockSpec((L, B_QK_DIM), lambda hd, c: (c, nq + hd)),
                  pl.BlockSpec((L, B_V_DIM), lambda hd, c: (c, (2 * qk_w) // B_V_DIM + hd)),
                  pl.BlockSpec((L, B_V_DIM), lambda hd, c: (c, (2 * qk_w) // B_V_DIM + nv + hd)),
                  pl.BlockSpec((L, B_GATE_PAD), lambda hd, c: (c, 0)),
                  pl.BlockSpec((n_gate, L), lambda hd, c: (0, c)),
                  pl.BlockSpec((1, B_V_DIM), lambda hd, c: (0, hd))],
        out_specs=pl.BlockSpec((L, B_V_DIM), lambda hd, c: (c, hd)),
        scratch_shapes=[pltpu.VMEM((B_QK_DIM, B_V_DIM), F32),
                        pltpu.VMEM((1, B_QK_DIM), F32),
                        pltpu.VMEM((8, LANES), F32)],
        compiler_params=_params(2),
        name="mlstm_chunks",
    )(proj, proj, proj, proj, gates, gates_t, h_gain.reshape(1, v_w))
    return matmul_residual(hb, w_out.astype(BF16), h, name="mlstm_out_proj")


def _pool_kernel(u_ref, halo_ref, wg_ref, scale_ref, o_ref):
    g = pl.program_id(0)
    i = pl.program_id(1)
    tm = u_ref.shape[0]
    win = jnp.left_shift(jnp.int32(C_WINDOWS[0]), g)
    t_i = lax.broadcasted_iota(jnp.int32, (tm, tm), 0)
    s_i = lax.broadcasted_iota(jnp.int32, (tm, tm), 1)
    dist = t_i - s_i
    band = ((dist >= 0) & (dist < win)).astype(BF16)
    t_h = lax.broadcasted_iota(jnp.int32, (tm, C_HALO), 0)
    s_h = lax.broadcasted_iota(jnp.int32, (tm, C_HALO), 1)
    band_halo = ((t_h + C_HALO - s_h < win) & (i > 0)).astype(BF16)
    u = u_ref[...]
    total = (jnp.dot(band, u, preferred_element_type=F32)
             + jnp.dot(band_halo, halo_ref[...], preferred_element_type=F32))
    t_glob = i * tm + lax.broadcasted_iota(jnp.int32, (tm, 1), 0)
    count = jnp.minimum(t_glob + 1, win).astype(F32)
    pooled = total / count - u.astype(F32)
    y = jnp.dot(pooled.astype(BF16), wg_ref[...], preferred_element_type=F32) * scale_ref[...]
    o_ref[...] = y.astype(o_ref.dtype)


def pooling_mixer(xn, h, w_in, w_group, scale, w_out, tm=512):
    seq, d = xn.shape
    assert all(C_WINDOWS[g] == C_WINDOWS[0] << g for g in range(len(C_WINDOWS)))
    assert max(C_WINDOWS) <= C_HALO
    u = fused_matmul(xn, w_in.astype(BF16), _ep_identity, out_dtype=BF16,
                     tm=min(1024, seq), tn=1024, name="pool_in_proj")
    tm = min(tm, seq)
    gd = C_GROUP_DIM
    halo_per_tile = tm // C_HALO
    y = pl.pallas_call(
        _pool_kernel,
        out_shape=jax.ShapeDtypeStruct((seq, d), BF16),
        grid=(len(C_WINDOWS), seq // tm),
        in_specs=[pl.BlockSpec((tm, gd), lambda g, i: (i, g)),
                  pl.BlockSpec((C_HALO, gd), lambda g, i: (jnp.maximum(i * halo_per_tile - 1, 0), g)),
                  pl.BlockSpec((None, gd, gd), lambda g, i: (g, 0, 0)),
                  pl.BlockSpec((1, gd), lambda g, i: (0, g))],
        out_specs=pl.BlockSpec((tm, gd), lambda g, i: (i, g)),
        compiler_params=_params(2),
        name="pool_group",
    )(u, u, w_group.astype(BF16), scale.reshape(1, d))
    return matmul_residual(y, w_out.astype(BF16), h, name="pool_out_proj")


def squared_relu_mlp(xn, h, w_in, w_out):
    hidden = fused_matmul(xn, w_in.astype(BF16), _ep_relu_sq, out_dtype=BF16,
                          tm=min(1024, xn.shape[0]), tn=1024, name="mlp_up")
    return matmul_kgrid_residual(hidden, w_out.astype(BF16), h, name="mlp_down")


def kernel(x, p, norm_mix, norm_mlp, norm_ple, w_ple, w_ple_gate, w_mlp_in, w_mlp_out, a_w_in, a_q_norm, a_k_norm, a_w_out, b_w_in, b_gate_bias, b_h_norm, b_w_out, c_w_in, c_w_group, c_scale, c_w_out):
    bsz, seq, d = x.shape
    depth = p.shape[0]
    rope = rope_tables(seq)
    outs = []
    for b in range(bsz):
        h = x[b]
        for i in range(depth):
            kind, j = i % 3, i // 3
            xn = rmsnorm_bf16(h, norm_mix[i])
            if kind == 0:
                h = dilated_attention_mixer(xn, h, a_w_in[j], a_q_norm[j], a_k_norm[j], a_w_out[j], rope)
            elif kind == 1:
                h = mlstm_mixer(xn, h, b_w_in[j], b_gate_bias[j], b_h_norm[j], b_w_out[j])
            else:
                h = pooling_mixer(xn, h, c_w_in[j], c_w_group[j], c_scale[j], c_w_out[j])
            h = squared_relu_mlp(rmsnorm_bf16(h, norm_mlp[i]), h, w_mlp_in[i], w_mlp_out[i])
            h = ple_update(rmsnorm_bf16(h, norm_ple[i]), w_ple_gate[i].astype(BF16),
                           p[i, b].astype(BF16), w_ple[i].astype(BF16), h)
        outs.append(h)
    return outs[0][None] if bsz == 1 else jnp.stack(outs, axis=0)
```

```python
import functools
import math

import jax
import jax.numpy as jnp
from jax import lax
from jax.experimental import pallas as pl
from jax.experimental.pallas import tpu as pltpu

F32 = jnp.float32
BF16 = jnp.bfloat16

RMS_EPS = 1e-6
ROPE_THETA = 10000.0

A_HEAD_DIM = 128
A_HEADS_PER_GROUP = 16
A_PATTERNS = ((128, 1), (512, 4), (2048, 16))
A_STEPS = 128
A_RESIDUES = 16
A_GROUP_WIDTH = A_HEADS_PER_GROUP * A_HEAD_DIM

B_HEADS = 8
B_QK_DIM = 256
B_V_DIM = 512
B_CHUNK = 256
B_GATE_PAD = 128

C_WINDOWS = (2, 4, 8, 16)
C_GROUP_DIM = 1024
C_HALO = 128

LANES = 128
VMEM_LIMIT_BYTES = 56 * 1024 * 1024


def _params(n_grid_axes):
    return pltpu.CompilerParams(
        dimension_semantics=("arbitrary",) * n_grid_axes,
        vmem_limit_bytes=VMEM_LIMIT_BYTES)


def _rmsnorm_rows(x, gain):
    inv = lax.rsqrt(jnp.mean(x * x, axis=-1, keepdims=True) + RMS_EPS)
    return x * inv * gain


def _rmsnorm_kernel(x_ref, g_ref, o_ref):
    o_ref[...] = _rmsnorm_rows(x_ref[...], g_ref[...]).astype(o_ref.dtype)


def _row_permutation(tm, rows, to_residue_major):
    a = lax.broadcasted_iota(jnp.int32, (tm, tm), 0)
    b = lax.broadcasted_iota(jnp.int32, (tm, tm), 1)
    major, seq_order = (a, b) if to_residue_major else (b, a)
    return (seq_order == (major % rows) * A_RESIDUES + major // rows).astype(BF16)


def _rmsnorm_residue_major_kernel(x_ref, g_ref, o_ref):
    tm = x_ref.shape[0]
    rows = tm // A_RESIDUES
    y = _rmsnorm_rows(x_ref[...], g_ref[...]).astype(BF16)
    y = jnp.dot(_row_permutation(tm, rows, True), y, preferred_element_type=F32)
    o_ref[...] = y.astype(o_ref.dtype).reshape(o_ref.shape)


def rmsnorm_bf16(h, gain, tm=256, residue_major=False):
    m, d = h.shape
    tm = min(tm, m)
    in_specs = [pl.BlockSpec((tm, d), lambda i: (i, 0)),
                pl.BlockSpec((1, d), lambda i: (0, 0))]
    if not residue_major:
        return pl.pallas_call(
            _rmsnorm_kernel,
            out_shape=jax.ShapeDtypeStruct((m, d), BF16),
            grid=(m // tm,), in_specs=in_specs,
            out_specs=pl.BlockSpec((tm, d), lambda i: (i, 0)),
            compiler_params=_params(1), name="rmsnorm",
        )(h, gain.reshape(1, d))
    out = pl.pallas_call(
        _rmsnorm_residue_major_kernel,
        out_shape=jax.ShapeDtypeStruct((A_RESIDUES, m // A_RESIDUES, d), BF16),
        grid=(m // tm,), in_specs=in_specs,
        out_specs=pl.BlockSpec((A_RESIDUES, tm // A_RESIDUES, d), lambda i: (0, i, 0)),
        compiler_params=_params(1), name="rmsnorm_residue_major",
    )(h, gain.reshape(1, d))
    return out.reshape(m, d)


def _mm_kernel(x_ref, w_ref, *rest, epilogue, n_extra):
    extra_refs = rest[:n_extra]
    o_ref, wb_ref = rest[n_extra], rest[n_extra + 1]

    @pl.when(pl.program_id(1) == 0)
    def _():
        wb_ref[...] = w_ref[...].astype(BF16)

    acc = jnp.dot(x_ref[...], wb_ref[...], preferred_element_type=F32)
    o_ref[...] = epilogue(acc, slice(None), *extra_refs).astype(o_ref.dtype)


def fused_matmul(x, w, layer, col0, n, epilogue, *, out_dtype, tm=1024, tn=512, extras=(), name):
    m, k = x.shape
    tm, tn = min(tm, m), min(tn, n)
    assert w.shape[1] == k and m % tm == 0 and n % tn == 0 and col0 % tn == 0, (w.shape, m, n, tm, tn, col0)
    jb = col0 // tn
    in_specs = [pl.BlockSpec((tm, k), lambda j, i: (i, 0)),
                pl.BlockSpec((None, k, tn), lambda j, i: (layer, 0, jb + j))]
    in_specs += [pl.BlockSpec(bs, im) for _, bs, im in extras]
    return pl.pallas_call(
        functools.partial(_mm_kernel, epilogue=epilogue, n_extra=len(extras)),
        out_shape=jax.ShapeDtypeStruct((m, n), out_dtype),
        grid=(n // tn, m // tm),
        in_specs=in_specs,
        out_specs=pl.BlockSpec((tm, tn), lambda j, i: (i, j)),
        scratch_shapes=[pltpu.VMEM((k, tn), BF16)],
        compiler_params=_params(2),
        name=name,
    )(x, w, *[a for a, _, _ in extras])


def _ep_identity(acc, cs):
    return acc


def _ep_relu_sq(acc, cs):
    r = jnp.maximum(acc, 0.0)
    return r * r


def _ep_residual(acc, cs, res_ref):
    return res_ref[:, cs] + acc


def matmul_residual(x, w, layer, res, *, tm=1024, tn=512, name):
    n = w.shape[2]
    tm, tn = min(tm, x.shape[0]), min(tn, n)
    return fused_matmul(x, w, layer, 0, n, _ep_residual, out_dtype=F32, tm=tm, tn=tn,
                        extras=[(res, (tm, tn), lambda j, i: (i, j))], name=name)


def _mm_kgrid_kernel(x_ref, w_ref, res_ref, o_ref, acc_ref):
    kk = pl.program_id(2)

    @pl.when(kk == 0)
    def _():
        acc_ref[...] = jnp.zeros_like(acc_ref)

    acc_ref[...] += jnp.dot(x_ref[...], w_ref[...], preferred_element_type=F32)

    @pl.when(kk == pl.num_programs(2) - 1)
    def _():
        o_ref[...] = res_ref[...] + acc_ref[...]


def matmul_kgrid_residual(x, w, res, *, tm=1024, tn=1024, tk=2048, name):
    m, k = x.shape
    n = w.shape[1]
    tm, tn, tk = min(tm, m), min(tn, n), min(tk, k)
    assert m % tm == 0 and n % tn == 0 and k % tk == 0
    return pl.pallas_call(
        _mm_kgrid_kernel,
        out_shape=jax.ShapeDtypeStruct((m, n), F32),
        grid=(m // tm, n // tn, k // tk),
        in_specs=[pl.BlockSpec((tm, tk), lambda i, j, kk: (i, kk)),
                  pl.BlockSpec((tk, tn), lambda i, j, kk: (kk, j)),
                  pl.BlockSpec((tm, tn), lambda i, j, kk: (i, j))],
        out_specs=pl.BlockSpec((tm, tn), lambda i, j, kk: (i, j)),
        scratch_shapes=[pltpu.VMEM((tm, tn), F32)],
        compiler_params=_params(3),
        name=name,
    )(x, w, res)


def _ep_ple(acc, cs, p_ref, wp_ref, res_ref):
    emb = jnp.dot(p_ref[...], wp_ref[:, cs], preferred_element_type=F32)
    return res_ref[:, cs] + jax.nn.sigmoid(acc) * emb


def ple_update(xn, w_gate, layer, p, w_ple, res, *, tm=1024, tn=512):
    n = w_gate.shape[2]
    tm, tn = min(tm, xn.shape[0]), min(tn, n)
    kp = p.shape[1]
    return fused_matmul(
        xn, w_gate, layer, 0, n, _ep_ple, out_dtype=F32, tm=tm, tn=tn,
        extras=[(p, (tm, kp), lambda j, i: (i, 0)),
                (w_ple, (kp, tn), lambda j, i: (0, j)),
                (res, (tm, tn), lambda j, i: (i, j))],
        name="ple_gate")


def _rope_table_kernel(cos_ref, sin_ref, *, rows_per_residue):
    tm = cos_ref.shape[0]
    half = A_HEAD_DIM // 2
    row0 = pl.program_id(0) * tm
    residue = row0 // rows_per_residue
    n0 = row0 % rows_per_residue
    n = n0 + lax.broadcasted_iota(jnp.int32, (tm, A_HEAD_DIM), 0)
    pos = (n * A_RESIDUES + residue).astype(F32)
    lane = lax.broadcasted_iota(jnp.int32, (tm, A_HEAD_DIM), 1)
    j = jnp.where(lane >= half, lane - half, lane).astype(F32)
    inv_freq = jnp.exp(j * (-2.0 * math.log(ROPE_THETA) / A_HEAD_DIM))
    ang = pos * inv_freq
    cos_ref[...] = jnp.cos(ang)
    sin_ref[...] = jnp.where(lane >= half, 1.0, -1.0) * jnp.sin(ang)


def rope_tables(seq, tm=256):
    rows_per_residue = seq // A_RESIDUES
    tm = min(tm, rows_per_residue)
    assert rows_per_residue % tm == 0
    spec = pl.BlockSpec((tm, A_HEAD_DIM), lambda i: (i, 0))
    shape = jax.ShapeDtypeStruct((seq, A_HEAD_DIM), F32)
    return pl.pallas_call(
        functools.partial(_rope_table_kernel, rows_per_residue=rows_per_residue),
        out_shape=(shape, shape), grid=(seq // tm,),
        in_specs=[], out_specs=(spec, spec), compiler_params=_params(1),
        name="rope_tables")()


def _ep_qk_norm_rope(acc, cs, gain_ref, cos_ref, sin_ref):
    cos, sin = cos_ref[...], sin_ref[...]
    gain = gain_ref[:, cs]
    d = A_HEAD_DIM
    ones = jnp.ones((d, d), BF16)
    r_i = lax.broadcasted_iota(jnp.int32, (d, d), 0)
    c_i = lax.broadcasted_iota(jnp.int32, (d, d), 1)
    swap_halves = (r_i == (c_i + d // 2) % d).astype(BF16)
    outs = []
    for hh in range(acc.shape[1] // d):
        hs = slice(hh * d, (hh + 1) * d)
        a = acc[:, hs]
        sum_sq = jnp.dot((a * a).astype(BF16), ones, preferred_element_type=F32)
        inv = lax.rsqrt(sum_sq * (1.0 / d) + RMS_EPS)
        t = a * gain[:, hs]
        t_swapped = jnp.dot(t.astype(BF16), swap_halves, preferred_element_type=F32)
        outs.append((t * cos + t_swapped * sin) * inv)
    return jnp.concatenate(outs, axis=1)


def _attn_kernel(q_ref, kc_ref, kp_ref, vc_ref, vp_ref, o_ref, lse_ref, *, parts):
    rows = q_ref.shape[-2]
    qb = parts * rows
    n = pl.program_id(1)

    def load(ref, hs):
        return ref[..., hs].reshape(qb, A_HEAD_DIM)

    def step_of(idx):
        return idx if parts == 1 else (idx % rows) * parts + idx // rows

    lq = step_of(lax.broadcasted_iota(jnp.int32, (qb, 2 * qb), 0))
    col = lax.broadcasted_iota(jnp.int32, (qb, 2 * qb), 1)
    is_prev = col < qb
    lk = step_of(jnp.where(is_prev, col, col - qb))
    dist = lq - lk + jnp.where(is_prev, qb, 0)
    ok = (dist >= 0) & (dist <= A_STEPS) & ((n > 0) | jnp.logical_not(is_prev))
    lane = lax.broadcasted_iota(jnp.int32, (qb, LANES), 1)
    scale = A_HEAD_DIM ** -0.5
    nt = (((1,), (1,)), ((), ()))
    ones = jnp.ones((2 * qb, A_HEAD_DIM), BF16)
    lse_tile = jnp.zeros((qb, LANES), F32)
    for hh in range(A_HEADS_PER_GROUP):
        hs = slice(hh * A_HEAD_DIM, (hh + 1) * A_HEAD_DIM)
        q = load(q_ref, hs)
        k_cat = jnp.concatenate([load(kp_ref, hs), load(kc_ref, hs)], axis=0)
        v_cat = jnp.concatenate([load(vp_ref, hs), load(vc_ref, hs)], axis=0)
        s = lax.dot_general(q, k_cat, nt, preferred_element_type=F32) * scale
        s = jnp.where(ok, s, -jnp.inf)
        mx = jnp.max(jnp.maximum(s[:, :qb], s[:, qb:]), axis=-1, keepdims=True)
        e = jnp.exp(s - mx).astype(BF16)
        pv = jnp.dot(e, jnp.concatenate([v_cat, ones], axis=1), preferred_element_type=F32)
        den = pv[:, A_HEAD_DIM:]
        o_ref[..., hs] = (pv[:, :A_HEAD_DIM] / den).reshape(o_ref.shape[:-1] + (A_HEAD_DIM,)).astype(o_ref.dtype)
        lse_tile = jnp.where(lane == hh, mx + jnp.log(den), lse_tile)
    lse_ref[...] = lse_tile.reshape(lse_ref.shape)


def dilated_attention(qk, v, group, dilation):
    seq = qk.shape[0]
    gw = A_GROUP_WIDTH
    n_groups = len(A_PATTERNS)
    parts = A_RESIDUES // dilation
    rpr = seq // A_RESIDUES
    rows = max(A_STEPS // parts, 16)
    nb = rpr // rows
    qk_w, v_w = qk.shape[1], v.shape[1]
    if parts == 1:
        vshape = lambda w: (seq, w)
        blk = lambda w: (rows, w)
        at = lambda r, n, c: (r * nb + n, c)
    else:
        vshape = lambda w: (parts, dilation, rpr, w)
        blk = lambda w: (parts, None, rows, w)
        at = lambda r, n, c: (0, r, n, c)
    view = lambda a: a.reshape(vshape(a.shape[1]))
    prev = lambda n: jnp.maximum(n - 1, 0)
    o, lse = pl.pallas_call(
        functools.partial(_attn_kernel, parts=parts),
        out_shape=(jax.ShapeDtypeStruct(vshape(gw), BF16),
                   jax.ShapeDtypeStruct(vshape(LANES), F32)),
        grid=(dilation, nb),
        in_specs=[pl.BlockSpec(blk(gw), lambda r, n: at(r, n, group)),
                  pl.BlockSpec(blk(gw), lambda r, n: at(r, n, n_groups + group)),
                  pl.BlockSpec(blk(gw), lambda r, n: at(r, prev(n), n_groups + group)),
                  pl.BlockSpec(blk(gw), lambda r, n: at(r, n, group)),
                  pl.BlockSpec(blk(gw), lambda r, n: at(r, prev(n), group))],
        out_specs=(pl.BlockSpec(blk(gw), lambda r, n: at(r, n, 0)),
                   pl.BlockSpec(blk(LANES), lambda r, n: at(r, n, 0))),
        compiler_params=_params(2),
        name=f"dilated_attn_g{group}",
    )(view(qk), view(qk), view(qk), view(v), view(v))
    return o.reshape(seq, gw), lse.reshape(seq, LANES)


def _attn_merge_kernel(o0_ref, o1_ref, o2_ref, l0_ref, l1_ref, l2_ref, out_ref):
    rows = o0_ref.shape[1]
    tm = A_RESIDUES * rows

    def flat(ref, sl=slice(None)):
        return ref[:, :, sl].reshape(tm, -1)

    l0, l1, l2 = flat(l0_ref), flat(l1_ref), flat(l2_ref)
    mx = jnp.maximum(jnp.maximum(l0, l1), l2)
    w0, w1, w2 = jnp.exp(l0 - mx), jnp.exp(l1 - mx), jnp.exp(l2 - mx)
    tot = w0 + w1 + w2
    w0, w1, w2 = w0 / tot, w1 / tot, w2 / tot
    to_sequence_order = _row_permutation(tm, rows, False)
    for hh in range(A_HEADS_PER_GROUP):
        hs = slice(hh * A_HEAD_DIM, (hh + 1) * A_HEAD_DIM)
        merged = (w0[:, hh:hh + 1] * flat(o0_ref, hs).astype(F32)
                  + w1[:, hh:hh + 1] * flat(o1_ref, hs).astype(F32)
                  + w2[:, hh:hh + 1] * flat(o2_ref, hs).astype(F32)).astype(BF16)
        out_ref[:, hs] = jnp.dot(to_sequence_order, merged, preferred_element_type=F32).astype(out_ref.dtype)


def attention_merge(outs, lses, rows=16):
    seq, gw = outs[0].shape
    rpr = seq // A_RESIDUES
    rows = min(rows, rpr)
    tm = A_RESIDUES * rows
    o_spec = pl.BlockSpec((A_RESIDUES, rows, gw), lambda i: (0, i, 0))
    l_spec = pl.BlockSpec((A_RESIDUES, rows, LANES), lambda i: (0, i, 0))
    return pl.pallas_call(
        _attn_merge_kernel,
        out_shape=jax.ShapeDtypeStruct((seq, gw), BF16),
        grid=(rpr // rows,),
        in_specs=[o_spec] * 3 + [l_spec] * 3,
        out_specs=pl.BlockSpec((tm, gw), lambda i: (i, 0)),
        compiler_params=_params(1),
        name="attn_merge",
    )(*[o.reshape(A_RESIDUES, rpr, gw) for o in outs],
      *[l.reshape(A_RESIDUES, rpr, LANES) for l in lses])


def dilated_attention_mixer(h, norm_gain, w_in, layer, q_gain, k_gain, w_out, rope):
    seq = h.shape[0]
    n_heads = len(A_PATTERNS) * A_HEADS_PER_GROUP
    qk_width = 2 * n_heads * A_HEAD_DIM
    v_width = n_heads * A_HEAD_DIM
    assert seq % (A_RESIDUES * A_STEPS) == 0
    assert all(w // d == A_STEPS and A_RESIDUES % d == 0 for w, d in A_PATTERNS)
    xn = rmsnorm_bf16(h, norm_gain, residue_major=True)
    cos, sin = rope
    gain = jnp.concatenate([jnp.tile(q_gain, n_heads), jnp.tile(k_gain, n_heads)]).reshape(1, qk_width)
    tm, tn = min(1024, seq), 512
    qk = fused_matmul(
        xn, w_in, layer, 0, qk_width, _ep_qk_norm_rope, out_dtype=BF16, tm=tm, tn=tn,
        extras=[(gain, (1, tn), lambda j, i: (0, j)),
                (cos, (tm, A_HEAD_DIM), lambda j, i: (i, 0)),
                (sin, (tm, A_HEAD_DIM), lambda j, i: (i, 0))],
        name="attn_qk_proj")
    v = fused_matmul(xn, w_in, layer, qk_width, v_width, _ep_identity, out_dtype=BF16,
                     tm=tm, tn=tn, name="attn_v_proj")
    outs, lses = [], []
    for g, (_, dilation) in enumerate(A_PATTERNS):
        o, lse = dilated_attention(qk, v, g, dilation)
        outs.append(o)
        lses.append(lse)
    merged = attention_merge(outs, lses)
    return matmul_residual(merged, w_out, layer, h, name="attn_out_proj")


def _ep_gates(acc, cs, bias_ref):
    g = acc + bias_ref[...]
    lane = lax.broadcasted_iota(jnp.int32, g.shape, 1)
    return jnp.where(lane >= B_HEADS, jax.nn.log_sigmoid(g), g)


def _mlstm_kernel(q_ref, k_ref, v_ref, og_ref, gcol_ref, grow_ref, gain_ref, o_ref,
                  c_ref, n_ref, m_ref):
    hd = pl.program_id(0)
    L = q_ref.shape[0]
    hi = lax.Precision.HIGHEST

    @pl.when(pl.program_id(1) == 0)
    def _():
        c_ref[...] = jnp.zeros_like(c_ref)
        n_ref[...] = jnp.zeros_like(n_ref)
        m_ref[...] = jnp.zeros_like(m_ref)

    gcol = gcol_ref[...]
    lane = lax.broadcasted_iota(jnp.int32, gcol.shape, 1)
    ig_col = jnp.sum(jnp.where(lane == hd, gcol, 0.0), axis=-1, keepdims=True)
    lf_col = jnp.sum(jnp.where(lane == hd + B_HEADS, gcol, 0.0), axis=-1, keepdims=True)
    ig_row = grow_ref[pl.ds(hd, 1), :]
    lf_row = grow_ref[pl.ds(hd + B_HEADS, 1), :]

    t_i = lax.broadcasted_iota(jnp.int32, (L, L), 0)
    s_i = lax.broadcasted_iota(jnp.int32, (L, L), 1)
    causal = s_i <= t_i
    lower = causal.astype(F32)
    upper = (t_i <= s_i).astype(F32)
    b_col = jnp.dot(lower, jnp.broadcast_to(lf_col, (L, LANES)), precision=hi,
                    preferred_element_type=F32)[:, :1]
    b_row = jnp.dot(jnp.broadcast_to(lf_row, (8, L)), upper, precision=hi,
                    preferred_element_type=F32)[:1, :]

    m_prev = m_ref[:1, :1]
    c_prev = c_ref[...]
    n_prev = n_ref[...]

    q = q_ref[...] * jnp.asarray(B_QK_DIM ** -0.5, q_ref.dtype)
    k = k_ref[...]
    v = v_ref[...]

    log_d = jnp.where(causal, b_col - b_row + ig_row, -jnp.inf)
    log_inter = b_col + m_prev
    m_t = jnp.maximum(log_inter, jnp.max(log_d, axis=-1, keepdims=True))
    scores = lax.dot_general(q, k, (((1,), (1,)), ((), ())), preferred_element_type=F32)
    w = scores * jnp.exp(log_d - m_t)
    decay = jnp.exp(log_inter - m_t)
    num = (decay * jnp.dot(q, c_prev.astype(BF16), preferred_element_type=F32)
           + jnp.dot(w.astype(BF16), v, preferred_element_type=F32))
    qn = (decay * jnp.sum(q.astype(F32) * n_prev, axis=-1, keepdims=True)
          + jnp.sum(w, axis=-1, keepdims=True))
    hval = num / jnp.maximum(jnp.abs(qn), jnp.exp(-m_t))

    inv = lax.rsqrt(jnp.mean(hval * hval, axis=-1, keepdims=True) + RMS_EPS)
    o_ref[...] = (hval * inv * gain_ref[...] * jax.nn.sigmoid(og_ref[...].astype(F32))).astype(o_ref.dtype)

    b_last = b_col[L - 1:L, :]
    log_w = b_last - b_col + ig_col
    m_new = jnp.maximum(b_last + m_prev, jnp.max(log_w, axis=0, keepdims=True))
    w_s = jnp.exp(log_w - m_new)
    carry = jnp.exp(b_last + m_prev - m_new)
    k_w = k.astype(F32) * w_s
    c_ref[...] = carry * c_prev + lax.dot_general(
        k_w.astype(BF16), v, (((0,), (0,)), ((), ())), preferred_element_type=F32)
    n_ref[...] = carry * n_prev + jnp.sum(k_w, axis=0, keepdims=True)
    m_ref[...] = jnp.broadcast_to(m_new, m_ref.shape)


def mlstm_mixer(h, norm_gain, w_in, layer, gate_bias, h_gain, w_out):
    seq, d = h.shape
    xn = rmsnorm_bf16(h, norm_gain)
    qk_w = B_HEADS * B_QK_DIM
    v_w = B_HEADS * B_V_DIM
    main_w = 2 * qk_w + v_w + d
    proj = fused_matmul(xn, w_in, layer, 0, main_w, _ep_identity, out_dtype=BF16,
                        name="mlstm_in_proj")
    n_gate = 2 * B_HEADS
    w_gate = jnp.pad(w_in[layer, :, main_w:], ((0, 0), (0, B_GATE_PAD - n_gate)))[None]
    bias = jnp.pad(gate_bias, (0, B_GATE_PAD - n_gate)).reshape(1, B_GATE_PAD)
    gates = fused_matmul(xn, w_gate, 0, 0, B_GATE_PAD, _ep_gates, out_dtype=F32, tn=B_GATE_PAD,
                         extras=[(bias, (1, B_GATE_PAD), lambda j, i: (0, 0))],
                         name="mlstm_gate_proj")
    gates_t = gates[:, :n_gate].T
    L = min(B_CHUNK, seq)
    nq = qk_w // B_QK_DIM
    nv = v_w // B_V_DIM
    hb = pl.pallas_call(
        _mlstm_kernel,
        out_shape=jax.ShapeDtypeStruct((seq, v_w), BF16),
        grid=(B_HEADS, seq // L),
        in_specs=[pl.BlockSpec((L, B_QK_DIM), lambda hd, c: (c, hd)),
                  pl.BlockSpec((L, B_QK_DIM), lambda hd, c: (c, nq + hd)),
                  pl.BlockSpec((L, B_V_DIM), lambda hd, c: (c, (2 * qk_w) // B_V_DIM + hd)),
                  pl.BlockSpec((L, B_V_DIM), lambda hd, c: (c, (2 * qk_w) // B_V_DIM + nv + hd)),
                  pl.BlockSpec((L, B_GATE_PAD), lambda hd, c: (c, 0)),
                  pl.BlockSpec((n_gate, L), lambda hd, c: (0, c)),
                  pl.BlockSpec((1, B_V_DIM), lambda hd, c: (0, hd))],
        out_specs=pl.BlockSpec((L, B_V_DIM), lambda hd, c: (c, hd)),
        scratch_shapes=[pltpu.VMEM((B_QK_DIM, B_V_DIM), F32),
                        pltpu.VMEM((1, B_QK_DIM), F32),
                        pltpu.VMEM((8, LANES), F32)],
        compiler_params=_params(2),
        name="mlstm_chunks",
    )(proj, proj, proj, proj, gates, gates_t, h_gain.reshape(1, v_w))
    return matmul_residual(hb, w_out, layer, h, name="mlstm_out_proj")


def _pool_kernel(u_ref, halo_ref, wg_ref, scale_ref, o_ref, wb_ref):
    g = pl.program_id(0)
    i = pl.program_id(1)
    tm = u_ref.shape[0]

    @pl.when(i == 0)
    def _():
        wb_ref[...] = wg_ref[...].astype(BF16)

    win = jnp.left_shift(jnp.int32(C_WINDOWS[0]), g)
    t_i = lax.broadcasted_iota(jnp.int32, (tm, tm), 0)
    s_i = lax.broadcasted_iota(jnp.int32, (tm, tm), 1)
    dist = t_i - s_i
    band = ((dist >= 0) & (dist < win)).astype(BF16)
    t_h = lax.broadcasted_iota(jnp.int32, (tm, C_HALO), 0)
    s_h = lax.broadcasted_iota(jnp.int32, (tm, C_HALO), 1)
    band_halo = ((t_h + C_HALO - s_h < win) & (i > 0)).astype(BF16)
    u = u_ref[...]
    total = (jnp.dot(band, u, preferred_element_type=F32)
             + jnp.dot(band_halo, halo_ref[...], preferred_element_type=F32))
    t_glob = i * tm + lax.broadcasted_iota(jnp.int32, (tm, 1), 0)
    count = jnp.minimum(t_glob + 1, win).astype(F32)
    pooled = total / count - u.astype(F32)
    y = jnp.dot(pooled.astype(BF16), wb_ref[...], preferred_element_type=F32) * scale_ref[...]
    o_ref[...] = y.astype(o_ref.dtype)


def pooling_mixer(h, norm_gain, w_in, layer, w_group, scale, w_out, tm=512):
    seq, d = h.shape
    assert all(C_WINDOWS[g] == C_WINDOWS[0] << g for g in range(len(C_WINDOWS)))
    assert max(C_WINDOWS) <= C_HALO
    xn = rmsnorm_bf16(h, norm_gain)
    u = fused_matmul(xn, w_in, layer, 0, d, _ep_identity, out_dtype=BF16, name="pool_in_proj")
    tm = min(tm, seq)
    gd = C_GROUP_DIM
    halo_per_tile = tm // C_HALO
    y = pl.pallas_call(
        _pool_kernel,
        out_shape=jax.ShapeDtypeStruct((seq, d), BF16),
        grid=(len(C_WINDOWS), seq // tm),
        in_specs=[pl.BlockSpec((tm, gd), lambda g, i: (i, g)),
                  pl.BlockSpec((C_HALO, gd), lambda g, i: (jnp.maximum(i * halo_per_tile - 1, 0), g)),
                  pl.BlockSpec((None, None, gd, gd), lambda g, i: (layer, g, 0, 0)),
                  pl.BlockSpec((1, gd), lambda g, i: (0, g))],
        out_specs=pl.BlockSpec((tm, gd), lambda g, i: (i, g)),
        scratch_shapes=[pltpu.VMEM((gd, gd), BF16)],
        compiler_params=_params(2),
        name="pool_group",
    )(u, u, w_group, scale.reshape(1, d))
    return matmul_residual(y, w_out, layer, h, name="pool_out_proj")


def squared_relu_mlp(h, norm_gain, w_in, w_out_bf16, layer):
    xn = rmsnorm_bf16(h, norm_gain)
    hidden = fused_matmul(xn, w_in, layer, 0, w_in.shape[2], _ep_relu_sq, out_dtype=BF16, name="mlp_up")
    return matmul_kgrid_residual(hidden, w_out_bf16, h, name="mlp_down")


def kernel(x, p, norm_mix, norm_mlp, norm_ple, w_ple, w_ple_gate, w_mlp_in, w_mlp_out, a_w_in, a_q_norm, a_k_norm, a_w_out, b_w_in, b_gate_bias, b_h_norm, b_w_out, c_w_in, c_w_group, c_scale, c_w_out):
    bsz, seq, d = x.shape
    depth = p.shape[0]
    rope = rope_tables(seq)
    outs = []
    for b in range(bsz):
        h = x[b]
        for i in range(depth):
            kind, j = i % 3, i // 3
            if kind == 0:
                h = dilated_attention_mixer(h, norm_mix[i], a_w_in, j, a_q_norm[j], a_k_norm[j], a_w_out, rope)
            elif kind == 1:
                h = mlstm_mixer(h, norm_mix[i], b_w_in, j, b_gate_bias[j], b_h_norm[j], b_w_out)
            else:
                h = pooling_mixer(h, norm_mix[i], c_w_in, j, c_w_group, c_scale[j], c_w_out)
            h = squared_relu_mlp(h, norm_mlp[i], w_mlp_in, w_mlp_out[i].astype(BF16), i)
            h = ple_update(rmsnorm_bf16(h, norm_ple[i]), w_ple_gate, i,
                           p[i, b].astype(BF16), w_ple[i].astype(BF16), h)
        outs.append(h)
    return outs[0][None] if bsz == 1 else jnp.stack(outs, axis=0)
```

```python
import functools
import math

import jax
import jax.numpy as jnp
from jax import lax
from jax.experimental import pallas as pl
from jax.experimental.pallas import tpu as pltpu

F32 = jnp.float32
BF16 = jnp.bfloat16

RMS_EPS = 1e-6
ROPE_THETA = 10000.0

A_HEAD_DIM = 128
A_HEADS_PER_GROUP = 16
A_PATTERNS = ((128, 1), (512, 4), (2048, 16))
A_STEPS = 128
A_RESIDUES = 16
A_GROUP_WIDTH = A_HEADS_PER_GROUP * A_HEAD_DIM

B_HEADS = 8
B_QK_DIM = 256
B_V_DIM = 512
B_CHUNK = 256
B_GATE_PAD = 128

C_WINDOWS = (2, 4, 8, 16)
C_GROUP_DIM = 1024
C_HALO = 128

LANES = 128
VMEM_LIMIT_BYTES = 60 * 1024 * 1024


def _params(n_grid_axes):
    return pltpu.CompilerParams(
        dimension_semantics=("arbitrary",) * n_grid_axes,
        vmem_limit_bytes=VMEM_LIMIT_BYTES)


def _rmsnorm_rows(x, gain):
    inv = lax.rsqrt(jnp.mean(x * x, axis=-1, keepdims=True) + RMS_EPS)
    return x * inv * gain


def _rmsnorm_kernel(x_ref, g_ref, o_ref):
    o_ref[...] = _rmsnorm_rows(x_ref[...], g_ref[...]).astype(o_ref.dtype)


def _row_permutation(tm, rows, to_residue_major):
    a = lax.broadcasted_iota(jnp.int32, (tm, tm), 0)
    b = lax.broadcasted_iota(jnp.int32, (tm, tm), 1)
    major, seq_order = (a, b) if to_residue_major else (b, a)
    return (seq_order == (major % rows) * A_RESIDUES + major // rows).astype(BF16)


def _rmsnorm_residue_major_kernel(x_ref, g_ref, o_ref):
    tm = x_ref.shape[0]
    rows = tm // A_RESIDUES
    y = _rmsnorm_rows(x_ref[...], g_ref[...]).astype(BF16)
    y = jnp.dot(_row_permutation(tm, rows, True), y, preferred_element_type=F32)
    o_ref[...] = y.astype(o_ref.dtype).reshape(o_ref.shape)


def rmsnorm_bf16(h, gain, tm=256, residue_major=False):
    m, d = h.shape
    tm = min(tm, m)
    in_specs = [pl.BlockSpec((tm, d), lambda i: (i, 0)),
                pl.BlockSpec((1, d), lambda i: (0, 0))]
    if not residue_major:
        return pl.pallas_call(
            _rmsnorm_kernel,
            out_shape=jax.ShapeDtypeStruct((m, d), BF16),
            grid=(m // tm,), in_specs=in_specs,
            out_specs=pl.BlockSpec((tm, d), lambda i: (i, 0)),
            compiler_params=_params(1), name="rmsnorm",
        )(h, gain.reshape(1, d))
    out = pl.pallas_call(
        _rmsnorm_residue_major_kernel,
        out_shape=jax.ShapeDtypeStruct((A_RESIDUES, m // A_RESIDUES, d), BF16),
        grid=(m // tm,), in_specs=in_specs,
        out_specs=pl.BlockSpec((A_RESIDUES, tm // A_RESIDUES, d), lambda i: (0, i, 0)),
        compiler_params=_params(1), name="rmsnorm_residue_major",
    )(h, gain.reshape(1, d))
    return out.reshape(m, d)


SIDE_CAST_ROWS = 64


def _mm_kernel(x_ref, wchunk_ref, *rest, epilogue, n_extra, n_slabs, n_side_steps):
    extra_refs = rest[:n_extra]
    rest = rest[n_extra:]
    if n_side_steps:
        side_ref, o_ref, side_out_ref, wb_ref = rest
    else:
        o_ref, wb_ref = rest
    jj, i = pl.program_id(0), pl.program_id(1)
    chunk = wchunk_ref.shape[0]

    @pl.when(jj < n_slabs)
    def _():
        row0 = pl.multiple_of(i * chunk, chunk)
        wb_ref[jj % 2, pl.ds(row0, chunk), :] = wchunk_ref[...].astype(BF16)

    @pl.when(jj > 0)
    def _():
        acc = jnp.dot(x_ref[...], wb_ref[(jj - 1) % 2], preferred_element_type=F32)
        o_ref[...] = epilogue(acc, slice(None), *extra_refs).astype(o_ref.dtype)

    if n_side_steps:
        @pl.when(jj * pl.num_programs(1) + i < n_side_steps)
        def _():
            side_out_ref[...] = side_ref[...].astype(BF16)


def fused_matmul(x, w, layer, col0, n, epilogue, *, out_dtype, tm=1024, tn=1024, extras=(), side=None, name):
    m, k = x.shape
    tm, tn = min(tm, m), min(tn, n)
    n_slabs, n_rows = n // tn, m // tm
    assert w.shape[1] == k and m % tm == 0 and n % tn == 0 and col0 % tn == 0 and k % n_rows == 0, (
        w.shape, m, n, tm, tn, col0)
    jb = col0 // tn
    chunk = k // n_rows

    def lag(im):
        return lambda jj, i: im(jnp.maximum(jj - 1, 0), jnp.where(jj > 0, i, 0))

    in_specs = [pl.BlockSpec((tm, k), lag(lambda j, i: (i, 0))),
                pl.BlockSpec((None, chunk, tn),
                             lambda jj, i: (layer, jnp.where(jj < n_slabs, i, n_rows - 1),
                                            jb + jnp.minimum(jj, n_slabs - 1)))]
    in_specs += [pl.BlockSpec(bs, lag(im)) for _, bs, im in extras]
    operands = [x, w] + [a for a, _, _ in extras]
    out_shape = [jax.ShapeDtypeStruct((m, n), out_dtype)]
    out_specs = [pl.BlockSpec((tm, tn), lag(lambda j, i: (i, j)))]
    n_side_steps = 0
    if side is not None:
        side_w, side_layer = side
        side_rows, side_cols = side_w.shape[1:]
        rows_per_step = SIDE_CAST_ROWS
        while side_rows // rows_per_step > (n_slabs + 1) * n_rows:
            rows_per_step *= 2
        n_side_steps = side_rows // rows_per_step
        assert side_rows % rows_per_step == 0
        side_block = lambda jj, i: jnp.minimum(jj * n_rows + i, n_side_steps - 1)
        in_specs.append(pl.BlockSpec((None, rows_per_step, side_cols),
                                     lambda jj, i: (side_layer, side_block(jj, i), 0)))
        operands.append(side_w)
        out_shape.append(jax.ShapeDtypeStruct((side_rows, side_cols), BF16))
        out_specs.append(pl.BlockSpec((rows_per_step, side_cols), lambda jj, i: (side_block(jj, i), 0)))
    outs = pl.pallas_call(
        functools.partial(_mm_kernel, epilogue=epilogue, n_extra=len(extras), n_slabs=n_slabs,
                          n_side_steps=n_side_steps),
        out_shape=out_shape,
        grid=(n_slabs + 1, n_rows),
        in_specs=in_specs,
        out_specs=out_specs,
        scratch_shapes=[pltpu.VMEM((2, k, tn), BF16)],
        compiler_params=_params(2),
        name=name,
    )(*operands)
    return outs if side is not None else outs[0]


def _ep_identity(acc, cs):
    return acc


def _ep_relu_sq(acc, cs):
    r = jnp.maximum(acc, 0.0)
    return r * r


def _ep_residual(acc, cs, res_ref):
    return res_ref[:, cs] + acc


def matmul_residual(x, w, layer, res, *, tm=512, tn=1024, name):
    n = w.shape[2]
    tm, tn = min(tm, x.shape[0]), min(tn, n)
    return fused_matmul(x, w, layer, 0, n, _ep_residual, out_dtype=F32, tm=tm, tn=tn,
                        extras=[(res, (tm, tn), lambda j, i: (i, j))], name=name)


def _mm_kgrid_kernel(x_ref, w_ref, res_ref, o_ref):
    kk = pl.program_id(2)
    part = jnp.dot(x_ref[...], w_ref[...], preferred_element_type=F32)

    @pl.when(kk == 0)
    def _():
        o_ref[...] = res_ref[...] + part

    @pl.when(kk > 0)
    def _():
        o_ref[...] += part


def matmul_kgrid_residual(x, w, res, *, tm=1024, tn=1024, tk=4096, name):
    m, k = x.shape
    n = w.shape[1]
    tm, tn, tk = min(tm, m), min(tn, n), min(tk, k)
    assert m % tm == 0 and n % tn == 0 and k % tk == 0
    return pl.pallas_call(
        _mm_kgrid_kernel,
        out_shape=jax.ShapeDtypeStruct((m, n), F32),
        grid=(m // tm, n // tn, k // tk),
        in_specs=[pl.BlockSpec((tm, tk), lambda i, j, kk: (i, kk)),
                  pl.BlockSpec((tk, tn), lambda i, j, kk: (kk, j)),
                  pl.BlockSpec((tm, tn), lambda i, j, kk: (i, j))],
        out_specs=pl.BlockSpec((tm, tn), lambda i, j, kk: (i, j)),
        compiler_params=_params(3),
        name=name,
    )(x, w, res)


def _ep_ple(acc, cs, p_ref, wp_ref, res_ref):
    emb = jnp.dot(p_ref[...], wp_ref[:, cs], preferred_element_type=F32)
    return res_ref[:, cs] + jax.nn.sigmoid(acc) * emb


def ple_update(xn, w_gate, layer, p, w_ple, res, *, tm=512, tn=1024):
    n = w_gate.shape[2]
    tm, tn = min(tm, xn.shape[0]), min(tn, n)
    kp = p.shape[1]
    return fused_matmul(
        xn, w_gate, layer, 0, n, _ep_ple, out_dtype=F32, tm=tm, tn=tn,
        extras=[(p, (tm, kp), lambda j, i: (i, 0)),
                (w_ple, (kp, tn), lambda j, i: (0, j)),
                (res, (tm, tn), lambda j, i: (i, j))],
        name="ple_gate")


def _rope_table_kernel(cos_ref, sin_ref, *, rows_per_residue):
    tm = cos_ref.shape[0]
    half = A_HEAD_DIM // 2
    row0 = pl.program_id(0) * tm
    residue = row0 // rows_per_residue
    n0 = row0 % rows_per_residue
    n = n0 + lax.broadcasted_iota(jnp.int32, (tm, A_HEAD_DIM), 0)
    pos = (n * A_RESIDUES + residue).astype(F32)
    lane = lax.broadcasted_iota(jnp.int32, (tm, A_HEAD_DIM), 1)
    j = jnp.where(lane >= half, lane - half, lane).astype(F32)
    inv_freq = jnp.exp(j * (-2.0 * math.log(ROPE_THETA) / A_HEAD_DIM))
    ang = pos * inv_freq
    cos_ref[...] = jnp.cos(ang)
    sin_ref[...] = jnp.where(lane >= half, 1.0, -1.0) * jnp.sin(ang)


def rope_tables(seq, tm=256):
    rows_per_residue = seq // A_RESIDUES
    tm = min(tm, rows_per_residue)
    assert rows_per_residue % tm == 0
    spec = pl.BlockSpec((tm, A_HEAD_DIM), lambda i: (i, 0))
    shape = jax.ShapeDtypeStruct((seq, A_HEAD_DIM), F32)
    return pl.pallas_call(
        functools.partial(_rope_table_kernel, rows_per_residue=rows_per_residue),
        out_shape=(shape, shape), grid=(seq // tm,),
        in_specs=[], out_specs=(spec, spec), compiler_params=_params(1),
        name="rope_tables")()


def _ep_qk_norm_rope(acc, cs, gain_ref, cos_ref, sin_ref):
    cos, sin = cos_ref[...], sin_ref[...]
    gain = gain_ref[:, cs]
    d = A_HEAD_DIM
    ones = jnp.ones((d, d), BF16)
    r_i = lax.broadcasted_iota(jnp.int32, (d, d), 0)
    c_i = lax.broadcasted_iota(jnp.int32, (d, d), 1)
    swap_halves = (r_i == (c_i + d // 2) % d).astype(BF16)
    outs = []
    for hh in range(acc.shape[1] // d):
        hs = slice(hh * d, (hh + 1) * d)
        a = acc[:, hs]
        sum_sq = jnp.dot((a * a).astype(BF16), ones, preferred_element_type=F32)
        inv = lax.rsqrt(sum_sq * (1.0 / d) + RMS_EPS)
        t = a * gain[:, hs]
        t_swapped = jnp.dot(t.astype(BF16), swap_halves, preferred_element_type=F32)
        outs.append((t * cos + t_swapped * sin) * inv)
    return jnp.concatenate(outs, axis=1)


def _attn_kernel(q_ref, kc_ref, kp_ref, vc_ref, vp_ref, o_ref, lse_ref, *, parts):
    rows = q_ref.shape[-2]
    qb = parts * rows
    n = pl.program_id(1)

    def load(ref, hs):
        return ref[..., hs].reshape(qb, A_HEAD_DIM)

    def step_of(idx):
        return idx if parts == 1 else (idx % rows) * parts + idx // rows

    lq = step_of(lax.broadcasted_iota(jnp.int32, (qb, 2 * qb), 0))
    col = lax.broadcasted_iota(jnp.int32, (qb, 2 * qb), 1)
    is_prev = col < qb
    lk = step_of(jnp.where(is_prev, col, col - qb))
    dist = lq - lk + jnp.where(is_prev, qb, 0)
    ok = (dist >= 0) & (dist <= A_STEPS) & ((n > 0) | jnp.logical_not(is_prev))
    lane = lax.broadcasted_iota(jnp.int32, (qb, LANES), 1)
    scale = A_HEAD_DIM ** -0.5
    nt = (((1,), (1,)), ((), ()))
    ones = jnp.ones((2 * qb, A_HEAD_DIM), BF16)
    lse_tile = jnp.zeros((qb, LANES), F32)
    for hh in range(A_HEADS_PER_GROUP):
        hs = slice(hh * A_HEAD_DIM, (hh + 1) * A_HEAD_DIM)
        q = load(q_ref, hs)
        k_cat = jnp.concatenate([load(kp_ref, hs), load(kc_ref, hs)], axis=0)
        v_cat = jnp.concatenate([load(vp_ref, hs), load(vc_ref, hs)], axis=0)
        s = lax.dot_general(q, k_cat, nt, preferred_element_type=F32) * scale
        s = jnp.where(ok, s, -jnp.inf)
        mx = jnp.max(jnp.maximum(s[:, :qb], s[:, qb:]), axis=-1, keepdims=True)
        e = jnp.exp(s - mx).astype(BF16)
        pv = jnp.dot(e, jnp.concatenate([v_cat, ones], axis=1), preferred_element_type=F32)
        den = pv[:, A_HEAD_DIM:]
        o_ref[..., hs] = (pv[:, :A_HEAD_DIM] / den).reshape(o_ref.shape[:-1] + (A_HEAD_DIM,)).astype(o_ref.dtype)
        lse_tile = jnp.where(lane == hh, mx + jnp.log(den), lse_tile)
    lse_ref[...] = lse_tile.reshape(lse_ref.shape)


def dilated_attention(qk, v, group, dilation):
    seq = qk.shape[0]
    gw = A_GROUP_WIDTH
    n_groups = len(A_PATTERNS)
    parts = A_RESIDUES // dilation
    rpr = seq // A_RESIDUES
    rows = max(A_STEPS // parts, 16)
    nb = rpr // rows
    qk_w, v_w = qk.shape[1], v.shape[1]
    if parts == 1:
        vshape = lambda w: (seq, w)
        blk = lambda w: (rows, w)
        at = lambda r, n, c: (r * nb + n, c)
    else:
        vshape = lambda w: (parts, dilation, rpr, w)
        blk = lambda w: (parts, None, rows, w)
        at = lambda r, n, c: (0, r, n, c)
    view = lambda a: a.reshape(vshape(a.shape[1]))
    prev = lambda n: jnp.maximum(n - 1, 0)
    o, lse = pl.pallas_call(
        functools.partial(_attn_kernel, parts=parts),
        out_shape=(jax.ShapeDtypeStruct(vshape(gw), BF16),
                   jax.ShapeDtypeStruct(vshape(LANES), F32)),
        grid=(dilation, nb),
        in_specs=[pl.BlockSpec(blk(gw), lambda r, n: at(r, n, group)),
                  pl.BlockSpec(blk(gw), lambda r, n: at(r, n, n_groups + group)),
                  pl.BlockSpec(blk(gw), lambda r, n: at(r, prev(n), n_groups + group)),
                  pl.BlockSpec(blk(gw), lambda r, n: at(r, n, group)),
                  pl.BlockSpec(blk(gw), lambda r, n: at(r, prev(n), group))],
        out_specs=(pl.BlockSpec(blk(gw), lambda r, n: at(r, n, 0)),
                   pl.BlockSpec(blk(LANES), lambda r, n: at(r, n, 0))),
        compiler_params=_params(2),
        name=f"dilated_attn_g{group}",
    )(view(qk), view(qk), view(qk), view(v), view(v))
    return o.reshape(seq, gw), lse.reshape(seq, LANES)


def _attn_merge_kernel(o0_ref, o1_ref, o2_ref, l0_ref, l1_ref, l2_ref, out_ref):
    rows = o0_ref.shape[1]
    tm = A_RESIDUES * rows

    def flat(ref, sl=slice(None)):
        return ref[:, :, sl].reshape(tm, -1)

    l0, l1, l2 = flat(l0_ref), flat(l1_ref), flat(l2_ref)
    mx = jnp.maximum(jnp.maximum(l0, l1), l2)
    w0, w1, w2 = jnp.exp(l0 - mx), jnp.exp(l1 - mx), jnp.exp(l2 - mx)
    tot = w0 + w1 + w2
    w0, w1, w2 = w0 / tot, w1 / tot, w2 / tot
    to_sequence_order = _row_permutation(tm, rows, False)
    for hh in range(A_HEADS_PER_GROUP):
        hs = slice(hh * A_HEAD_DIM, (hh + 1) * A_HEAD_DIM)
        merged = (w0[:, hh:hh + 1] * flat(o0_ref, hs).astype(F32)
                  + w1[:, hh:hh + 1] * flat(o1_ref, hs).astype(F32)
                  + w2[:, hh:hh + 1] * flat(o2_ref, hs).astype(F32)).astype(BF16)
        out_ref[:, hs] = jnp.dot(to_sequence_order, merged, preferred_element_type=F32).astype(out_ref.dtype)


def attention_merge(outs, lses, rows=16):
    seq, gw = outs[0].shape
    rpr = seq // A_RESIDUES
    rows = min(rows, rpr)
    tm = A_RESIDUES * rows
    o_spec = pl.BlockSpec((A_RESIDUES, rows, gw), lambda i: (0, i, 0))
    l_spec = pl.BlockSpec((A_RESIDUES, rows, LANES), lambda i: (0, i, 0))
    return pl.pallas_call(
        _attn_merge_kernel,
        out_shape=jax.ShapeDtypeStruct((seq, gw), BF16),
        grid=(rpr // rows,),
        in_specs=[o_spec] * 3 + [l_spec] * 3,
        out_specs=pl.BlockSpec((tm, gw), lambda i: (i, 0)),
        compiler_params=_params(1),
        name="attn_merge",
    )(*[o.reshape(A_RESIDUES, rpr, gw) for o in outs],
      *[l.reshape(A_RESIDUES, rpr, LANES) for l in lses])


def dilated_attention_mixer(h, norm_gain, w_in, layer, q_gain, k_gain, w_out, rope):
    seq = h.shape[0]
    n_heads = len(A_PATTERNS) * A_HEADS_PER_GROUP
    qk_width = 2 * n_heads * A_HEAD_DIM
    v_width = n_heads * A_HEAD_DIM
    assert seq % (A_RESIDUES * A_STEPS) == 0
    assert all(w // d == A_STEPS and A_RESIDUES % d == 0 for w, d in A_PATTERNS)
    xn = rmsnorm_bf16(h, norm_gain, residue_major=True)
    cos, sin = rope
    gain = jnp.concatenate([jnp.tile(q_gain, n_heads), jnp.tile(k_gain, n_heads)]).reshape(1, qk_width)
    tm, tn = min(1024, seq), 1024
    qk = fused_matmul(
        xn, w_in, layer, 0, qk_width, _ep_qk_norm_rope, out_dtype=BF16, tm=tm, tn=tn,
        extras=[(gain, (1, tn), lambda j, i: (0, j)),
                (cos, (tm, A_HEAD_DIM), lambda j, i: (i, 0)),
                (sin, (tm, A_HEAD_DIM), lambda j, i: (i, 0))],
        name="attn_qk_proj")
    v = fused_matmul(xn, w_in, layer, qk_width, v_width, _ep_identity, out_dtype=BF16,
                     tm=tm, tn=tn, name="attn_v_proj")
    outs, lses = [], []
    for g, (_, dilation) in enumerate(A_PATTERNS):
        o, lse = dilated_attention(qk, v, g, dilation)
        outs.append(o)
        lses.append(lse)
    merged = attention_merge(outs, lses)
    return matmul_residual(merged, w_out, layer, h, name="attn_out_proj")


def _ep_gates(acc, cs, bias_ref):
    lane = lax.broadcasted_iota(jnp.int32, acc.shape, 1)
    g = jnp.where(lane < 2 * B_HEADS, acc + bias_ref[...], 0.0)
    return jnp.where(lane >= B_HEADS, jax.nn.log_sigmoid(g), g)


def _mlstm_kernel(q_ref, k_ref, v_ref, og_ref, gcol_ref, grow_ref, gain_ref, o_ref,
                  c_ref, n_ref, m_ref):
    hd = pl.program_id(0)
    L = q_ref.shape[0]
    hi = lax.Precision.HIGHEST

    @pl.when(pl.program_id(1) == 0)
    def _():
        c_ref[...] = jnp.zeros_like(c_ref)
        n_ref[...] = jnp.zeros_like(n_ref)
        m_ref[...] = jnp.zeros_like(m_ref)

    gcol = gcol_ref[...]
    lane = lax.broadcasted_iota(jnp.int32, gcol.shape, 1)
    ig_col = jnp.sum(jnp.where(lane == hd, gcol, 0.0), axis=-1, keepdims=True)
    lf_col = jnp.sum(jnp.where(lane == hd + B_HEADS, gcol, 0.0), axis=-1, keepdims=True)
    ig_row = grow_ref[pl.ds(hd, 1), :]
    lf_row = grow_ref[pl.ds(hd + B_HEADS, 1), :]

    t_i = lax.broadcasted_iota(jnp.int32, (L, L), 0)
    s_i = lax.broadcasted_iota(jnp.int32, (L, L), 1)
    causal = s_i <= t_i
    lower = causal.astype(F32)
    upper = (t_i <= s_i).astype(F32)
    b_col = jnp.dot(lower, jnp.broadcast_to(lf_col, (L, LANES)), precision=hi,
                    preferred_element_type=F32)[:, :1]
    b_row = jnp.dot(jnp.broadcast_to(lf_row, (8, L)), upper, precision=hi,
                    preferred_element_type=F32)[:1, :]

    m_prev = m_ref[:1, :1]
    c_prev = c_ref[...]
    n_prev = n_ref[...]

    q = q_ref[...] * jnp.asarray(B_QK_DIM ** -0.5, q_ref.dtype)
    k = k_ref[...]
    v = v_ref[...]

    log_d = jnp.where(causal, b_col - b_row + ig_row, -jnp.inf)
    log_inter = b_col + m_prev
    m_t = jnp.maximum(log_inter, jnp.max(log_d, axis=-1, keepdims=True))
    scores = lax.dot_general(q, k, (((1,), (1,)), ((), ())), preferred_element_type=F32)
    w = scores * jnp.exp(log_d - m_t)
    decay = jnp.exp(log_inter - m_t)
    num = (decay * jnp.dot(q, c_prev.astype(BF16), preferred_element_type=F32)
           + jnp.dot(w.astype(BF16), v, preferred_element_type=F32))
    qn = (decay * jnp.sum(q.astype(F32) * n_prev, axis=-1, keepdims=True)
          + jnp.sum(w, axis=-1, keepdims=True))
    hval = num / jnp.maximum(jnp.abs(qn), jnp.exp(-m_t))

    inv = lax.rsqrt(jnp.mean(hval * hval, axis=-1, keepdims=True) + RMS_EPS)
    o_ref[...] = (hval * inv * gain_ref[...] * jax.nn.sigmoid(og_ref[...].astype(F32))).astype(o_ref.dtype)

    b_last = b_col[L - 1:L, :]
    log_w = b_last - b_col + ig_col
    m_new = jnp.maximum(b_last + m_prev, jnp.max(log_w, axis=0, keepdims=True))
    w_s = jnp.exp(log_w - m_new)
    carry = jnp.exp(b_last + m_prev - m_new)
    k_w = k.astype(F32) * w_s
    c_ref[...] = carry * c_prev + lax.dot_general(
        k_w.astype(BF16), v, (((0,), (0,)), ((), ())), preferred_element_type=F32)
    n_ref[...] = carry * n_prev + jnp.sum(k_w, axis=0, keepdims=True)
    m_ref[...] = jnp.broadcast_to(m_new, m_ref.shape)


def mlstm_mixer(h, norm_gain, w_in, layer, gate_bias, h_gain, w_out):
    seq, d = h.shape
    xn = rmsnorm_bf16(h, norm_gain)
    qk_w = B_HEADS * B_QK_DIM
    v_w = B_HEADS * B_V_DIM
    main_w = 2 * qk_w + v_w + d
    proj = fused_matmul(xn, w_in, layer, 0, main_w, _ep_identity, out_dtype=BF16,
                        name="mlstm_in_proj")
    n_gate = 2 * B_HEADS
    assert w_in.shape[2] == main_w + n_gate
    bias = jnp.pad(gate_bias, (0, B_GATE_PAD - n_gate)).reshape(1, B_GATE_PAD)
    gates = fused_matmul(xn, w_in, layer, main_w, B_GATE_PAD, _ep_gates, out_dtype=F32, tn=B_GATE_PAD,
                         extras=[(bias, (1, B_GATE_PAD), lambda j, i: (0, 0))],
                         name="mlstm_gate_proj")
    gates_t = gates[:, :n_gate].T
    L = min(B_CHUNK, seq)
    nq = qk_w // B_QK_DIM
    nv = v_w // B_V_DIM
    hb = pl.pallas_call(
        _mlstm_kernel,
        out_shape=jax.ShapeDtypeStruct((seq, v_w), BF16),
        grid=(B_HEADS, seq // L),
        in_specs=[pl.BlockSpec((L, B_QK_DIM), lambda hd, c: (c, hd)),
                  pl.BlockSpec((L, B_QK_DIM), lambda hd, c: (c, nq + hd)),
                  pl.BlockSpec((L, B_V_DIM), lambda hd, c: (c, (2 * qk_w) // B_V_DIM + hd)),
                  pl.BlockSpec((L, B_V_DIM), lambda hd, c: (c, (2 * qk_w) // B_V_DIM + nv + hd)),
                  pl.BlockSpec((L, B_GATE_PAD), lambda hd, c: (c, 0)),
                  pl.BlockSpec((n_gate, L), lambda hd, c: (0, c)),
                  pl.BlockSpec((1, B_V_DIM), lambda hd, c: (0, hd))],
        out_specs=pl.BlockSpec((L, B_V_DIM), lambda hd, c: (c, hd)),
        scratch_shapes=[pltpu.VMEM((B_QK_DIM, B_V_DIM), F32),
                        pltpu.VMEM((1, B_QK_DIM), F32),
                        pltpu.VMEM((8, LANES), F32)],
        compiler_params=_params(2),
        name="mlstm_chunks",
    )(proj, proj, proj, proj, gates, gates_t, h_gain.reshape(1, v_w))
    return matmul_residual(hb, w_out, layer, h, name="mlstm_out_proj")


def _pool_kernel(u_ref, halo_ref, wg_ref, scale_ref, o_ref, wb_ref):
    g = pl.program_id(0)
    i = pl.program_id(1)
    tm = u_ref.shape[0]

    @pl.when(i == 0)
    def _():
        wb_ref[...] = wg_ref[...].astype(BF16)

    win = jnp.left_shift(jnp.int32(C_WINDOWS[0]), g)
    t_i = lax.broadcasted_iota(jnp.int32, (tm, tm), 0)
    s_i = lax.broadcasted_iota(jnp.int32, (tm, tm), 1)
    dist = t_i - s_i
    band = ((dist >= 0) & (dist < win)).astype(BF16)
    t_h = lax.broadcasted_iota(jnp.int32, (tm, C_HALO), 0)
    s_h = lax.broadcasted_iota(jnp.int32, (tm, C_HALO), 1)
    band_halo = ((t_h + C_HALO - s_h < win) & (i > 0)).astype(BF16)
    u = u_ref[...]
    total = (jnp.dot(band, u, preferred_element_type=F32)
             + jnp.dot(band_halo, halo_ref[...], preferred_element_type=F32))
    t_glob = i * tm + lax.broadcasted_iota(jnp.int32, (tm, 1), 0)
    count = jnp.minimum(t_glob + 1, win).astype(F32)
    pooled = total / count - u.astype(F32)
    y = jnp.dot(pooled.astype(BF16), wb_ref[...], preferred_element_type=F32) * scale_ref[...]
    o_ref[...] = y.astype(o_ref.dtype)


def pooling_mixer(h, norm_gain, w_in, layer, w_group, scale, w_out, tm=512):
    seq, d = h.shape
    assert all(C_WINDOWS[g] == C_WINDOWS[0] << g for g in range(len(C_WINDOWS)))
    assert max(C_WINDOWS) <= C_HALO
    xn = rmsnorm_bf16(h, norm_gain)
    u = fused_matmul(xn, w_in, layer, 0, d, _ep_identity, out_dtype=BF16, name="pool_in_proj")
    tm = min(tm, seq)
    gd = C_GROUP_DIM
    halo_per_tile = tm // C_HALO
    y = pl.pallas_call(
        _pool_kernel,
        out_shape=jax.ShapeDtypeStruct((seq, d), BF16),
        grid=(len(C_WINDOWS), seq // tm),
        in_specs=[pl.BlockSpec((tm, gd), lambda g, i: (i, g)),
                  pl.BlockSpec((C_HALO, gd), lambda g, i: (jnp.maximum(i * halo_per_tile - 1, 0), g)),
                  pl.BlockSpec((None, None, gd, gd), lambda g, i: (layer, g, 0, 0)),
                  pl.BlockSpec((1, gd), lambda g, i: (0, g))],
        out_specs=pl.BlockSpec((tm, gd), lambda g, i: (i, g)),
        scratch_shapes=[pltpu.VMEM((gd, gd), BF16)],
        compiler_params=_params(2),
        name="pool_group",
    )(u, u, w_group, scale.reshape(1, d))
    return matmul_residual(y, w_out, layer, h, name="pool_out_proj")


def squared_relu_mlp(h, norm_gain, w_in, w_out, layer):
    xn = rmsnorm_bf16(h, norm_gain)
    hidden, w_out_bf16 = fused_matmul(xn, w_in, layer, 0, w_in.shape[2], _ep_relu_sq, out_dtype=BF16,
                                      side=(w_out, layer), name="mlp_up")
    return matmul_kgrid_residual(hidden, w_out_bf16, h, name="mlp_down")


def kernel(x, p, norm_mix, norm_mlp, norm_ple, w_ple, w_ple_gate, w_mlp_in, w_mlp_out, a_w_in, a_q_norm, a_k_norm, a_w_out, b_w_in, b_gate_bias, b_h_norm, b_w_out, c_w_in, c_w_group, c_scale, c_w_out):
    bsz, seq, d = x.shape
    depth = p.shape[0]
    rope = rope_tables(seq)
    outs = []
    for b in range(bsz):
        h = x.reshape(seq, d) if bsz == 1 else x[b]
        for i in range(depth):
            kind, j = i % 3, i // 3
            if kind == 0:
                h = dilated_attention_mixer(h, norm_mix[i], a_w_in, j, a_q_norm[j], a_k_norm[j], a_w_out, rope)
            elif kind == 1:
                h = mlstm_mixer(h, norm_mix[i], b_w_in, j, b_gate_bias[j], b_h_norm[j], b_w_out)
            else:
                h = pooling_mixer(h, norm_mix[i], c_w_in, j, c_w_group, c_scale[j], c_w_out)
            h = squared_relu_mlp(h, norm_mlp[i], w_mlp_in, w_mlp_out, i)
            h = ple_update(rmsnorm_bf16(h, norm_ple[i]), w_ple_gate, i,
                           p[i, b].astype(BF16), w_ple[i].astype(BF16), h)
        outs.append(h)
    return outs[0][None] if bsz == 1 else jnp.stack(outs, axis=0)
```

```python
import functools
import math

import jax
import jax.numpy as jnp
from jax import lax
from jax.experimental import pallas as pl
from jax.experimental.pallas import tpu as pltpu

F32 = jnp.float32
BF16 = jnp.bfloat16

RMS_EPS = 1e-6
ROPE_THETA = 10000.0

A_HEAD_DIM = 128
A_HEADS_PER_GROUP = 16
A_PATTERNS = ((128, 1), (512, 4), (2048, 16))
A_STEPS = 128
A_RESIDUES = 16
A_GROUP_WIDTH = A_HEADS_PER_GROUP * A_HEAD_DIM

B_HEADS = 8
B_QK_DIM = 256
B_V_DIM = 512
B_CHUNK = 256
B_GATE_PAD = 128

C_WINDOWS = (2, 4, 8, 16)
C_GROUP_DIM = 1024
C_HALO = 128

LANES = 128
VMEM_LIMIT_BYTES = 60 * 1024 * 1024


def _params(n_grid_axes):
    return pltpu.CompilerParams(
        dimension_semantics=("arbitrary",) * n_grid_axes,
        vmem_limit_bytes=VMEM_LIMIT_BYTES)


def _rmsnorm_rows(x, gain):
    inv = lax.rsqrt(jnp.mean(x * x, axis=-1, keepdims=True) + RMS_EPS)
    return x * inv * gain


def _row_permutation(tm, rows, to_residue_major):
    a = lax.broadcasted_iota(jnp.int32, (tm, tm), 0)
    b = lax.broadcasted_iota(jnp.int32, (tm, tm), 1)
    major, seq_order = (a, b) if to_residue_major else (b, a)
    return (seq_order == (major % rows) * A_RESIDUES + major // rows).astype(BF16)


def _rmsnorm_residue_major_kernel(x_ref, g_ref, o_ref):
    tm = x_ref.shape[0]
    rows = tm // A_RESIDUES
    y = _rmsnorm_rows(x_ref[...], g_ref[...]).astype(BF16)
    y = jnp.dot(_row_permutation(tm, rows, True), y, preferred_element_type=F32)
    o_ref[...] = y.astype(o_ref.dtype).reshape(o_ref.shape)


def rmsnorm_residue_major(h, gain, tm=256):
    m, d = h.shape
    tm = min(tm, m)
    in_specs = [pl.BlockSpec((tm, d), lambda i: (i, 0)),
                pl.BlockSpec((1, d), lambda i: (0, 0))]
    out = pl.pallas_call(
        _rmsnorm_residue_major_kernel,
        out_shape=jax.ShapeDtypeStruct((A_RESIDUES, m // A_RESIDUES, d), BF16),
        grid=(m // tm,), in_specs=in_specs,
        out_specs=pl.BlockSpec((A_RESIDUES, tm // A_RESIDUES, d), lambda i: (0, i, 0)),
        compiler_params=_params(1), name="rmsnorm_residue_major",
    )(h, gain.reshape(1, d))
    return out.reshape(m, d)


SIDE_CAST_ROWS = 64
NORM_ROWS = 64


def _row_inv_rms(x_ref):
    tm, k = x_ref.shape
    rows = min(NORM_ROWS, tm)
    pieces = []
    for r in range(tm // rows):
        part = None
        for c in range(k // LANES):
            xc = x_ref[r * rows:(r + 1) * rows, c * LANES:(c + 1) * LANES].astype(F32)
            part = xc * xc if part is None else part + xc * xc
        pieces.append(jnp.sum(part, axis=-1, keepdims=True))
    return lax.rsqrt(jnp.concatenate(pieces, axis=0) * (1.0 / k) + RMS_EPS)


def _mm_kernel(x_ref, wchunk_ref, *rest, epilogue, n_extra, n_out, n_slabs, n_side_steps, normed):
    if normed:
        gcol_ref, rest = rest[0], rest[1:]
    extra_refs, rest = rest[:n_extra], rest[n_extra:]
    if n_side_steps:
        side_ref, rest = rest[0], rest[1:]
    o_refs, rest = rest[:n_out], rest[n_out:]
    if n_side_steps:
        side_out_ref, rest = rest[0], rest[1:]
    if normed:
        wb_ref, inv_ref = rest
    else:
        wb_ref, = rest
    jj, i = pl.program_id(0), pl.program_id(1)
    chunk = wchunk_ref.shape[0]

    @pl.when(jj < n_slabs)
    def _():
        row0 = pl.multiple_of(i * chunk, chunk)
        wc = wchunk_ref[...]
        if normed:
            wc = wc * gcol_ref[...]
        wb_ref[jj % 2, pl.ds(row0, chunk), :] = wc.astype(BF16)

    if normed:
        tile_lane = lax.broadcasted_iota(jnp.int32, inv_ref.shape, 1) == i

        @pl.when((jj == 0) & (i == 0))
        def _():
            inv_ref[...] = jnp.zeros_like(inv_ref)

        @pl.when(jj == 1)
        def _():
            inv_ref[...] = jnp.where(tile_lane, _row_inv_rms(x_ref), inv_ref[...])

    @pl.when(jj > 0)
    def _():
        acc = jnp.dot(x_ref[...], wb_ref[(jj - 1) % 2], preferred_element_type=F32)
        if normed:
            acc = acc * jnp.sum(jnp.where(tile_lane, inv_ref[...], 0.0), axis=-1, keepdims=True)
        vals = epilogue(acc, slice(None), *extra_refs)
        vals = vals if isinstance(vals, tuple) else (vals,)
        for o_ref, val in zip(o_refs, vals, strict=True):
            o_ref[...] = val.astype(o_ref.dtype)

    if n_side_steps:
        @pl.when(jj * pl.num_programs(1) + i < n_side_steps)
        def _():
            side_out_ref[...] = side_ref[...].astype(BF16)


def fused_matmul(x, w, layer, col0, n, epilogue, *, out_dtype, tm=1024, tn=1024, extras=(), side=None,
                 norm_gain=None, name):
    m, k = x.shape
    tm, tn = min(tm, m), min(tn, n)
    n_slabs, n_rows = n // tn, m // tm
    assert w.shape[1] == k and m % tm == 0 and n % tn == 0 and col0 % tn == 0 and k % n_rows == 0, (
        w.shape, m, n, tm, tn, col0)
    jb = col0 // tn
    chunk = k // n_rows
    out_dtypes = out_dtype if isinstance(out_dtype, tuple) else (out_dtype,)

    def lag(im):
        return lambda jj, i: im(jnp.maximum(jj - 1, 0), jnp.where(jj > 0, i, 0))

    chunk_row = lambda jj, i: jnp.where(jj < n_slabs, i, n_rows - 1)
    in_specs = [pl.BlockSpec((tm, k), lag(lambda j, i: (i, 0))),
                pl.BlockSpec((None, chunk, tn),
                             lambda jj, i: (layer, chunk_row(jj, i), jb + jnp.minimum(jj, n_slabs - 1)))]
    operands = [x, w]
    if norm_gain is not None:
        assert n_rows <= LANES
        in_specs.append(pl.BlockSpec((chunk, 1), lambda jj, i: (chunk_row(jj, i), 0)))
        operands.append(norm_gain.reshape(k, 1))
    in_specs += [pl.BlockSpec(bs, lag(im)) for _, bs, im in extras]
    operands += [a for a, _, _ in extras]
    out_shape = [jax.ShapeDtypeStruct((m, n), dt) for dt in out_dtypes]
    out_specs = [pl.BlockSpec((tm, tn), lag(lambda j, i: (i, j))) for _ in out_dtypes]
    n_side_steps = 0
    if side is not None:
        side_w, side_layer = side
        side_rows, side_cols = side_w.shape[1:]
        rows_per_step = SIDE_CAST_ROWS
        while side_rows // rows_per_step > (n_slabs + 1) * n_rows:
            rows_per_step *= 2
        n_side_steps = side_rows // rows_per_step
        assert side_rows % rows_per_step == 0
        side_block = lambda jj, i: jnp.minimum(jj * n_rows + i, n_side_steps - 1)
        in_specs.append(pl.BlockSpec((None, rows_per_step, side_cols),
                                     lambda jj, i: (side_layer, side_block(jj, i), 0)))
        operands.append(side_w)
        out_shape.append(jax.ShapeDtypeStruct((side_rows, side_cols), BF16))
        out_specs.append(pl.BlockSpec((rows_per_step, side_cols), lambda jj, i: (side_block(jj, i), 0)))
    outs = pl.pallas_call(
        functools.partial(_mm_kernel, epilogue=epilogue, n_extra=len(extras), n_out=len(out_dtypes),
                          n_slabs=n_slabs, n_side_steps=n_side_steps, normed=norm_gain is not None),
        out_shape=out_shape,
        grid=(n_slabs + 1, n_rows),
        in_specs=in_specs,
        out_specs=out_specs,
        scratch_shapes=[pltpu.VMEM((2, k, tn), BF16)] + (
            [pltpu.VMEM((tm, LANES), F32)] if norm_gain is not None else []),
        compiler_params=_params(2),
        name=name,
    )(*operands)
    return outs[0] if len(outs) == 1 else tuple(outs)


def _ep_identity(acc, cs):
    return acc


def _ep_relu_sq(acc, cs):
    r = jnp.maximum(acc, 0.0)
    return r * r


def _ep_residual(acc, cs, res_ref):
    h = res_ref[:, cs] + acc
    return h, h


def matmul_residual(x, w, layer, res, *, tm=512, tn=1024, name):
    n = w.shape[2]
    tm, tn = min(tm, x.shape[0]), min(tn, n)
    return fused_matmul(x, w, layer, 0, n, _ep_residual, out_dtype=(F32, BF16), tm=tm, tn=tn,
                        extras=[(res, (tm, tn), lambda j, i: (i, j))], name=name)


def _mm_kgrid_kernel(x_ref, w_ref, res_ref, o_ref, ob_ref):
    kk = pl.program_id(2)
    part = jnp.dot(x_ref[...], w_ref[...], preferred_element_type=F32)

    @pl.when(kk == 0)
    def _():
        o_ref[...] = res_ref[...] + part

    @pl.when(kk > 0)
    def _():
        o_ref[...] += part

    @pl.when(kk == pl.num_programs(2) - 1)
    def _():
        ob_ref[...] = o_ref[...].astype(ob_ref.dtype)


def matmul_kgrid_residual(x, w, res, *, tm=1024, tn=1024, tk=2048, name):
    m, k = x.shape
    n = w.shape[1]
    tm, tn, tk = min(tm, m), min(tn, n), min(tk, k)
    assert m % tm == 0 and n % tn == 0 and k % tk == 0
    return pl.pallas_call(
        _mm_kgrid_kernel,
        out_shape=(jax.ShapeDtypeStruct((m, n), F32), jax.ShapeDtypeStruct((m, n), BF16)),
        grid=(m // tm, n // tn, k // tk),
        in_specs=[pl.BlockSpec((tm, tk), lambda i, j, kk: (i, kk)),
                  pl.BlockSpec((tk, tn), lambda i, j, kk: (kk, j)),
                  pl.BlockSpec((tm, tn), lambda i, j, kk: (i, j))],
        out_specs=(pl.BlockSpec((tm, tn), lambda i, j, kk: (i, j)),
                   pl.BlockSpec((tm, tn), lambda i, j, kk: (i, j))),
        compiler_params=_params(3),
        name=name,
    )(x, w, res)


def _ep_ple(acc, cs, p_ref, wp_ref, res_ref):
    emb = jnp.dot(p_ref[...], wp_ref[:, cs], preferred_element_type=F32)
    h = res_ref[:, cs] + jax.nn.sigmoid(acc) * emb
    return h, h


def ple_update(h, hb, norm_gain, w_gate, layer, p, w_ple, *, tm=512, tn=1024):
    n = w_gate.shape[2]
    tm, tn = min(tm, h.shape[0]), min(tn, n)
    kp = p.shape[1]
    return fused_matmul(
        hb, w_gate, layer, 0, n, _ep_ple, out_dtype=(F32, BF16), tm=tm, tn=tn, norm_gain=norm_gain,
        extras=[(p, (tm, kp), lambda j, i: (i, 0)),
                (w_ple, (kp, tn), lambda j, i: (0, j)),
                (h, (tm, tn), lambda j, i: (i, j))],
        name="ple_gate")


def _rope_table_kernel(cos_ref, sin_ref, *, rows_per_residue):
    tm = cos_ref.shape[0]
    half = A_HEAD_DIM // 2
    row0 = pl.program_id(0) * tm
    residue = row0 // rows_per_residue
    n0 = row0 % rows_per_residue
    n = n0 + lax.broadcasted_iota(jnp.int32, (tm, A_HEAD_DIM), 0)
    pos = (n * A_RESIDUES + residue).astype(F32)
    lane = lax.broadcasted_iota(jnp.int32, (tm, A_HEAD_DIM), 1)
    j = jnp.where(lane >= half, lane - half, lane).astype(F32)
    inv_freq = jnp.exp(j * (-2.0 * math.log(ROPE_THETA) / A_HEAD_DIM))
    ang = pos * inv_freq
    cos_ref[...] = jnp.cos(ang)
    sin_ref[...] = jnp.where(lane >= half, 1.0, -1.0) * jnp.sin(ang)


def rope_tables(seq, tm=256):
    rows_per_residue = seq // A_RESIDUES
    tm = min(tm, rows_per_residue)
    assert rows_per_residue % tm == 0
    spec = pl.BlockSpec((tm, A_HEAD_DIM), lambda i: (i, 0))
    shape = jax.ShapeDtypeStruct((seq, A_HEAD_DIM), F32)
    return pl.pallas_call(
        functools.partial(_rope_table_kernel, rows_per_residue=rows_per_residue),
        out_shape=(shape, shape), grid=(seq // tm,),
        in_specs=[], out_specs=(spec, spec), compiler_params=_params(1),
        name="rope_tables")()


def _ep_qk_norm_rope(acc, cs, gain_ref, cos_ref, sin_ref):
    cos, sin = cos_ref[...], sin_ref[...]
    gain = gain_ref[:, cs]
    d = A_HEAD_DIM
    ones = jnp.ones((d, d), BF16)
    r_i = lax.broadcasted_iota(jnp.int32, (d, d), 0)
    c_i = lax.broadcasted_iota(jnp.int32, (d, d), 1)
    swap_halves = (r_i == (c_i + d // 2) % d).astype(BF16)
    outs = []
    for hh in range(acc.shape[1] // d):
        hs = slice(hh * d, (hh + 1) * d)
        a = acc[:, hs]
        sum_sq = jnp.dot((a * a).astype(BF16), ones, preferred_element_type=F32)
        inv = lax.rsqrt(sum_sq * (1.0 / d) + RMS_EPS)
        t = a * gain[:, hs]
        t_swapped = jnp.dot(t.astype(BF16), swap_halves, preferred_element_type=F32)
        outs.append((t * cos + t_swapped * sin) * inv)
    return jnp.concatenate(outs, axis=1)


def _attn_kernel(q_ref, kc_ref, kp_ref, vc_ref, vp_ref, o_ref, lse_ref, *, parts):
    rows = q_ref.shape[-2]
    qb = parts * rows
    n = pl.program_id(1)

    def load(ref, hs):
        return ref[..., hs].reshape(qb, A_HEAD_DIM)

    def step_of(idx):
        return idx if parts == 1 else (idx % rows) * parts + idx // rows

    lq = step_of(lax.broadcasted_iota(jnp.int32, (qb, 2 * qb), 0))
    col = lax.broadcasted_iota(jnp.int32, (qb, 2 * qb), 1)
    is_prev = col < qb
    lk = step_of(jnp.where(is_prev, col, col - qb))
    dist = lq - lk + jnp.where(is_prev, qb, 0)
    ok = (dist >= 0) & (dist <= A_STEPS) & ((n > 0) | jnp.logical_not(is_prev))
    lane = lax.broadcasted_iota(jnp.int32, (qb, LANES), 1)
    scale = A_HEAD_DIM ** -0.5
    nt = (((1,), (1,)), ((), ()))
    ones = jnp.ones((2 * qb, A_HEAD_DIM), BF16)
    lse_tile = jnp.zeros((qb, LANES), F32)
    for hh in range(A_HEADS_PER_GROUP):
        hs = slice(hh * A_HEAD_DIM, (hh + 1) * A_HEAD_DIM)
        q = load(q_ref, hs)
        k_cat = jnp.concatenate([load(kp_ref, hs), load(kc_ref, hs)], axis=0)
        v_cat = jnp.concatenate([load(vp_ref, hs), load(vc_ref, hs)], axis=0)
        s = lax.dot_general(q, k_cat, nt, preferred_element_type=F32) * scale
        s = jnp.where(ok, s, -jnp.inf)
        mx = jnp.max(jnp.maximum(s[:, :qb], s[:, qb:]), axis=-1, keepdims=True)
        e = jnp.exp(s - mx).astype(BF16)
        pv = jnp.dot(e, jnp.concatenate([v_cat, ones], axis=1), preferred_element_type=F32)
        den = pv[:, A_HEAD_DIM:]
        o_ref[..., hs] = (pv[:, :A_HEAD_DIM] / den).reshape(o_ref.shape[:-1] + (A_HEAD_DIM,)).astype(o_ref.dtype)
        lse_tile = jnp.where(lane == hh, mx + jnp.log(den), lse_tile)
    lse_ref[...] = lse_tile.reshape(lse_ref.shape)


def dilated_attention(qk, v, group, dilation):
    seq = qk.shape[0]
    gw = A_GROUP_WIDTH
    n_groups = len(A_PATTERNS)
    parts = A_RESIDUES // dilation
    rpr = seq // A_RESIDUES
    rows = max(A_STEPS // parts, 16)
    nb = rpr // rows
    qk_w, v_w = qk.shape[1], v.shape[1]
    if parts == 1:
        vshape = lambda w: (seq, w)
        blk = lambda w: (rows, w)
        at = lambda r, n, c: (r * nb + n, c)
    else:
        vshape = lambda w: (parts, dilation, rpr, w)
        blk = lambda w: (parts, None, rows, w)
        at = lambda r, n, c: (0, r, n, c)
    view = lambda a: a.reshape(vshape(a.shape[1]))
    prev = lambda n: jnp.maximum(n - 1, 0)
    o, lse = pl.pallas_call(
        functools.partial(_attn_kernel, parts=parts),
        out_shape=(jax.ShapeDtypeStruct(vshape(gw), BF16),
                   jax.ShapeDtypeStruct(vshape(LANES), F32)),
        grid=(dilation, nb),
        in_specs=[pl.BlockSpec(blk(gw), lambda r, n: at(r, n, group)),
                  pl.BlockSpec(blk(gw), lambda r, n: at(r, n, n_groups + group)),
                  pl.BlockSpec(blk(gw), lambda r, n: at(r, prev(n), n_groups + group)),
                  pl.BlockSpec(blk(gw), lambda r, n: at(r, n, group)),
                  pl.BlockSpec(blk(gw), lambda r, n: at(r, prev(n), group))],
        out_specs=(pl.BlockSpec(blk(gw), lambda r, n: at(r, n, 0)),
                   pl.BlockSpec(blk(LANES), lambda r, n: at(r, n, 0))),
        compiler_params=_params(2),
        name=f"dilated_attn_g{group}",
    )(view(qk), view(qk), view(qk), view(v), view(v))
    return o.reshape(seq, gw), lse.reshape(seq, LANES)


def _attn_merge_kernel(o0_ref, o1_ref, o2_ref, l0_ref, l1_ref, l2_ref, out_ref):
    rows = o0_ref.shape[1]
    tm = A_RESIDUES * rows

    def flat(ref, sl=slice(None)):
        return ref[:, :, sl].reshape(tm, -1)

    l0, l1, l2 = flat(l0_ref), flat(l1_ref), flat(l2_ref)
    mx = jnp.maximum(jnp.maximum(l0, l1), l2)
    w0, w1, w2 = jnp.exp(l0 - mx), jnp.exp(l1 - mx), jnp.exp(l2 - mx)
    tot = w0 + w1 + w2
    w0, w1, w2 = w0 / tot, w1 / tot, w2 / tot
    to_sequence_order = _row_permutation(tm, rows, False)
    for hh in range(A_HEADS_PER_GROUP):
        hs = slice(hh * A_HEAD_DIM, (hh + 1) * A_HEAD_DIM)
        merged = (w0[:, hh:hh + 1] * flat(o0_ref, hs).astype(F32)
                  + w1[:, hh:hh + 1] * flat(o1_ref, hs).astype(F32)
                  + w2[:, hh:hh + 1] * flat(o2_ref, hs).astype(F32)).astype(BF16)
        out_ref[:, hs] = jnp.dot(to_sequence_order, merged, preferred_element_type=F32).astype(out_ref.dtype)


def attention_merge(outs, lses, rows=16):
    seq, gw = outs[0].shape
    rpr = seq // A_RESIDUES
    rows = min(rows, rpr)
    tm = A_RESIDUES * rows
    o_spec = pl.BlockSpec((A_RESIDUES, rows, gw), lambda i: (0, i, 0))
    l_spec = pl.BlockSpec((A_RESIDUES, rows, LANES), lambda i: (0, i, 0))
    return pl.pallas_call(
        _attn_merge_kernel,
        out_shape=jax.ShapeDtypeStruct((seq, gw), BF16),
        grid=(rpr // rows,),
        in_specs=[o_spec] * 3 + [l_spec] * 3,
        out_specs=pl.BlockSpec((tm, gw), lambda i: (i, 0)),
        compiler_params=_params(1),
        name="attn_merge",
    )(*[o.reshape(A_RESIDUES, rpr, gw) for o in outs],
      *[l.reshape(A_RESIDUES, rpr, LANES) for l in lses])


def dilated_attention_mixer(h, norm_gain, w_in, layer, q_gain, k_gain, w_out, rope):
    seq = h.shape[0]
    n_heads = len(A_PATTERNS) * A_HEADS_PER_GROUP
    qk_width = 2 * n_heads * A_HEAD_DIM
    v_width = n_heads * A_HEAD_DIM
    assert seq % (A_RESIDUES * A_STEPS) == 0
    assert all(w // d == A_STEPS and A_RESIDUES % d == 0 for w, d in A_PATTERNS)
    xn = rmsnorm_residue_major(h, norm_gain)
    cos, sin = rope
    gain = jnp.concatenate([jnp.tile(q_gain, n_heads), jnp.tile(k_gain, n_heads)]).reshape(1, qk_width)
    tm, tn = min(1024, seq), 1024
    qk = fused_matmul(
        xn, w_in, layer, 0, qk_width, _ep_qk_norm_rope, out_dtype=BF16, tm=tm, tn=tn,
        extras=[(gain, (1, tn), lambda j, i: (0, j)),
                (cos, (tm, A_HEAD_DIM), lambda j, i: (i, 0)),
                (sin, (tm, A_HEAD_DIM), lambda j, i: (i, 0))],
        name="attn_qk_proj")
    v = fused_matmul(xn, w_in, layer, qk_width, v_width, _ep_identity, out_dtype=BF16,
                     tm=tm, tn=tn, name="attn_v_proj")
    outs, lses = [], []
    for g, (_, dilation) in enumerate(A_PATTERNS):
        o, lse = dilated_attention(qk, v, g, dilation)
        outs.append(o)
        lses.append(lse)
    merged = attention_merge(outs, lses)
    return matmul_residual(merged, w_out, layer, h, name="attn_out_proj")


def _ep_gates(acc, cs, bias_ref):
    lane = lax.broadcasted_iota(jnp.int32, acc.shape, 1)
    g = jnp.where(lane < 2 * B_HEADS, acc + bias_ref[...], 0.0)
    return jnp.where(lane >= B_HEADS, jax.nn.log_sigmoid(g), g)


def _mlstm_kernel(q_ref, k_ref, v_ref, og_ref, gcol_ref, grow_ref, gain_ref, o_ref,
                  c_ref, n_ref, m_ref):
    hd = pl.program_id(0)
    L = q_ref.shape[0]
    hi = lax.Precision.HIGHEST

    @pl.when(pl.program_id(1) == 0)
    def _():
        c_ref[...] = jnp.zeros_like(c_ref)
        n_ref[...] = jnp.zeros_like(n_ref)
        m_ref[...] = jnp.zeros_like(m_ref)

    gcol = gcol_ref[...]
    lane = lax.broadcasted_iota(jnp.int32, gcol.shape, 1)
    ig_col = jnp.sum(jnp.where(lane == hd, gcol, 0.0), axis=-1, keepdims=True)
    lf_col = jnp.sum(jnp.where(lane == hd + B_HEADS, gcol, 0.0), axis=-1, keepdims=True)
    ig_row = grow_ref[pl.ds(hd, 1), :]
    lf_row = grow_ref[pl.ds(hd + B_HEADS, 1), :]

    t_i = lax.broadcasted_iota(jnp.int32, (L, L), 0)
    s_i = lax.broadcasted_iota(jnp.int32, (L, L), 1)
    causal = s_i <= t_i
    lower = causal.astype(F32)
    upper = (t_i <= s_i).astype(F32)
    b_col = jnp.dot(lower, jnp.broadcast_to(lf_col, (L, LANES)), precision=hi,
                    preferred_element_type=F32)[:, :1]
    b_row = jnp.dot(jnp.broadcast_to(lf_row, (8, L)), upper, precision=hi,
                    preferred_element_type=F32)[:1, :]

    m_prev = m_ref[:1, :1]
    c_prev = c_ref[...]
    n_prev = n_ref[...]

    q = q_ref[...] * jnp.asarray(B_QK_DIM ** -0.5, q_ref.dtype)
    k = k_ref[...]
    v = v_ref[...]

    log_d = jnp.where(causal, b_col - b_row + ig_row, -jnp.inf)
    log_inter = b_col + m_prev
    m_t = jnp.maximum(log_inter, jnp.max(log_d, axis=-1, keepdims=True))
    scores = lax.dot_general(q, k, (((1,), (1,)), ((), ())), preferred_element_type=F32)
    w = scores * jnp.exp(log_d - m_t)
    decay = jnp.exp(log_inter - m_t)
    num = (decay * jnp.dot(q, c_prev.astype(BF16), preferred_element_type=F32)
           + jnp.dot(w.astype(BF16), v, preferred_element_type=F32))
    qn = (decay * jnp.sum(q.astype(F32) * n_prev, axis=-1, keepdims=True)
          + jnp.sum(w, axis=-1, keepdims=True))
    hval = num / jnp.maximum(jnp.abs(qn), jnp.exp(-m_t))

    inv = lax.rsqrt(jnp.mean(hval * hval, axis=-1, keepdims=True) + RMS_EPS)
    o_ref[...] = (hval * inv * gain_ref[...] * jax.nn.sigmoid(og_ref[...].astype(F32))).astype(o_ref.dtype)

    b_last = b_col[L - 1:L, :]
    log_w = b_last - b_col + ig_col
    m_new = jnp.maximum(b_last + m_prev, jnp.max(log_w, axis=0, keepdims=True))
    w_s = jnp.exp(log_w - m_new)
    carry = jnp.exp(b_last + m_prev - m_new)
    k_w = k.astype(F32) * w_s
    c_ref[...] = carry * c_prev + lax.dot_general(
        k_w.astype(BF16), v, (((0,), (0,)), ((), ())), preferred_element_type=F32)
    n_ref[...] = carry * n_prev + jnp.sum(k_w, axis=0, keepdims=True)
    m_ref[...] = jnp.broadcast_to(m_new, m_ref.shape)


def mlstm_mixer(h, hb, norm_gain, w_in, layer, gate_bias, h_gain, w_out):
    seq, d = h.shape
    qk_w = B_HEADS * B_QK_DIM
    v_w = B_HEADS * B_V_DIM
    main_w = 2 * qk_w + v_w + d
    proj = fused_matmul(hb, w_in, layer, 0, main_w, _ep_identity, out_dtype=BF16, norm_gain=norm_gain,
                        name="mlstm_in_proj")
    n_gate = 2 * B_HEADS
    assert w_in.shape[2] == main_w + n_gate
    bias = jnp.pad(gate_bias, (0, B_GATE_PAD - n_gate)).reshape(1, B_GATE_PAD)
    gates = fused_matmul(hb, w_in, layer, main_w, B_GATE_PAD, _ep_gates, out_dtype=F32, tn=B_GATE_PAD,
                         norm_gain=norm_gain, extras=[(bias, (1, B_GATE_PAD), lambda j, i: (0, 0))],
                         name="mlstm_gate_proj")
    gates_t = gates[:, :n_gate].T
    L = min(B_CHUNK, seq)
    nq = qk_w // B_QK_DIM
    nv = v_w // B_V_DIM
    mixed = pl.pallas_call(
        _mlstm_kernel,
        out_shape=jax.ShapeDtypeStruct((seq, v_w), BF16),
        grid=(B_HEADS, seq // L),
        in_specs=[pl.BlockSpec((L, B_QK_DIM), lambda hd, c: (c, hd)),
                  pl.BlockSpec((L, B_QK_DIM), lambda hd, c: (c, nq + hd)),
                  pl.BlockSpec((L, B_V_DIM), lambda hd, c: (c, (2 * qk_w) // B_V_DIM + hd)),
                  pl.BlockSpec((L, B_V_DIM), lambda hd, c: (c, (2 * qk_w) // B_V_DIM + nv + hd)),
                  pl.BlockSpec((L, B_GATE_PAD), lambda hd, c: (c, 0)),
                  pl.BlockSpec((n_gate, L), lambda hd, c: (0, c)),
                  pl.BlockSpec((1, B_V_DIM), lambda hd, c: (0, hd))],
        out_specs=pl.BlockSpec((L, B_V_DIM), lambda hd, c: (c, hd)),
        scratch_shapes=[pltpu.VMEM((B_QK_DIM, B_V_DIM), F32),
                        pltpu.VMEM((1, B_QK_DIM), F32),
                        pltpu.VMEM((8, LANES), F32)],
        compiler_params=_params(2),
        name="mlstm_chunks",
    )(proj, proj, proj, proj, gates, gates_t, h_gain.reshape(1, v_w))
    return matmul_residual(mixed, w_out, layer, h, name="mlstm_out_proj")


def _pool_kernel(u_ref, halo_ref, wg_ref, scale_ref, o_ref, wb_ref):
    g = pl.program_id(0)
    i = pl.program_id(1)
    tm = u_ref.shape[0]

    @pl.when(i == 0)
    def _():
        wb_ref[...] = wg_ref[...].astype(BF16)

    win = jnp.left_shift(jnp.int32(C_WINDOWS[0]), g)
    t_i = lax.broadcasted_iota(jnp.int32, (tm, tm), 0)
    s_i = lax.broadcasted_iota(jnp.int32, (tm, tm), 1)
    dist = t_i - s_i
    band = ((dist >= 0) & (dist < win)).astype(BF16)
    t_h = lax.broadcasted_iota(jnp.int32, (tm, C_HALO), 0)
    s_h = lax.broadcasted_iota(jnp.int32, (tm, C_HALO), 1)
    band_halo = ((t_h + C_HALO - s_h < win) & (i > 0)).astype(BF16)
    u = u_ref[...]
    total = (jnp.dot(band, u, preferred_element_type=F32)
             + jnp.dot(band_halo, halo_ref[...], preferred_element_type=F32))
    t_glob = i * tm + lax.broadcasted_iota(jnp.int32, (tm, 1), 0)
    count = jnp.minimum(t_glob + 1, win).astype(F32)
    pooled = total / count - u.astype(F32)
    y = jnp.dot(pooled.astype(BF16), wb_ref[...], preferred_element_type=F32) * scale_ref[...]
    o_ref[...] = y.astype(o_ref.dtype)


def pooling_mixer(h, hb, norm_gain, w_in, layer, w_group, scale, w_out, tm=512):
    seq, d = h.shape
    assert all(C_WINDOWS[g] == C_WINDOWS[0] << g for g in range(len(C_WINDOWS)))
    assert max(C_WINDOWS) <= C_HALO
    u = fused_matmul(hb, w_in, layer, 0, d, _ep_identity, out_dtype=BF16, norm_gain=norm_gain,
                     name="pool_in_proj")
    tm = min(tm, seq)
    gd = C_GROUP_DIM
    halo_per_tile = tm // C_HALO
    y = pl.pallas_call(
        _pool_kernel,
        out_shape=jax.ShapeDtypeStruct((seq, d), BF16),
        grid=(len(C_WINDOWS), seq // tm),
        in_specs=[pl.BlockSpec((tm, gd), lambda g, i: (i, g)),
                  pl.BlockSpec((C_HALO, gd), lambda g, i: (jnp.maximum(i * halo_per_tile - 1, 0), g)),
                  pl.BlockSpec((None, None, gd, gd), lambda g, i: (layer, g, 0, 0)),
                  pl.BlockSpec((1, gd), lambda g, i: (0, g))],
        out_specs=pl.BlockSpec((tm, gd), lambda g, i: (i, g)),
        scratch_shapes=[pltpu.VMEM((gd, gd), BF16)],
        compiler_params=_params(2),
        name="pool_group",
    )(u, u, w_group, scale.reshape(1, d))
    return matmul_residual(y, w_out, layer, h, name="pool_out_proj")


def squared_relu_mlp(h, hb, norm_gain, w_in, w_out, layer):
    hidden, w_out_bf16 = fused_matmul(hb, w_in, layer, 0, w_in.shape[2], _ep_relu_sq, out_dtype=BF16,
                                      norm_gain=norm_gain, side=(w_out, layer), name="mlp_up")
    return matmul_kgrid_residual(hidden, w_out_bf16, h, name="mlp_down")


def kernel(x, p, norm_mix, norm_mlp, norm_ple, w_ple, w_ple_gate, w_mlp_in, w_mlp_out, a_w_in, a_q_norm, a_k_norm, a_w_out, b_w_in, b_gate_bias, b_h_norm, b_w_out, c_w_in, c_w_group, c_scale, c_w_out):
    bsz, seq, d = x.shape
    depth = p.shape[0]
    rope = rope_tables(seq)
    outs = []
    for b in range(bsz):
        h = x.reshape(seq, d) if bsz == 1 else x[b]
        hb = None
        for i in range(depth):
            kind, j = i % 3, i // 3
            if kind == 0:
                h, hb = dilated_attention_mixer(h, norm_mix[i], a_w_in, j, a_q_norm[j], a_k_norm[j], a_w_out, rope)
            elif kind == 1:
                h, hb = mlstm_mixer(h, hb, norm_mix[i], b_w_in, j, b_gate_bias[j], b_h_norm[j], b_w_out)
            else:
                h, hb = pooling_mixer(h, hb, norm_mix[i], c_w_in, j, c_w_group, c_scale[j], c_w_out)
            h, hb = squared_relu_mlp(h, hb, norm_mlp[i], w_mlp_in, w_mlp_out, i)
            h, hb = ple_update(h, hb, norm_ple[i], w_ple_gate, i,
                               p[i, b].astype(BF16), w_ple[i].astype(BF16))
        outs.append(h)
    return outs[0][None] if bsz == 1 else jnp.stack(outs, axis=0)
```

```python
import functools
import math

import jax
import jax.numpy as jnp
from jax import lax
from jax.experimental import pallas as pl
from jax.experimental.pallas import tpu as pltpu

F32 = jnp.float32
BF16 = jnp.bfloat16

RMS_EPS = 1e-6
ROPE_THETA = 10000.0

A_HEAD_DIM = 128
A_HEADS_PER_GROUP = 16
A_PATTERNS = ((128, 1), (512, 4), (2048, 16))
A_STEPS = 128
A_RESIDUES = 16
A_GROUP_WIDTH = A_HEADS_PER_GROUP * A_HEAD_DIM

B_HEADS = 8
B_QK_DIM = 256
B_V_DIM = 512
B_CHUNK = 256
B_GATE_PAD = 128

C_WINDOWS = (2, 4, 8, 16)
C_GROUP_DIM = 1024
C_HALO = 128

LANES = 128
VMEM_LIMIT_BYTES = 60 * 1024 * 1024


def _params(n_grid_axes):
    return pltpu.CompilerParams(
        dimension_semantics=("arbitrary",) * n_grid_axes,
        vmem_limit_bytes=VMEM_LIMIT_BYTES)


def _rmsnorm_rows(x, gain):
    inv = lax.rsqrt(jnp.mean(x * x, axis=-1, keepdims=True) + RMS_EPS)
    return x * inv * gain


def _row_permutation(tm, rows, to_residue_major):
    a = lax.broadcasted_iota(jnp.int32, (tm, tm), 0)
    b = lax.broadcasted_iota(jnp.int32, (tm, tm), 1)
    major, seq_order = (a, b) if to_residue_major else (b, a)
    return (seq_order == (major % rows) * A_RESIDUES + major // rows).astype(BF16)


def _rmsnorm_residue_major_kernel(x_ref, g_ref, o_ref):
    tm = x_ref.shape[0]
    rows = tm // A_RESIDUES
    y = _rmsnorm_rows(x_ref[...], g_ref[...]).astype(BF16)
    y = jnp.dot(_row_permutation(tm, rows, True), y, preferred_element_type=F32)
    o_ref[...] = y.astype(o_ref.dtype).reshape(o_ref.shape)


def rmsnorm_residue_major(h, gain, tm=256):
    m, d = h.shape
    tm = min(tm, m)
    in_specs = [pl.BlockSpec((tm, d), lambda i: (i, 0)),
                pl.BlockSpec((1, d), lambda i: (0, 0))]
    out = pl.pallas_call(
        _rmsnorm_residue_major_kernel,
        out_shape=jax.ShapeDtypeStruct((A_RESIDUES, m // A_RESIDUES, d), BF16),
        grid=(m // tm,), in_specs=in_specs,
        out_specs=pl.BlockSpec((A_RESIDUES, tm // A_RESIDUES, d), lambda i: (0, i, 0)),
        compiler_params=_params(1), name="rmsnorm_residue_major",
    )(h, gain.reshape(1, d))
    return out.reshape(m, d)


SIDE_CAST_ROWS = 64
NORM_ROWS = 64


def _row_inv_rms(x_ref):
    tm, k = x_ref.shape
    rows = min(NORM_ROWS, tm)
    pieces = []
    for r in range(tm // rows):
        part = None
        for c in range(k // LANES):
            xc = x_ref[r * rows:(r + 1) * rows, c * LANES:(c + 1) * LANES].astype(F32)
            part = xc * xc if part is None else part + xc * xc
        pieces.append(jnp.sum(part, axis=-1, keepdims=True))
    return lax.rsqrt(jnp.concatenate(pieces, axis=0) * (1.0 / k) + RMS_EPS)


def _mm_kernel(x_ref, wchunk_ref, *rest, epilogue, n_extra, n_out, n_slabs, n_side_steps, normed):
    if normed:
        gcol_ref, rest = rest[0], rest[1:]
    extra_refs, rest = rest[:n_extra], rest[n_extra:]
    if n_side_steps:
        side_ref, rest = rest[0], rest[1:]
    o_refs, rest = rest[:n_out], rest[n_out:]
    if n_side_steps:
        side_out_ref, rest = rest[0], rest[1:]
    if normed:
        wb_refs, inv_ref = rest[:2], rest[2]
    else:
        wb_refs = rest
    jj, i = pl.program_id(0), pl.program_id(1)
    chunk = wchunk_ref.shape[0]

    def cast_jobs(dst_ref):
        row0 = pl.multiple_of(i * chunk, chunk)
        wc = wchunk_ref[...]
        if normed:
            wc = wc * gcol_ref[...]
        dst_ref[pl.ds(row0, chunk), :] = wc.astype(BF16)
        if n_side_steps:
            side_out_ref[...] = side_ref[...].astype(BF16)

    @pl.when(jj == 0)
    def _():
        cast_jobs(wb_refs[0])

    if normed:
        tile_lane = lax.broadcasted_iota(jnp.int32, inv_ref.shape, 1) == i

        @pl.when((jj == 0) & (i == 0))
        def _():
            inv_ref[...] = jnp.zeros_like(inv_ref)

        @pl.when(jj == 1)
        def _():
            inv_ref[...] = jnp.where(tile_lane, _row_inv_rms(x_ref), inv_ref[...])

    def multiply(cur_ref, next_ref):
        cast_jobs(next_ref)
        acc = jnp.dot(x_ref[...], cur_ref[...], preferred_element_type=F32)
        if normed:
            acc = acc * jnp.sum(jnp.where(tile_lane, inv_ref[...], 0.0), axis=-1, keepdims=True)
        vals = epilogue(acc, slice(None), *extra_refs)
        vals = vals if isinstance(vals, tuple) else (vals,)
        for o_ref, val in zip(o_refs, vals, strict=True):
            o_ref[...] = val.astype(o_ref.dtype)

    @pl.when(jj % 2 == 1)
    def _():
        multiply(wb_refs[0], wb_refs[1])

    @pl.when((jj > 0) & (jj % 2 == 0))
    def _():
        multiply(wb_refs[1], wb_refs[0])


def fused_matmul(x, w, layer, col0, n, epilogue, *, out_dtype, tm=1024, tn=1024, extras=(), side=None,
                 norm_gain=None, name):
    m, k = x.shape
    tm, tn = min(tm, m), min(tn, n)
    n_slabs, n_rows = n // tn, m // tm
    assert w.shape[1] == k and m % tm == 0 and n % tn == 0 and col0 % tn == 0 and k % n_rows == 0, (
        w.shape, m, n, tm, tn, col0)
    jb = col0 // tn
    chunk = k // n_rows
    out_dtypes = out_dtype if isinstance(out_dtype, tuple) else (out_dtype,)

    def lag(im):
        return lambda jj, i: im(jnp.maximum(jj - 1, 0), jnp.where(jj > 0, i, 0))

    chunk_row = lambda jj, i: jnp.where(jj < n_slabs, i, n_rows - 1)
    in_specs = [pl.BlockSpec((tm, k), lag(lambda j, i: (i, 0))),
                pl.BlockSpec((None, chunk, tn),
                             lambda jj, i: (layer, chunk_row(jj, i), jb + jnp.minimum(jj, n_slabs - 1)))]
    operands = [x, w]
    if norm_gain is not None:
        assert n_rows <= LANES
        in_specs.append(pl.BlockSpec((chunk, 1), lambda jj, i: (chunk_row(jj, i), 0)))
        operands.append(norm_gain.reshape(k, 1))
    in_specs += [pl.BlockSpec(bs, lag(im)) for _, bs, im in extras]
    operands += [a for a, _, _ in extras]
    out_shape = [jax.ShapeDtypeStruct((m, n), dt) for dt in out_dtypes]
    out_specs = [pl.BlockSpec((tm, tn), lag(lambda j, i: (i, j))) for _ in out_dtypes]
    n_side_steps = 0
    if side is not None:
        side_w, side_layer = side
        side_rows, side_cols = side_w.shape[1:]
        rows_per_step = SIDE_CAST_ROWS
        while side_rows // rows_per_step > (n_slabs + 1) * n_rows:
            rows_per_step *= 2
        n_side_steps = side_rows // rows_per_step
        assert side_rows % rows_per_step == 0
        side_block = lambda jj, i: jnp.minimum(jj * n_rows + i, n_side_steps - 1)
        in_specs.append(pl.BlockSpec((None, rows_per_step, side_cols),
                                     lambda jj, i: (side_layer, side_block(jj, i), 0)))
        operands.append(side_w)
        out_shape.append(jax.ShapeDtypeStruct((side_rows, side_cols), BF16))
        out_specs.append(pl.BlockSpec((rows_per_step, side_cols), lambda jj, i: (side_block(jj, i), 0)))
    outs = pl.pallas_call(
        functools.partial(_mm_kernel, epilogue=epilogue, n_extra=len(extras), n_out=len(out_dtypes),
                          n_slabs=n_slabs, n_side_steps=n_side_steps, normed=norm_gain is not None),
        out_shape=out_shape,
        grid=(n_slabs + 1, n_rows),
        in_specs=in_specs,
        out_specs=out_specs,
        scratch_shapes=[pltpu.VMEM((k, tn), BF16), pltpu.VMEM((k, tn), BF16)] + (
            [pltpu.VMEM((tm, LANES), F32)] if norm_gain is not None else []),
        compiler_params=_params(2),
        name=name,
    )(*operands)
    return outs[0] if len(outs) == 1 else tuple(outs)


def _ep_identity(acc, cs):
    return acc


def _ep_relu_sq(acc, cs):
    r = jnp.maximum(acc, 0.0)
    return r * r


def _ep_residual(acc, cs, res_ref):
    h = res_ref[:, cs] + acc
    return h, h


def matmul_residual(x, w, layer, res, *, tm=512, tn=1024, name):
    n = w.shape[2]
    tm, tn = min(tm, x.shape[0]), min(tn, n)
    return fused_matmul(x, w, layer, 0, n, _ep_residual, out_dtype=(F32, BF16), tm=tm, tn=tn,
                        extras=[(res, (tm, tn), lambda j, i: (i, j))], name=name)


def _mm_kgrid_kernel(x_ref, w_ref, res_ref, o_ref, ob_ref, acc_ref):
    kk = pl.program_id(2)

    @pl.when(kk == 0)
    def _():
        acc_ref[...] = jnp.zeros_like(acc_ref)

    acc_ref[...] += jnp.dot(x_ref[...], w_ref[...], preferred_element_type=F32)

    @pl.when(kk == pl.num_programs(2) - 1)
    def _():
        h = res_ref[...] + acc_ref[...]
        o_ref[...] = h
        ob_ref[...] = h.astype(ob_ref.dtype)


def matmul_kgrid_residual(x, w, res, *, tm=1024, tn=1024, tk=2048, name):
    m, k = x.shape
    n = w.shape[1]
    tm, tn, tk = min(tm, m), min(tn, n), min(tk, k)
    assert m % tm == 0 and n % tn == 0 and k % tk == 0
    return pl.pallas_call(
        _mm_kgrid_kernel,
        out_shape=(jax.ShapeDtypeStruct((m, n), F32), jax.ShapeDtypeStruct((m, n), BF16)),
        grid=(m // tm, n // tn, k // tk),
        in_specs=[pl.BlockSpec((tm, tk), lambda i, j, kk: (i, kk)),
                  pl.BlockSpec((tk, tn), lambda i, j, kk: (kk, j)),
                  pl.BlockSpec((tm, tn), lambda i, j, kk: (i, j))],
        out_specs=(pl.BlockSpec((tm, tn), lambda i, j, kk: (i, j)),
                   pl.BlockSpec((tm, tn), lambda i, j, kk: (i, j))),
        scratch_shapes=[pltpu.VMEM((tm, tn), F32)],
        compiler_params=_params(3),
        name=name,
    )(x, w, res)


def _ep_ple(acc, cs, p_ref, wp_ref, res_ref):
    emb = jnp.dot(p_ref[...], wp_ref[:, cs], preferred_element_type=F32)
    h = res_ref[:, cs] + jax.nn.sigmoid(acc) * emb
    return h, h


def ple_update(h, hb, norm_gain, w_gate, layer, p, w_ple, *, tm=512, tn=1024):
    n = w_gate.shape[2]
    tm, tn = min(tm, h.shape[0]), min(tn, n)
    kp = p.shape[1]
    return fused_matmul(
        hb, w_gate, layer, 0, n, _ep_ple, out_dtype=(F32, BF16), tm=tm, tn=tn, norm_gain=norm_gain,
        extras=[(p, (tm, kp), lambda j, i: (i, 0)),
                (w_ple, (kp, tn), lambda j, i: (0, j)),
                (h, (tm, tn), lambda j, i: (i, j))],
        name="ple_gate")


def _rope_table_kernel(cos_ref, sin_ref, *, rows_per_residue):
    tm = cos_ref.shape[0]
    half = A_HEAD_DIM // 2
    row0 = pl.program_id(0) * tm
    residue = row0 // rows_per_residue
    n0 = row0 % rows_per_residue
    n = n0 + lax.broadcasted_iota(jnp.int32, (tm, A_HEAD_DIM), 0)
    pos = (n * A_RESIDUES + residue).astype(F32)
    lane = lax.broadcasted_iota(jnp.int32, (tm, A_HEAD_DIM), 1)
    j = jnp.where(lane >= half, lane - half, lane).astype(F32)
    inv_freq = jnp.exp(j * (-2.0 * math.log(ROPE_THETA) / A_HEAD_DIM))
    ang = pos * inv_freq
    cos_ref[...] = jnp.cos(ang)
    sin_ref[...] = jnp.where(lane >= half, 1.0, -1.0) * jnp.sin(ang)


def rope_tables(seq, tm=256):
    rows_per_residue = seq // A_RESIDUES
    tm = min(tm, rows_per_residue)
    assert rows_per_residue % tm == 0
    spec = pl.BlockSpec((tm, A_HEAD_DIM), lambda i: (i, 0))
    shape = jax.ShapeDtypeStruct((seq, A_HEAD_DIM), F32)
    return pl.pallas_call(
        functools.partial(_rope_table_kernel, rows_per_residue=rows_per_residue),
        out_shape=(shape, shape), grid=(seq // tm,),
        in_specs=[], out_specs=(spec, spec), compiler_params=_params(1),
        name="rope_tables")()


def _ep_qk_norm_rope(acc, cs, gain_ref, cos_ref, sin_ref):
    cos, sin = cos_ref[...], sin_ref[...]
    gain = gain_ref[:, cs]
    d = A_HEAD_DIM
    ones = jnp.ones((d, d), BF16)
    r_i = lax.broadcasted_iota(jnp.int32, (d, d), 0)
    c_i = lax.broadcasted_iota(jnp.int32, (d, d), 1)
    swap_halves = (r_i == (c_i + d // 2) % d).astype(BF16)
    outs = []
    for hh in range(acc.shape[1] // d):
        hs = slice(hh * d, (hh + 1) * d)
        a = acc[:, hs]
        sum_sq = jnp.dot((a * a).astype(BF16), ones, preferred_element_type=F32)
        inv = lax.rsqrt(sum_sq * (1.0 / d) + RMS_EPS)
        t = a * gain[:, hs]
        t_swapped = jnp.dot(t.astype(BF16), swap_halves, preferred_element_type=F32)
        outs.append((t * cos + t_swapped * sin) * inv)
    return jnp.concatenate(outs, axis=1)


def _attn_kernel(q_ref, kc_ref, kp_ref, vc_ref, vp_ref, o_ref, lse_ref, *, parts):
    rows = q_ref.shape[-2]
    qb = parts * rows
    n = pl.program_id(1)

    def load(ref, hs):
        return ref[..., hs].reshape(qb, A_HEAD_DIM)

    def step_of(idx):
        return idx if parts == 1 else (idx % rows) * parts + idx // rows

    lq = step_of(lax.broadcasted_iota(jnp.int32, (qb, 2 * qb), 0))
    col = lax.broadcasted_iota(jnp.int32, (qb, 2 * qb), 1)
    is_prev = col < qb
    lk = step_of(jnp.where(is_prev, col, col - qb))
    dist = lq - lk + jnp.where(is_prev, qb, 0)
    ok = (dist >= 0) & (dist <= A_STEPS) & ((n > 0) | jnp.logical_not(is_prev))
    lane = lax.broadcasted_iota(jnp.int32, (qb, LANES), 1)
    scale = A_HEAD_DIM ** -0.5
    nt = (((1,), (1,)), ((), ()))
    ones = jnp.ones((2 * qb, A_HEAD_DIM), BF16)
    lse_tile = jnp.zeros((qb, LANES), F32)
    for hh in range(A_HEADS_PER_GROUP):
        hs = slice(hh * A_HEAD_DIM, (hh + 1) * A_HEAD_DIM)
        q = load(q_ref, hs)
        k_cat = jnp.concatenate([load(kp_ref, hs), load(kc_ref, hs)], axis=0)
        v_cat = jnp.concatenate([load(vp_ref, hs), load(vc_ref, hs)], axis=0)
        s = lax.dot_general(q, k_cat, nt, preferred_element_type=F32) * scale
        s = jnp.where(ok, s, -jnp.inf)
        mx = jnp.max(jnp.maximum(s[:, :qb], s[:, qb:]), axis=-1, keepdims=True)
        e = jnp.exp(s - mx).astype(BF16)
        pv = jnp.dot(e, jnp.concatenate([v_cat, ones], axis=1), preferred_element_type=F32)
        den = pv[:, A_HEAD_DIM:]
        o_ref[..., hs] = (pv[:, :A_HEAD_DIM] / den).reshape(o_ref.shape[:-1] + (A_HEAD_DIM,)).astype(o_ref.dtype)
        lse_tile = jnp.where(lane == hh, mx + jnp.log(den), lse_tile)
    lse_ref[...] = lse_tile.reshape(lse_ref.shape)


def dilated_attention(qk, v, group, dilation):
    seq = qk.shape[0]
    gw = A_GROUP_WIDTH
    n_groups = len(A_PATTERNS)
    parts = A_RESIDUES // dilation
    rpr = seq // A_RESIDUES
    rows = max(A_STEPS // parts, 16)
    nb = rpr // rows
    qk_w, v_w = qk.shape[1], v.shape[1]
    if parts == 1:
        vshape = lambda w: (seq, w)
        blk = lambda w: (rows, w)
        at = lambda r, n, c: (r * nb + n, c)
    else:
        vshape = lambda w: (parts, dilation, rpr, w)
        blk = lambda w: (parts, None, rows, w)
        at = lambda r, n, c: (0, r, n, c)
    view = lambda a: a.reshape(vshape(a.shape[1]))
    prev = lambda n: jnp.maximum(n - 1, 0)
    o, lse = pl.pallas_call(
        functools.partial(_attn_kernel, parts=parts),
        out_shape=(jax.ShapeDtypeStruct(vshape(gw), BF16),
                   jax.ShapeDtypeStruct(vshape(LANES), F32)),
        grid=(dilation, nb),
        in_specs=[pl.BlockSpec(blk(gw), lambda r, n: at(r, n, group)),
                  pl.BlockSpec(blk(gw), lambda r, n: at(r, n, n_groups + group)),
                  pl.BlockSpec(blk(gw), lambda r, n: at(r, prev(n), n_groups + group)),
                  pl.BlockSpec(blk(gw), lambda r, n: at(r, n, group)),
                  pl.BlockSpec(blk(gw), lambda r, n: at(r, prev(n), group))],
        out_specs=(pl.BlockSpec(blk(gw), lambda r, n: at(r, n, 0)),
                   pl.BlockSpec(blk(LANES), lambda r, n: at(r, n, 0))),
        compiler_params=_params(2),
        name=f"dilated_attn_g{group}",
    )(view(qk), view(qk), view(qk), view(v), view(v))
    return o.reshape(seq, gw), lse.reshape(seq, LANES)


def _attn_merge_kernel(o0_ref, o1_ref, o2_ref, l0_ref, l1_ref, l2_ref, out_ref):
    rows = o0_ref.shape[1]
    tm = A_RESIDUES * rows

    def flat(ref, sl=slice(None)):
        return ref[:, :, sl].reshape(tm, -1)

    l0, l1, l2 = flat(l0_ref), flat(l1_ref), flat(l2_ref)
    mx = jnp.maximum(jnp.maximum(l0, l1), l2)
    w0, w1, w2 = jnp.exp(l0 - mx), jnp.exp(l1 - mx), jnp.exp(l2 - mx)
    tot = w0 + w1 + w2
    w0, w1, w2 = w0 / tot, w1 / tot, w2 / tot
    to_sequence_order = _row_permutation(tm, rows, False)
    for hh in range(A_HEADS_PER_GROUP):
        hs = slice(hh * A_HEAD_DIM, (hh + 1) * A_HEAD_DIM)
        merged = (w0[:, hh:hh + 1] * flat(o0_ref, hs).astype(F32)
                  + w1[:, hh:hh + 1] * flat(o1_ref, hs).astype(F32)
                  + w2[:, hh:hh + 1] * flat(o2_ref, hs).astype(F32)).astype(BF16)
        out_ref[:, hs] = jnp.dot(to_sequence_order, merged, preferred_element_type=F32).astype(out_ref.dtype)


def attention_merge(outs, lses, rows=16):
    seq, gw = outs[0].shape
    rpr = seq // A_RESIDUES
    rows = min(rows, rpr)
    tm = A_RESIDUES * rows
    o_spec = pl.BlockSpec((A_RESIDUES, rows, gw), lambda i: (0, i, 0))
    l_spec = pl.BlockSpec((A_RESIDUES, rows, LANES), lambda i: (0, i, 0))
    return pl.pallas_call(
        _attn_merge_kernel,
        out_shape=jax.ShapeDtypeStruct((seq, gw), BF16),
        grid=(rpr // rows,),
        in_specs=[o_spec] * 3 + [l_spec] * 3,
        out_specs=pl.BlockSpec((tm, gw), lambda i: (i, 0)),
        compiler_params=_params(1),
        name="attn_merge",
    )(*[o.reshape(A_RESIDUES, rpr, gw) for o in outs],
      *[l.reshape(A_RESIDUES, rpr, LANES) for l in lses])


def dilated_attention_mixer(h, norm_gain, w_in, layer, q_gain, k_gain, w_out, rope):
    seq = h.shape[0]
    n_heads = len(A_PATTERNS) * A_HEADS_PER_GROUP
    qk_width = 2 * n_heads * A_HEAD_DIM
    v_width = n_heads * A_HEAD_DIM
    assert seq % (A_RESIDUES * A_STEPS) == 0
    assert all(w // d == A_STEPS and A_RESIDUES % d == 0 for w, d in A_PATTERNS)
    xn = rmsnorm_residue_major(h, norm_gain)
    cos, sin = rope
    gain = jnp.concatenate([jnp.tile(q_gain, n_heads), jnp.tile(k_gain, n_heads)]).reshape(1, qk_width)
    tm, tn = min(1024, seq), 1024
    qk = fused_matmul(
        xn, w_in, layer, 0, qk_width, _ep_qk_norm_rope, out_dtype=BF16, tm=tm, tn=tn,
        extras=[(gain, (1, tn), lambda j, i: (0, j)),
                (cos, (tm, A_HEAD_DIM), lambda j, i: (i, 0)),
                (sin, (tm, A_HEAD_DIM), lambda j, i: (i, 0))],
        name="attn_qk_proj")
    v = fused_matmul(xn, w_in, layer, qk_width, v_width, _ep_identity, out_dtype=BF16,
                     tm=tm, tn=tn, name="attn_v_proj")
    outs, lses = [], []
    for g, (_, dilation) in enumerate(A_PATTERNS):
        o, lse = dilated_attention(qk, v, g, dilation)
        outs.append(o)
        lses.append(lse)
    merged = attention_merge(outs, lses)
    return matmul_residual(merged, w_out, layer, h, name="attn_out_proj")


def _ep_gates(acc, cs, bias_ref):
    lane = lax.broadcasted_iota(jnp.int32, acc.shape, 1)
    g = jnp.where(lane < 2 * B_HEADS, acc + bias_ref[...], 0.0)
    return jnp.where(lane >= B_HEADS, jax.nn.log_sigmoid(g), g)


def _mlstm_kernel(q_ref, k_ref, v_ref, og_ref, gcol_ref, grow_ref, gain_ref, o_ref,
                  c_ref, n_ref, m_ref):
    hd = pl.program_id(0)
    L = q_ref.shape[0]
    hi = lax.Precision.HIGHEST

    @pl.when(pl.program_id(1) == 0)
    def _():
        c_ref[...] = jnp.zeros_like(c_ref)
        n_ref[...] = jnp.zeros_like(n_ref)
        m_ref[...] = jnp.zeros_like(m_ref)

    gcol = gcol_ref[...]
    lane = lax.broadcasted_iota(jnp.int32, gcol.shape, 1)
    ig_col = jnp.sum(jnp.where(lane == hd, gcol, 0.0), axis=-1, keepdims=True)
    lf_col = jnp.sum(jnp.where(lane == hd + B_HEADS, gcol, 0.0), axis=-1, keepdims=True)
    ig_row = grow_ref[pl.ds(hd, 1), :]
    lf_row = grow_ref[pl.ds(hd + B_HEADS, 1), :]

    t_i = lax.broadcasted_iota(jnp.int32, (L, L), 0)
    s_i = lax.broadcasted_iota(jnp.int32, (L, L), 1)
    causal = s_i <= t_i
    lower = causal.astype(F32)
    upper = (t_i <= s_i).astype(F32)
    b_col = jnp.dot(lower, jnp.broadcast_to(lf_col, (L, LANES)), precision=hi,
                    preferred_element_type=F32)[:, :1]
    b_row = jnp.dot(jnp.broadcast_to(lf_row, (8, L)), upper, precision=hi,
                    preferred_element_type=F32)[:1, :]

    m_prev = m_ref[:1, :1]
    c_prev = c_ref[...]
    n_prev = n_ref[...]

    q = q_ref[...] * jnp.asarray(B_QK_DIM ** -0.5, q_ref.dtype)
    k = k_ref[...]
    v = v_ref[...]

    log_d = jnp.where(causal, b_col - b_row + ig_row, -jnp.inf)
    log_inter = b_col + m_prev
    m_t = jnp.maximum(log_inter, jnp.max(log_d, axis=-1, keepdims=True))
    scores = lax.dot_general(q, k, (((1,), (1,)), ((), ())), preferred_element_type=F32)
    w = scores * jnp.exp(log_d - m_t)
    decay = jnp.exp(log_inter - m_t)
    num = (decay * jnp.dot(q, c_prev.astype(BF16), preferred_element_type=F32)
           + jnp.dot(w.astype(BF16), v, preferred_element_type=F32))
    qn = (decay * jnp.sum(q.astype(F32) * n_prev, axis=-1, keepdims=True)
          + jnp.sum(w, axis=-1, keepdims=True))
    hval = num / jnp.maximum(jnp.abs(qn), jnp.exp(-m_t))

    inv = lax.rsqrt(jnp.mean(hval * hval, axis=-1, keepdims=True) + RMS_EPS)
    o_ref[...] = (hval * inv * gain_ref[...] * jax.nn.sigmoid(og_ref[...].astype(F32))).astype(o_ref.dtype)

    b_last = b_col[L - 1:L, :]
    log_w = b_last - b_col + ig_col
    m_new = jnp.maximum(b_last + m_prev, jnp.max(log_w, axis=0, keepdims=True))
    w_s = jnp.exp(log_w - m_new)
    carry = jnp.exp(b_last + m_prev - m_new)
    k_w = k.astype(F32) * w_s
    c_ref[...] = carry * c_prev + lax.dot_general(
        k_w.astype(BF16), v, (((0,), (0,)), ((), ())), preferred_element_type=F32)
    n_ref[...] = carry * n_prev + jnp.sum(k_w, axis=0, keepdims=True)
    m_ref[...] = jnp.broadcast_to(m_new, m_ref.shape)


def mlstm_mixer(h, hb, norm_gain, w_in, layer, gate_bias, h_gain, w_out):
    seq, d = h.shape
    qk_w = B_HEADS * B_QK_DIM
    v_w = B_HEADS * B_V_DIM
    main_w = 2 * qk_w + v_w + d
    proj = fused_matmul(hb, w_in, layer, 0, main_w, _ep_identity, out_dtype=BF16, norm_gain=norm_gain,
                        name="mlstm_in_proj")
    n_gate = 2 * B_HEADS
    assert w_in.shape[2] == main_w + n_gate
    bias = jnp.pad(gate_bias, (0, B_GATE_PAD - n_gate)).reshape(1, B_GATE_PAD)
    gates = fused_matmul(hb, w_in, layer, main_w, B_GATE_PAD, _ep_gates, out_dtype=F32, tn=B_GATE_PAD,
                         norm_gain=norm_gain, extras=[(bias, (1, B_GATE_PAD), lambda j, i: (0, 0))],
                         name="mlstm_gate_proj")
    gates_t = gates[:, :n_gate].T
    L = min(B_CHUNK, seq)
    nq = qk_w // B_QK_DIM
    nv = v_w // B_V_DIM
    mixed = pl.pallas_call(
        _mlstm_kernel,
        out_shape=jax.ShapeDtypeStruct((seq, v_w), BF16),
        grid=(B_HEADS, seq // L),
        in_specs=[pl.BlockSpec((L, B_QK_DIM), lambda hd, c: (c, hd)),
                  pl.BlockSpec((L, B_QK_DIM), lambda hd, c: (c, nq + hd)),
                  pl.BlockSpec((L, B_V_DIM), lambda hd, c: (c, (2 * qk_w) // B_V_DIM + hd)),
                  pl.BlockSpec((L, B_V_DIM), lambda hd, c: (c, (2 * qk_w) // B_V_DIM + nv + hd)),
                  pl.BlockSpec((L, B_GATE_PAD), lambda hd, c: (c, 0)),
                  pl.BlockSpec((n_gate, L), lambda hd, c: (0, c)),
                  pl.BlockSpec((1, B_V_DIM), lambda hd, c: (0, hd))],
        out_specs=pl.BlockSpec((L, B_V_DIM), lambda hd, c: (c, hd)),
        scratch_shapes=[pltpu.VMEM((B_QK_DIM, B_V_DIM), F32),
                        pltpu.VMEM((1, B_QK_DIM), F32),
                        pltpu.VMEM((8, LANES), F32)],
        compiler_params=_params(2),
        name="mlstm_chunks",
    )(proj, proj, proj, proj, gates, gates_t, h_gain.reshape(1, v_w))
    return matmul_residual(mixed, w_out, layer, h, name="mlstm_out_proj")


def _pool_kernel(u_ref, halo_ref, wg_ref, scale_ref, o_ref, wb_ref):
    g = pl.program_id(0)
    i = pl.program_id(1)
    tm = u_ref.shape[0]

    @pl.when(i == 0)
    def _():
        wb_ref[...] = wg_ref[...].astype(BF16)

    win = jnp.left_shift(jnp.int32(C_WINDOWS[0]), g)
    t_i = lax.broadcasted_iota(jnp.int32, (tm, tm), 0)
    s_i = lax.broadcasted_iota(jnp.int32, (tm, tm), 1)
    dist = t_i - s_i
    band = ((dist >= 0) & (dist < win)).astype(BF16)
    t_h = lax.broadcasted_iota(jnp.int32, (tm, C_HALO), 0)
    s_h = lax.broadcasted_iota(jnp.int32, (tm, C_HALO), 1)
    band_halo = ((t_h + C_HALO - s_h < win) & (i > 0)).astype(BF16)
    u = u_ref[...]
    total = (jnp.dot(band, u, preferred_element_type=F32)
             + jnp.dot(band_halo, halo_ref[...], preferred_element_type=F32))
    t_glob = i * tm + lax.broadcasted_iota(jnp.int32, (tm, 1), 0)
    count = jnp.minimum(t_glob + 1, win).astype(F32)
    pooled = total / count - u.astype(F32)
    y = jnp.dot(pooled.astype(BF16), wb_ref[...], preferred_element_type=F32) * scale_ref[...]
    o_ref[...] = y.astype(o_ref.dtype)


def pooling_mixer(h, hb, norm_gain, w_in, layer, w_group, scale, w_out, tm=512):
    seq, d = h.shape
    assert all(C_WINDOWS[g] == C_WINDOWS[0] << g for g in range(len(C_WINDOWS)))
    assert max(C_WINDOWS) <= C_HALO
    u = fused_matmul(hb, w_in, layer, 0, d, _ep_identity, out_dtype=BF16, norm_gain=norm_gain,
                     name="pool_in_proj")
    tm = min(tm, seq)
    gd = C_GROUP_DIM
    halo_per_tile = tm // C_HALO
    y = pl.pallas_call(
        _pool_kernel,
        out_shape=jax.ShapeDtypeStruct((seq, d), BF16),
        grid=(len(C_WINDOWS), seq // tm),
        in_specs=[pl.BlockSpec((tm, gd), lambda g, i: (i, g)),
                  pl.BlockSpec((C_HALO, gd), lambda g, i: (jnp.maximum(i * halo_per_tile - 1, 0), g)),
                  pl.BlockSpec((None, None, gd, gd), lambda g, i: (layer, g, 0, 0)),
                  pl.BlockSpec((1, gd), lambda g, i: (0, g))],
        out_specs=pl.BlockSpec((tm, gd), lambda g, i: (i, g)),
        scratch_shapes=[pltpu.VMEM((gd, gd), BF16)],
        compiler_params=_params(2),
        name="pool_group",
    )(u, u, w_group, scale.reshape(1, d))
    return matmul_residual(y, w_out, layer, h, name="pool_out_proj")


def squared_relu_mlp(h, hb, norm_gain, w_in, w_out, layer):
    hidden, w_out_bf16 = fused_matmul(hb, w_in, layer, 0, w_in.shape[2], _ep_relu_sq, out_dtype=BF16,
                                      norm_gain=norm_gain, side=(w_out, layer), name="mlp_up")
    return matmul_kgrid_residual(hidden, w_out_bf16, h, name="mlp_down")


def kernel(x, p, norm_mix, norm_mlp, norm_ple, w_ple, w_ple_gate, w_mlp_in, w_mlp_out, a_w_in, a_q_norm, a_k_norm, a_w_out, b_w_in, b_gate_bias, b_h_norm, b_w_out, c_w_in, c_w_group, c_scale, c_w_out):
    bsz, seq, d = x.shape
    depth = p.shape[0]
    rope = rope_tables(seq)
    outs = []
    for b in range(bsz):
        h = x.reshape(seq, d) if bsz == 1 else x[b]
        hb = None
        for i in range(depth):
            kind, j = i % 3, i // 3
            if kind == 0:
                h, hb = dilated_attention_mixer(h, norm_mix[i], a_w_in, j, a_q_norm[j], a_k_norm[j], a_w_out, rope)
            elif kind == 1:
                h, hb = mlstm_mixer(h, hb, norm_mix[i], b_w_in, j, b_gate_bias[j], b_h_norm[j], b_w_out)
            else:
                h, hb = pooling_mixer(h, hb, norm_mix[i], c_w_in, j, c_w_group, c_scale[j], c_w_out)
            h, hb = squared_relu_mlp(h, hb, norm_mlp[i], w_mlp_in, w_mlp_out, i)
            h, hb = ple_update(h, hb, norm_ple[i], w_ple_gate, i,
                               p[i, b].astype(BF16), w_ple[i].astype(BF16))
        outs.append(h)
    return outs[0][None] if bsz == 1 else jnp.stack(outs, axis=0)
```

```python
import functools
import math

import jax
import jax.numpy as jnp
from jax import lax
from jax.experimental import pallas as pl
from jax.experimental.pallas import tpu as pltpu

F32 = jnp.float32
BF16 = jnp.bfloat16

RMS_EPS = 1e-6
ROPE_THETA = 10000.0

A_HEAD_DIM = 128
A_HEADS_PER_GROUP = 16
A_PATTERNS = ((128, 1), (512, 4), (2048, 16))
A_STEPS = 128
A_RESIDUES = 16
A_GROUP_WIDTH = A_HEADS_PER_GROUP * A_HEAD_DIM

B_HEADS = 8
B_QK_DIM = 256
B_V_DIM = 512
B_CHUNK = 256
B_GATE_PAD = 128

C_WINDOWS = (2, 4, 8, 16)
C_GROUP_DIM = 1024
C_HALO = 128

LANES = 128
VMEM_LIMIT_BYTES = 60 * 1024 * 1024


def _params(n_grid_axes):
    return pltpu.CompilerParams(
        dimension_semantics=("arbitrary",) * n_grid_axes,
        vmem_limit_bytes=VMEM_LIMIT_BYTES)


def _rmsnorm_rows(x, gain):
    inv = lax.rsqrt(jnp.mean(x * x, axis=-1, keepdims=True) + RMS_EPS)
    return x * inv * gain


def _row_permutation(tm, rows, to_residue_major):
    a = lax.broadcasted_iota(jnp.int32, (tm, tm), 0)
    b = lax.broadcasted_iota(jnp.int32, (tm, tm), 1)
    major, seq_order = (a, b) if to_residue_major else (b, a)
    return (seq_order == (major % rows) * A_RESIDUES + major // rows).astype(BF16)


def _rmsnorm_residue_major_kernel(x_ref, g_ref, o_ref):
    tm = x_ref.shape[0]
    rows = tm // A_RESIDUES
    y = _rmsnorm_rows(x_ref[...], g_ref[...]).astype(BF16)
    y = jnp.dot(_row_permutation(tm, rows, True), y, preferred_element_type=F32)
    o_ref[...] = y.astype(o_ref.dtype).reshape(o_ref.shape)


def rmsnorm_residue_major(h, gain, tm=256):
    m, d = h.shape
    tm = min(tm, m)
    in_specs = [pl.BlockSpec((tm, d), lambda i: (i, 0)),
                pl.BlockSpec((1, d), lambda i: (0, 0))]
    out = pl.pallas_call(
        _rmsnorm_residue_major_kernel,
        out_shape=jax.ShapeDtypeStruct((A_RESIDUES, m // A_RESIDUES, d), BF16),
        grid=(m // tm,), in_specs=in_specs,
        out_specs=pl.BlockSpec((A_RESIDUES, tm // A_RESIDUES, d), lambda i: (0, i, 0)),
        compiler_params=_params(1), name="rmsnorm_residue_major",
    )(h, gain.reshape(1, d))
    return out.reshape(m, d)


SIDE_CAST_ROWS = 64
NORM_ROWS = 64


def _row_inv_rms(x_ref):
    tm, k = x_ref.shape
    rows = min(NORM_ROWS, tm)
    pieces = []
    for r in range(tm // rows):
        part = None
        for c in range(k // LANES):
            xc = x_ref[r * rows:(r + 1) * rows, c * LANES:(c + 1) * LANES].astype(F32)
            part = xc * xc if part is None else part + xc * xc
        pieces.append(jnp.sum(part, axis=-1, keepdims=True))
    return lax.rsqrt(jnp.concatenate(pieces, axis=0) * (1.0 / k) + RMS_EPS)


def _mm_kernel(x_ref, wchunk_ref, *rest, epilogue, n_extra, n_out, n_slabs, n_side_steps, normed):
    if normed:
        gcol_ref, rest = rest[0], rest[1:]
    extra_refs, rest = rest[:n_extra], rest[n_extra:]
    if n_side_steps:
        side_ref, rest = rest[0], rest[1:]
    o_refs, rest = rest[:n_out], rest[n_out:]
    if n_side_steps:
        side_out_ref, rest = rest[0], rest[1:]
    if normed:
        wb_refs, inv_ref = rest[:2], rest[2]
    else:
        wb_refs = rest
    jj, i = pl.program_id(0), pl.program_id(1)
    chunk = wchunk_ref.shape[0]

    def cast_jobs(dst_ref):
        row0 = pl.multiple_of(i * chunk, chunk)
        wc = wchunk_ref[...]
        if normed:
            wc = wc * gcol_ref[...]
        dst_ref[pl.ds(row0, chunk), :] = wc.astype(BF16)
        if n_side_steps:
            side_out_ref[...] = side_ref[...].astype(BF16)

    @pl.when(jj == 0)
    def _():
        cast_jobs(wb_refs[0])

    if normed:
        tile_lane = lax.broadcasted_iota(jnp.int32, inv_ref.shape, 1) == i

        @pl.when((jj == 0) & (i == 0))
        def _():
            inv_ref[...] = jnp.zeros_like(inv_ref)

        @pl.when(jj == 1)
        def _():
            inv_ref[...] = jnp.where(tile_lane, _row_inv_rms(x_ref), inv_ref[...])

    def multiply(cur_ref, next_ref):
        cast_jobs(next_ref)
        acc = jnp.dot(x_ref[...], cur_ref[...], preferred_element_type=F32)
        if normed:
            acc = acc * jnp.sum(jnp.where(tile_lane, inv_ref[...], 0.0), axis=-1, keepdims=True)
        vals = epilogue(acc, slice(None), *extra_refs)
        vals = vals if isinstance(vals, tuple) else (vals,)
        for o_ref, val in zip(o_refs, vals, strict=True):
            o_ref[...] = val.astype(o_ref.dtype)

    @pl.when(jj % 2 == 1)
    def _():
        multiply(wb_refs[0], wb_refs[1])

    @pl.when((jj > 0) & (jj % 2 == 0))
    def _():
        multiply(wb_refs[1], wb_refs[0])


def fused_matmul(x, w, layer, col0, n, epilogue, *, out_dtype, tm=1024, tn=1024, extras=(), side=None,
                 norm_gain=None, name):
    m, k = x.shape
    tm, tn = min(tm, m), min(tn, n)
    n_slabs, n_rows = n // tn, m // tm
    assert w.shape[1] == k and m % tm == 0 and n % tn == 0 and col0 % tn == 0 and k % n_rows == 0, (
        w.shape, m, n, tm, tn, col0)
    jb = col0 // tn
    chunk = k // n_rows
    out_dtypes = out_dtype if isinstance(out_dtype, tuple) else (out_dtype,)

    def lag(im):
        return lambda jj, i: im(jnp.maximum(jj - 1, 0), jnp.where(jj > 0, i, 0))

    chunk_row = lambda jj, i: jnp.where(jj < n_slabs, i, n_rows - 1)
    in_specs = [pl.BlockSpec((tm, k), lag(lambda j, i: (i, 0))),
                pl.BlockSpec((None, chunk, tn),
                             lambda jj, i: (layer, chunk_row(jj, i), jb + jnp.minimum(jj, n_slabs - 1)))]
    operands = [x, w]
    if norm_gain is not None:
        assert n_rows <= LANES
        in_specs.append(pl.BlockSpec((chunk, 1), lambda jj, i: (chunk_row(jj, i), 0)))
        operands.append(norm_gain.reshape(k, 1))
    in_specs += [pl.BlockSpec(bs, lag(im)) for _, bs, im in extras]
    operands += [a for a, _, _ in extras]
    out_shape = [jax.ShapeDtypeStruct((m, n), dt) for dt in out_dtypes]
    out_specs = [pl.BlockSpec((tm, tn), lag(lambda j, i: (i, j))) for _ in out_dtypes]
    n_side_steps = 0
    if side is not None:
        side_w, side_layer = side
        side_rows, side_cols = side_w.shape[1:]
        rows_per_step = SIDE_CAST_ROWS
        while side_rows // rows_per_step > (n_slabs + 1) * n_rows:
            rows_per_step *= 2
        n_side_steps = side_rows // rows_per_step
        assert side_rows % rows_per_step == 0
        side_block = lambda jj, i: jnp.minimum(jj * n_rows + i, n_side_steps - 1)
        in_specs.append(pl.BlockSpec((None, rows_per_step, side_cols),
                                     lambda jj, i: (side_layer, side_block(jj, i), 0)))
        operands.append(side_w)
        out_shape.append(jax.ShapeDtypeStruct((side_rows, side_cols), BF16))
        out_specs.append(pl.BlockSpec((rows_per_step, side_cols), lambda jj, i: (side_block(jj, i), 0)))
    outs = pl.pallas_call(
        functools.partial(_mm_kernel, epilogue=epilogue, n_extra=len(extras), n_out=len(out_dtypes),
                          n_slabs=n_slabs, n_side_steps=n_side_steps, normed=norm_gain is not None),
        out_shape=out_shape,
        grid=(n_slabs + 1, n_rows),
        in_specs=in_specs,
        out_specs=out_specs,
        scratch_shapes=[pltpu.VMEM((k, tn), BF16), pltpu.VMEM((k, tn), BF16)] + (
            [pltpu.VMEM((tm, LANES), F32)] if norm_gain is not None else []),
        compiler_params=_params(2),
        name=name,
    )(*operands)
    return outs[0] if len(outs) == 1 else tuple(outs)


def _ep_identity(acc, cs):
    return acc


def _ep_relu_sq(acc, cs):
    r = jnp.maximum(acc, 0.0)
    return r * r


def _ep_residual(acc, cs, res_ref):
    h = res_ref[:, cs] + acc
    return h, h


def matmul_residual(x, w, layer, res, *, tm=512, tn=1024, name):
    n = w.shape[2]
    tm, tn = min(tm, x.shape[0]), min(tn, n)
    return fused_matmul(x, w, layer, 0, n, _ep_residual, out_dtype=(F32, BF16), tm=tm, tn=tn,
                        extras=[(res, (tm, tn), lambda j, i: (i, j))], name=name)


def _mm_kgrid_kernel(x_ref, w_ref, res_ref, o_ref, ob_ref, acc_ref):
    kk = pl.program_id(2)

    @pl.when(kk == 0)
    def _():
        acc_ref[...] = jnp.zeros_like(acc_ref)

    acc_ref[...] += jnp.dot(x_ref[...], w_ref[...], preferred_element_type=F32)

    @pl.when(kk == pl.num_programs(2) - 1)
    def _():
        h = res_ref[...] + acc_ref[...]
        o_ref[...] = h
        ob_ref[...] = h.astype(ob_ref.dtype)


def matmul_kgrid_residual(x, w, res, *, tm=1024, tn=1024, tk=2048, name):
    m, k = x.shape
    n = w.shape[1]
    tm, tn, tk = min(tm, m), min(tn, n), min(tk, k)
    assert m % tm == 0 and n % tn == 0 and k % tk == 0
    return pl.pallas_call(
        _mm_kgrid_kernel,
        out_shape=(jax.ShapeDtypeStruct((m, n), F32), jax.ShapeDtypeStruct((m, n), BF16)),
        grid=(m // tm, n // tn, k // tk),
        in_specs=[pl.BlockSpec((tm, tk), lambda i, j, kk: (i, kk)),
                  pl.BlockSpec((tk, tn), lambda i, j, kk: (kk, j)),
                  pl.BlockSpec((tm, tn), lambda i, j, kk: (i, j))],
        out_specs=(pl.BlockSpec((tm, tn), lambda i, j, kk: (i, j)),
                   pl.BlockSpec((tm, tn), lambda i, j, kk: (i, j))),
        scratch_shapes=[pltpu.VMEM((tm, tn), F32)],
        compiler_params=_params(3),
        name=name,
    )(x, w, res)


def _ep_ple(acc, cs, p_ref, wp_ref, res_ref):
    emb = jnp.dot(p_ref[...], wp_ref[:, cs], preferred_element_type=F32)
    h = res_ref[:, cs] + jax.nn.sigmoid(acc) * emb
    return h, h


def ple_update(h, hb, norm_gain, w_gate, layer, p, w_ple, *, tm=512, tn=1024):
    n = w_gate.shape[2]
    tm, tn = min(tm, h.shape[0]), min(tn, n)
    kp = p.shape[1]
    return fused_matmul(
        hb, w_gate, layer, 0, n, _ep_ple, out_dtype=(F32, BF16), tm=tm, tn=tn, norm_gain=norm_gain,
        extras=[(p, (tm, kp), lambda j, i: (i, 0)),
                (w_ple, (kp, tn), lambda j, i: (0, j)),
                (h, (tm, tn), lambda j, i: (i, j))],
        name="ple_gate")


def _rope_table_kernel(cos_ref, sin_ref, *, rows_per_residue):
    tm = cos_ref.shape[0]
    half = A_HEAD_DIM // 2
    row0 = pl.program_id(0) * tm
    residue = row0 // rows_per_residue
    n0 = row0 % rows_per_residue
    n = n0 + lax.broadcasted_iota(jnp.int32, (tm, A_HEAD_DIM), 0)
    pos = (n * A_RESIDUES + residue).astype(F32)
    lane = lax.broadcasted_iota(jnp.int32, (tm, A_HEAD_DIM), 1)
    j = jnp.where(lane >= half, lane - half, lane).astype(F32)
    inv_freq = jnp.exp(j * (-2.0 * math.log(ROPE_THETA) / A_HEAD_DIM))
    ang = pos * inv_freq
    cos_ref[...] = jnp.cos(ang)
    sin_ref[...] = jnp.where(lane >= half, 1.0, -1.0) * jnp.sin(ang)


def rope_tables(seq, tm=256):
    rows_per_residue = seq // A_RESIDUES
    tm = min(tm, rows_per_residue)
    assert rows_per_residue % tm == 0
    spec = pl.BlockSpec((tm, A_HEAD_DIM), lambda i: (i, 0))
    shape = jax.ShapeDtypeStruct((seq, A_HEAD_DIM), F32)
    return pl.pallas_call(
        functools.partial(_rope_table_kernel, rows_per_residue=rows_per_residue),
        out_shape=(shape, shape), grid=(seq // tm,),
        in_specs=[], out_specs=(spec, spec), compiler_params=_params(1),
        name="rope_tables")()


def _ep_qk_norm_rope(acc, cs, gain_ref, cos_ref, sin_ref):
    cos, sin = cos_ref[...], sin_ref[...]
    gain = gain_ref[:, cs]
    d = A_HEAD_DIM
    ones = jnp.ones((d, d), BF16)
    r_i = lax.broadcasted_iota(jnp.int32, (d, d), 0)
    c_i = lax.broadcasted_iota(jnp.int32, (d, d), 1)
    swap_halves = (r_i == (c_i + d // 2) % d).astype(BF16)
    outs = []
    for hh in range(acc.shape[1] // d):
        hs = slice(hh * d, (hh + 1) * d)
        a = acc[:, hs]
        sum_sq = jnp.dot((a * a).astype(BF16), ones, preferred_element_type=F32)
        inv = lax.rsqrt(sum_sq * (1.0 / d) + RMS_EPS)
        t = a * gain[:, hs]
        t_swapped = jnp.dot(t.astype(BF16), swap_halves, preferred_element_type=F32)
        outs.append((t * cos + t_swapped * sin) * inv)
    return jnp.concatenate(outs, axis=1)


def _attn_kernel(q_ref, kc_ref, vc_ref, o_ref, lse_ref, kp_ref, vp_ref, *, parts):
    rows = q_ref.shape[-2]
    qb = parts * rows
    n = pl.program_id(1)

    @pl.when(n == 0)
    def _():
        kp_ref[...] = jnp.zeros_like(kp_ref)
        vp_ref[...] = jnp.zeros_like(vp_ref)

    def load(ref, hs):
        return ref[..., hs].reshape(qb, A_HEAD_DIM)

    def step_of(idx):
        return idx if parts == 1 else (idx % rows) * parts + idx // rows

    lq = step_of(lax.broadcasted_iota(jnp.int32, (qb, 2 * qb), 0))
    col = lax.broadcasted_iota(jnp.int32, (qb, 2 * qb), 1)
    is_prev = col < qb
    lk = step_of(jnp.where(is_prev, col, col - qb))
    dist = lq - lk + jnp.where(is_prev, qb, 0)
    ok = (dist >= 0) & (dist <= A_STEPS) & ((n > 0) | jnp.logical_not(is_prev))
    lane = lax.broadcasted_iota(jnp.int32, (qb, LANES), 1)
    scale = A_HEAD_DIM ** -0.5
    nt = (((1,), (1,)), ((), ()))
    ones = jnp.ones((2 * qb, A_HEAD_DIM), BF16)
    lse_tile = jnp.zeros((qb, LANES), F32)
    for hh in range(A_HEADS_PER_GROUP):
        hs = slice(hh * A_HEAD_DIM, (hh + 1) * A_HEAD_DIM)
        q = load(q_ref, hs)
        k_cat = jnp.concatenate([load(kp_ref, hs), load(kc_ref, hs)], axis=0)
        v_cat = jnp.concatenate([load(vp_ref, hs), load(vc_ref, hs)], axis=0)
        s = lax.dot_general(q, k_cat, nt, preferred_element_type=F32) * scale
        s = jnp.where(ok, s, -jnp.inf)
        mx = jnp.max(jnp.maximum(s[:, :qb], s[:, qb:]), axis=-1, keepdims=True)
        e = jnp.exp(s - mx).astype(BF16)
        pv = jnp.dot(e, jnp.concatenate([v_cat, ones], axis=1), preferred_element_type=F32)
        den = pv[:, A_HEAD_DIM:]
        o_ref[..., hs] = (pv[:, :A_HEAD_DIM] / den).reshape(o_ref.shape[:-1] + (A_HEAD_DIM,)).astype(o_ref.dtype)
        lse_tile = jnp.where(lane == hh, mx + jnp.log(den), lse_tile)
    lse_ref[...] = lse_tile.reshape(lse_ref.shape)
    kp_ref[...] = kc_ref[...]
    vp_ref[...] = vc_ref[...]


def dilated_attention(qk, v, group, dilation):
    seq = qk.shape[0]
    gw = A_GROUP_WIDTH
    n_groups = len(A_PATTERNS)
    parts = A_RESIDUES // dilation
    rpr = seq // A_RESIDUES
    rows = max(A_STEPS // parts, 16)
    nb = rpr // rows
    qk_w, v_w = qk.shape[1], v.shape[1]
    if parts == 1:
        vshape = lambda w: (seq, w)
        blk = lambda w: (rows, w)
        at = lambda r, n, c: (r * nb + n, c)
    else:
        vshape = lambda w: (parts, dilation, rpr, w)
        blk = lambda w: (parts, None, rows, w)
        at = lambda r, n, c: (0, r, n, c)
    view = lambda a: a.reshape(vshape(a.shape[1]))
    o, lse = pl.pallas_call(
        functools.partial(_attn_kernel, parts=parts),
        out_shape=(jax.ShapeDtypeStruct(vshape(gw), BF16),
                   jax.ShapeDtypeStruct(vshape(LANES), F32)),
        grid=(dilation, nb),
        in_specs=[pl.BlockSpec(blk(gw), lambda r, n: at(r, n, group)),
                  pl.BlockSpec(blk(gw), lambda r, n: at(r, n, n_groups + group)),
                  pl.BlockSpec(blk(gw), lambda r, n: at(r, n, group))],
        out_specs=(pl.BlockSpec(blk(gw), lambda r, n: at(r, n, 0)),
                   pl.BlockSpec(blk(LANES), lambda r, n: at(r, n, 0))),
        scratch_shapes=[pltpu.VMEM(tuple(d for d in blk(gw) if d is not None), BF16)] * 2,
        compiler_params=_params(2),
        name=f"dilated_attn_g{group}",
    )(view(qk), view(qk), view(v))
    return o.reshape(seq, gw), lse.reshape(seq, LANES)


def _attn_merge_kernel(o0_ref, o1_ref, o2_ref, l0_ref, l1_ref, l2_ref, out_ref):
    rows = o0_ref.shape[1]
    tm = A_RESIDUES * rows

    def flat(ref, sl=slice(None)):
        return ref[:, :, sl].reshape(tm, -1)

    l0, l1, l2 = flat(l0_ref), flat(l1_ref), flat(l2_ref)
    mx = jnp.maximum(jnp.maximum(l0, l1), l2)
    w0, w1, w2 = jnp.exp(l0 - mx), jnp.exp(l1 - mx), jnp.exp(l2 - mx)
    tot = w0 + w1 + w2
    w0, w1, w2 = w0 / tot, w1 / tot, w2 / tot
    to_sequence_order = _row_permutation(tm, rows, False)
    for hh in range(A_HEADS_PER_GROUP):
        hs = slice(hh * A_HEAD_DIM, (hh + 1) * A_HEAD_DIM)
        merged = (w0[:, hh:hh + 1] * flat(o0_ref, hs).astype(F32)
                  + w1[:, hh:hh + 1] * flat(o1_ref, hs).astype(F32)
                  + w2[:, hh:hh + 1] * flat(o2_ref, hs).astype(F32)).astype(BF16)
        out_ref[:, hs] = jnp.dot(to_sequence_order, merged, preferred_element_type=F32).astype(out_ref.dtype)


def attention_merge(outs, lses, rows=16):
    seq, gw = outs[0].shape
    rpr = seq // A_RESIDUES
    rows = min(rows, rpr)
    tm = A_RESIDUES * rows
    o_spec = pl.BlockSpec((A_RESIDUES, rows, gw), lambda i: (0, i, 0))
    l_spec = pl.BlockSpec((A_RESIDUES, rows, LANES), lambda i: (0, i, 0))
    return pl.pallas_call(
        _attn_merge_kernel,
        out_shape=jax.ShapeDtypeStruct((seq, gw), BF16),
        grid=(rpr // rows,),
        in_specs=[o_spec] * 3 + [l_spec] * 3,
        out_specs=pl.BlockSpec((tm, gw), lambda i: (i, 0)),
        compiler_params=_params(1),
        name="attn_merge",
    )(*[o.reshape(A_RESIDUES, rpr, gw) for o in outs],
      *[l.reshape(A_RESIDUES, rpr, LANES) for l in lses])


def dilated_attention_mixer(h, norm_gain, w_in, layer, q_gain, k_gain, w_out, rope):
    seq = h.shape[0]
    n_heads = len(A_PATTERNS) * A_HEADS_PER_GROUP
    qk_width = 2 * n_heads * A_HEAD_DIM
    v_width = n_heads * A_HEAD_DIM
    assert seq % (A_RESIDUES * A_STEPS) == 0
    assert all(w // d == A_STEPS and A_RESIDUES % d == 0 for w, d in A_PATTERNS)
    xn = rmsnorm_residue_major(h, norm_gain)
    cos, sin = rope
    gain = jnp.concatenate([jnp.tile(q_gain, n_heads), jnp.tile(k_gain, n_heads)]).reshape(1, qk_width)
    tm, tn = min(1024, seq), 1024
    qk = fused_matmul(
        xn, w_in, layer, 0, qk_width, _ep_qk_norm_rope, out_dtype=BF16, tm=tm, tn=tn,
        extras=[(gain, (1, tn), lambda j, i: (0, j)),
                (cos, (tm, A_HEAD_DIM), lambda j, i: (i, 0)),
                (sin, (tm, A_HEAD_DIM), lambda j, i: (i, 0))],
        name="attn_qk_proj")
    v = fused_matmul(xn, w_in, layer, qk_width, v_width, _ep_identity, out_dtype=BF16,
                     tm=tm, tn=tn, name="attn_v_proj")
    outs, lses = [], []
    for g, (_, dilation) in enumerate(A_PATTERNS):
        o, lse = dilated_attention(qk, v, g, dilation)
        outs.append(o)
        lses.append(lse)
    merged = attention_merge(outs, lses)
    return matmul_residual(merged, w_out, layer, h, name="attn_out_proj")


def _ep_gates(acc, cs, bias_ref):
    lane = lax.broadcasted_iota(jnp.int32, acc.shape, 1)
    g = jnp.where(lane < 2 * B_HEADS, acc + bias_ref[...], 0.0)
    return jnp.where(lane >= B_HEADS, jax.nn.log_sigmoid(g), g)


def _mlstm_kernel(q_ref, k_ref, v_ref, og_ref, gcol_ref, grow_ref, gain_ref, o_ref,
                  c_ref, n_ref, m_ref):
    hd = pl.program_id(0)
    L = q_ref.shape[0]
    @pl.when(pl.program_id(1) == 0)
    def _():
        c_ref[...] = jnp.zeros_like(c_ref)
        n_ref[...] = jnp.zeros_like(n_ref)
        m_ref[...] = jnp.zeros_like(m_ref)

    gcol = gcol_ref[...]
    lane = lax.broadcasted_iota(jnp.int32, gcol.shape, 1)
    ig_col = jnp.sum(jnp.where(lane == hd, gcol, 0.0), axis=-1, keepdims=True)
    lf_col = jnp.sum(jnp.where(lane == hd + B_HEADS, gcol, 0.0), axis=-1, keepdims=True)
    ig_row = grow_ref[pl.ds(hd, 1), :]
    lf_row = grow_ref[pl.ds(hd + B_HEADS, 1), :]

    t_i = lax.broadcasted_iota(jnp.int32, (L, L), 0)
    s_i = lax.broadcasted_iota(jnp.int32, (L, L), 1)
    causal = s_i <= t_i
    lower = causal.astype(BF16)
    upper = (t_i <= s_i).astype(BF16)

    def split3(a):
        hi = a.astype(BF16).astype(F32)
        r1 = a - hi
        mid = r1.astype(BF16).astype(F32)
        return hi, mid, (r1 - mid).astype(BF16).astype(F32)

    hi, mid, lo = split3(lf_col)
    parts_col = jnp.where(lane == 0, hi, jnp.where(lane == 1, mid, jnp.where(lane == 2, lo, 0.0))).astype(BF16)
    cum = jnp.dot(lower, parts_col, preferred_element_type=F32)
    b_col = cum[:, 0:1] + cum[:, 1:2] + cum[:, 2:3]
    hi, mid, lo = split3(lf_row)
    sub = lax.broadcasted_iota(jnp.int32, (8, L), 0)
    parts_row = jnp.where(sub == 0, hi, jnp.where(sub == 1, mid, jnp.where(sub == 2, lo, 0.0))).astype(BF16)
    cum = jnp.dot(parts_row, upper, preferred_element_type=F32)
    b_row = cum[0:1, :] + cum[1:2, :] + cum[2:3, :]

    m_prev = m_ref[:1, :1]
    c_prev = c_ref[...]
    n_prev = n_ref[...]

    q = q_ref[...] * jnp.asarray(B_QK_DIM ** -0.5, q_ref.dtype)
    k = k_ref[...]
    v = v_ref[...]

    log_d = jnp.where(causal, b_col - b_row + ig_row, -jnp.inf)
    log_inter = b_col + m_prev
    m_t = jnp.maximum(log_inter, jnp.max(log_d, axis=-1, keepdims=True))
    scores = lax.dot_general(q, k, (((1,), (1,)), ((), ())), preferred_element_type=F32)
    w = scores * jnp.exp(log_d - m_t)
    decay = jnp.exp(log_inter - m_t)
    num = (decay * jnp.dot(q, c_prev.astype(BF16), preferred_element_type=F32)
           + jnp.dot(w.astype(BF16), v, preferred_element_type=F32))
    qn = (decay * jnp.sum(q.astype(F32) * n_prev, axis=-1, keepdims=True)
          + jnp.sum(w, axis=-1, keepdims=True))
    hval = num / jnp.maximum(jnp.abs(qn), jnp.exp(-m_t))

    inv = lax.rsqrt(jnp.mean(hval * hval, axis=-1, keepdims=True) + RMS_EPS)
    o_ref[...] = (hval * inv * gain_ref[...] * jax.nn.sigmoid(og_ref[...].astype(F32))).astype(o_ref.dtype)

    b_last = b_col[L - 1:L, :]
    log_w = b_last - b_col + ig_col
    m_new = jnp.maximum(b_last + m_prev, jnp.max(log_w, axis=0, keepdims=True))
    w_s = jnp.exp(log_w - m_new)
    carry = jnp.exp(b_last + m_prev - m_new)
    k_w = k.astype(F32) * w_s
    c_ref[...] = carry * c_prev + lax.dot_general(
        k_w.astype(BF16), v, (((0,), (0,)), ((), ())), preferred_element_type=F32)
    n_ref[...] = carry * n_prev + jnp.sum(k_w, axis=0, keepdims=True)
    m_ref[...] = jnp.broadcast_to(m_new, m_ref.shape)


def mlstm_mixer(h, hb, norm_gain, w_in, layer, gate_bias, h_gain, w_out):
    seq, d = h.shape
    qk_w = B_HEADS * B_QK_DIM
    v_w = B_HEADS * B_V_DIM
    main_w = 2 * qk_w + v_w + d
    proj = fused_matmul(hb, w_in, layer, 0, main_w, _ep_identity, out_dtype=BF16, norm_gain=norm_gain,
                        name="mlstm_in_proj")
    n_gate = 2 * B_HEADS
    assert w_in.shape[2] == main_w + n_gate
    bias = jnp.pad(gate_bias, (0, B_GATE_PAD - n_gate)).reshape(1, B_GATE_PAD)
    gates = fused_matmul(hb, w_in, layer, main_w, B_GATE_PAD, _ep_gates, out_dtype=F32, tn=B_GATE_PAD,
                         norm_gain=norm_gain, extras=[(bias, (1, B_GATE_PAD), lambda j, i: (0, 0))],
                         name="mlstm_gate_proj")
    gates_t = gates[:, :n_gate].T
    L = min(B_CHUNK, seq)
    nq = qk_w // B_QK_DIM
    nv = v_w // B_V_DIM
    mixed = pl.pallas_call(
        _mlstm_kernel,
        out_shape=jax.ShapeDtypeStruct((seq, v_w), BF16),
        grid=(B_HEADS, seq // L),
        in_specs=[pl.BlockSpec((L, B_QK_DIM), lambda hd, c: (c, hd)),
                  pl.BlockSpec((L, B_QK_DIM), lambda hd, c: (c, nq + hd)),
                  pl.BlockSpec((L, B_V_DIM), lambda hd, c: (c, (2 * qk_w) // B_V_DIM + hd)),
                  pl.BlockSpec((L, B_V_DIM), lambda hd, c: (c, (2 * qk_w) // B_V_DIM + nv + hd)),
                  pl.BlockSpec((L, B_GATE_PAD), lambda hd, c: (c, 0)),
                  pl.BlockSpec((n_gate, L), lambda hd, c: (0, c)),
                  pl.BlockSpec((1, B_V_DIM), lambda hd, c: (0, hd))],
        out_specs=pl.BlockSpec((L, B_V_DIM), lambda hd, c: (c, hd)),
        scratch_shapes=[pltpu.VMEM((B_QK_DIM, B_V_DIM), F32),
                        pltpu.VMEM((1, B_QK_DIM), F32),
                        pltpu.VMEM((8, LANES), F32)],
        compiler_params=_params(2),
        name="mlstm_chunks",
    )(proj, proj, proj, proj, gates, gates_t, h_gain.reshape(1, v_w))
    return matmul_residual(mixed, w_out, layer, h, name="mlstm_out_proj")


def _pool_kernel(u_ref, halo_ref, wg_ref, scale_ref, o_ref, wb_ref):
    g = pl.program_id(0)
    i = pl.program_id(1)
    tm = u_ref.shape[0]

    @pl.when(i == 0)
    def _():
        wb_ref[...] = wg_ref[...].astype(BF16)

    win = jnp.left_shift(jnp.int32(C_WINDOWS[0]), g)
    t_i = lax.broadcasted_iota(jnp.int32, (tm, tm), 0)
    s_i = lax.broadcasted_iota(jnp.int32, (tm, tm), 1)
    dist = t_i - s_i
    band = ((dist >= 0) & (dist < win)).astype(BF16)
    t_h = lax.broadcasted_iota(jnp.int32, (tm, C_HALO), 0)
    s_h = lax.broadcasted_iota(jnp.int32, (tm, C_HALO), 1)
    band_halo = ((t_h + C_HALO - s_h < win) & (i > 0)).astype(BF16)
    u = u_ref[...]
    total = (jnp.dot(band, u, preferred_element_type=F32)
             + jnp.dot(band_halo, halo_ref[...], preferred_element_type=F32))
    t_glob = i * tm + lax.broadcasted_iota(jnp.int32, (tm, 1), 0)
    count = jnp.minimum(t_glob + 1, win).astype(F32)
    pooled = total / count - u.astype(F32)
    y = jnp.dot(pooled.astype(BF16), wb_ref[...], preferred_element_type=F32) * scale_ref[...]
    o_ref[...] = y.astype(o_ref.dtype)


def pooling_mixer(h, hb, norm_gain, w_in, layer, w_group, scale, w_out, tm=512):
    seq, d = h.shape
    assert all(C_WINDOWS[g] == C_WINDOWS[0] << g for g in range(len(C_WINDOWS)))
    assert max(C_WINDOWS) <= C_HALO
    u = fused_matmul(hb, w_in, layer, 0, d, _ep_identity, out_dtype=BF16, norm_gain=norm_gain,
                     name="pool_in_proj")
    tm = min(tm, seq)
    gd = C_GROUP_DIM
    halo_per_tile = tm // C_HALO
    y = pl.pallas_call(
        _pool_kernel,
        out_shape=jax.ShapeDtypeStruct((seq, d), BF16),
        grid=(len(C_WINDOWS), seq // tm),
        in_specs=[pl.BlockSpec((tm, gd), lambda g, i: (i, g)),
                  pl.BlockSpec((C_HALO, gd), lambda g, i: (jnp.maximum(i * halo_per_tile - 1, 0), g)),
                  pl.BlockSpec((None, None, gd, gd), lambda g, i: (layer, g, 0, 0)),
                  pl.BlockSpec((1, gd), lambda g, i: (0, g))],
        out_specs=pl.BlockSpec((tm, gd), lambda g, i: (i, g)),
        scratch_shapes=[pltpu.VMEM((gd, gd), BF16)],
        compiler_params=_params(2),
        name="pool_group",
    )(u, u, w_group, scale.reshape(1, d))
    return matmul_residual(y, w_out, layer, h, name="pool_out_proj")


def squared_relu_mlp(h, hb, norm_gain, w_in, w_out, layer):
    hidden, w_out_bf16 = fused_matmul(hb, w_in, layer, 0, w_in.shape[2], _ep_relu_sq, out_dtype=BF16,
                                      norm_gain=norm_gain, side=(w_out, layer), name="mlp_up")
    return matmul_kgrid_residual(hidden, w_out_bf16, h, name="mlp_down")


def kernel(x, p, norm_mix, norm_mlp, norm_ple, w_ple, w_ple_gate, w_mlp_in, w_mlp_out, a_w_in, a_q_norm, a_k_norm, a_w_out, b_w_in, b_gate_bias, b_h_norm, b_w_out, c_w_in, c_w_group, c_scale, c_w_out):
    bsz, seq, d = x.shape
    depth = p.shape[0]
    rope = rope_tables(seq)
    outs = []
    for b in range(bsz):
        h = x.reshape(seq, d) if bsz == 1 else x[b]
        hb = None
        for i in range(depth):
            kind, j = i % 3, i // 3
            if kind == 0:
                h, hb = dilated_attention_mixer(h, norm_mix[i], a_w_in, j, a_q_norm[j], a_k_norm[j], a_w_out, rope)
            elif kind == 1:
                h, hb = mlstm_mixer(h, hb, norm_mix[i], b_w_in, j, b_gate_bias[j], b_h_norm[j], b_w_out)
            else:
                h, hb = pooling_mixer(h, hb, norm_mix[i], c_w_in, j, c_w_group, c_scale[j], c_w_out)
            h, hb = squared_relu_mlp(h, hb, norm_mlp[i], w_mlp_in, w_mlp_out, i)
            h, hb = ple_update(h, hb, norm_ple[i], w_ple_gate, i,
                               p[i, b].astype(BF16), w_ple[i].astype(BF16))
        outs.append(h)
    return outs[0][None] if bsz == 1 else jnp.stack(outs, axis=0)
```

```python
import functools
import math

import jax
import jax.numpy as jnp
from jax import lax
from jax.experimental import pallas as pl
from jax.experimental.pallas import tpu as pltpu

F32 = jnp.float32
BF16 = jnp.bfloat16

RMS_EPS = 1e-6
ROPE_THETA = 10000.0

A_HEAD_DIM = 128
A_HEADS_PER_GROUP = 16
A_PATTERNS = ((128, 1), (512, 4), (2048, 16))
A_STEPS = 128
A_RESIDUES = 16
A_GROUP_WIDTH = A_HEADS_PER_GROUP * A_HEAD_DIM

B_HEADS = 8
B_QK_DIM = 256
B_V_DIM = 512
B_CHUNK = 256
B_HEADS_PER_STEP = 2
B_GATE_PAD = 128

C_WINDOWS = (2, 4, 8, 16)
C_GROUP_DIM = 1024
C_HALO = 128

LANES = 128
VMEM_LIMIT_BYTES = 60 * 1024 * 1024


def _params(n_grid_axes):
    return pltpu.CompilerParams(
        dimension_semantics=("arbitrary",) * n_grid_axes,
        vmem_limit_bytes=VMEM_LIMIT_BYTES)


def _rmsnorm_rows(x, gain):
    inv = lax.rsqrt(jnp.mean(x * x, axis=-1, keepdims=True) + RMS_EPS)
    return x * inv * gain


def _row_permutation(tm, rows, to_residue_major):
    a = lax.broadcasted_iota(jnp.int32, (tm, tm), 0)
    b = lax.broadcasted_iota(jnp.int32, (tm, tm), 1)
    major, seq_order = (a, b) if to_residue_major else (b, a)
    return (seq_order == (major % rows) * A_RESIDUES + major // rows).astype(BF16)


def _rmsnorm_residue_major_kernel(x_ref, g_ref, o_ref):
    tm = x_ref.shape[0]
    rows = tm // A_RESIDUES
    y = _rmsnorm_rows(x_ref[...], g_ref[...]).astype(BF16)
    y = jnp.dot(_row_permutation(tm, rows, True), y, preferred_element_type=F32)
    o_ref[...] = y.astype(o_ref.dtype).reshape(o_ref.shape)


def rmsnorm_residue_major(h, gain, tm=256):
    m, d = h.shape
    tm = min(tm, m)
    in_specs = [pl.BlockSpec((tm, d), lambda i: (i, 0)),
                pl.BlockSpec((1, d), lambda i: (0, 0))]
    out = pl.pallas_call(
        _rmsnorm_residue_major_kernel,
        out_shape=jax.ShapeDtypeStruct((A_RESIDUES, m // A_RESIDUES, d), BF16),
        grid=(m // tm,), in_specs=in_specs,
        out_specs=pl.BlockSpec((A_RESIDUES, tm // A_RESIDUES, d), lambda i: (0, i, 0)),
        compiler_params=_params(1), name="rmsnorm_residue_major",
    )(h, gain.reshape(1, d))
    return out.reshape(m, d)


SIDE_CAST_ROWS = 64
NORM_ROWS = 64


def _row_inv_rms(x_ref):
    tm, k = x_ref.shape
    rows = min(NORM_ROWS, tm)
    pieces = []
    for r in range(tm // rows):
        part = None
        for c in range(k // LANES):
            xc = x_ref[r * rows:(r + 1) * rows, c * LANES:(c + 1) * LANES].astype(F32)
            part = xc * xc if part is None else part + xc * xc
        pieces.append(jnp.sum(part, axis=-1, keepdims=True))
    return lax.rsqrt(jnp.concatenate(pieces, axis=0) * (1.0 / k) + RMS_EPS)


def _mm_kernel(x_ref, wchunk_ref, *rest, epilogue, n_extra, n_out, n_slabs, n_side_steps, normed):
    if normed:
        gcol_ref, rest = rest[0], rest[1:]
    extra_refs, rest = rest[:n_extra], rest[n_extra:]
    if n_side_steps:
        side_ref, rest = rest[0], rest[1:]
    o_refs, rest = rest[:n_out], rest[n_out:]
    if n_side_steps:
        side_out_ref, rest = rest[0], rest[1:]
    if normed:
        wb_refs, inv_ref = rest[:2], rest[2]
    else:
        wb_refs = rest
    jj, i = pl.program_id(0), pl.program_id(1)
    chunk = wchunk_ref.shape[0]

    def cast_jobs(dst_ref):
        row0 = pl.multiple_of(i * chunk, chunk)
        wc = wchunk_ref[...]
        if normed:
            wc = wc * gcol_ref[...]
        dst_ref[pl.ds(row0, chunk), :] = wc.astype(BF16)
        if n_side_steps:
            side_out_ref[...] = side_ref[...].astype(BF16)

    @pl.when(jj == 0)
    def _():
        cast_jobs(wb_refs[0])

    if normed:
        tile_lane = lax.broadcasted_iota(jnp.int32, inv_ref.shape, 1) == i

        @pl.when((jj == 0) & (i == 0))
        def _():
            inv_ref[...] = jnp.zeros_like(inv_ref)

        @pl.when(jj == 1)
        def _():
            inv_ref[...] = jnp.where(tile_lane, _row_inv_rms(x_ref), inv_ref[...])

    def multiply(cur_ref, next_ref):
        cast_jobs(next_ref)
        acc = jnp.dot(x_ref[...], cur_ref[...], preferred_element_type=F32)
        if normed:
            acc = acc * jnp.sum(jnp.where(tile_lane, inv_ref[...], 0.0), axis=-1, keepdims=True)
        vals = epilogue(acc, slice(None), *extra_refs)
        vals = vals if isinstance(vals, tuple) else (vals,)
        for o_ref, val in zip(o_refs, vals, strict=True):
            o_ref[...] = val.astype(o_ref.dtype)

    @pl.when(jj % 2 == 1)
    def _():
        multiply(wb_refs[0], wb_refs[1])

    @pl.when((jj > 0) & (jj % 2 == 0))
    def _():
        multiply(wb_refs[1], wb_refs[0])


def fused_matmul(x, w, layer, col0, n, epilogue, *, out_dtype, tm=1024, tn=1024, extras=(), side=None,
                 norm_gain=None, name):
    m, k = x.shape
    tm, tn = min(tm, m), min(tn, n)
    n_slabs, n_rows = n // tn, m // tm
    assert w.shape[1] == k and m % tm == 0 and n % tn == 0 and col0 % tn == 0 and k % n_rows == 0, (
        w.shape, m, n, tm, tn, col0)
    jb = col0 // tn
    chunk = k // n_rows
    out_dtypes = out_dtype if isinstance(out_dtype, tuple) else (out_dtype,)

    def lag(im):
        return lambda jj, i: im(jnp.maximum(jj - 1, 0), jnp.where(jj > 0, i, 0))

    chunk_row = lambda jj, i: jnp.where(jj < n_slabs, i, n_rows - 1)
    in_specs = [pl.BlockSpec((tm, k), lag(lambda j, i: (i, 0))),
                pl.BlockSpec((None, chunk, tn),
                             lambda jj, i: (layer, chunk_row(jj, i), jb + jnp.minimum(jj, n_slabs - 1)))]
    operands = [x, w]
    if norm_gain is not None:
        assert n_rows <= LANES
        in_specs.append(pl.BlockSpec((chunk, 1), lambda jj, i: (chunk_row(jj, i), 0)))
        operands.append(norm_gain.reshape(k, 1))
    in_specs += [pl.BlockSpec(bs, lag(im)) for _, bs, im in extras]
    operands += [a for a, _, _ in extras]
    out_shape = [jax.ShapeDtypeStruct((m, n), dt) for dt in out_dtypes]
    out_specs = [pl.BlockSpec((tm, tn), lag(lambda j, i: (i, j))) for _ in out_dtypes]
    n_side_steps = 0
    if side is not None:
        side_w, side_layer = side
        side_rows, side_cols = side_w.shape[1:]
        rows_per_step = SIDE_CAST_ROWS
        while side_rows // rows_per_step > (n_slabs + 1) * n_rows:
            rows_per_step *= 2
        n_side_steps = side_rows // rows_per_step
        assert side_rows % rows_per_step == 0
        side_block = lambda jj, i: jnp.minimum(jj * n_rows + i, n_side_steps - 1)
        in_specs.append(pl.BlockSpec((None, rows_per_step, side_cols),
                                     lambda jj, i: (side_layer, side_block(jj, i), 0)))
        operands.append(side_w)
        out_shape.append(jax.ShapeDtypeStruct((side_rows, side_cols), BF16))
        out_specs.append(pl.BlockSpec((rows_per_step, side_cols), lambda jj, i: (side_block(jj, i), 0)))
    outs = pl.pallas_call(
        functools.partial(_mm_kernel, epilogue=epilogue, n_extra=len(extras), n_out=len(out_dtypes),
                          n_slabs=n_slabs, n_side_steps=n_side_steps, normed=norm_gain is not None),
        out_shape=out_shape,
        grid=(n_slabs + 1, n_rows),
        in_specs=in_specs,
        out_specs=out_specs,
        scratch_shapes=[pltpu.VMEM((k, tn), BF16), pltpu.VMEM((k, tn), BF16)] + (
            [pltpu.VMEM((tm, LANES), F32)] if norm_gain is not None else []),
        compiler_params=_params(2),
        name=name,
    )(*operands)
    return outs[0] if len(outs) == 1 else tuple(outs)


def _ep_identity(acc, cs):
    return acc


def _ep_relu_sq(acc, cs):
    r = jnp.maximum(acc, 0.0)
    return r * r


def _ep_residual(acc, cs, res_ref):
    h = res_ref[:, cs] + acc
    return h, h


def matmul_residual(x, w, layer, res, *, tm=512, tn=1024, name):
    n = w.shape[2]
    tm, tn = min(tm, x.shape[0]), min(tn, n)
    return fused_matmul(x, w, layer, 0, n, _ep_residual, out_dtype=(F32, BF16), tm=tm, tn=tn,
                        extras=[(res, (tm, tn), lambda j, i: (i, j))], name=name)


def _mm_kgrid_kernel(x_ref, w_ref, res_ref, o_ref, ob_ref, acc_ref):
    kk = pl.program_id(2)

    @pl.when(kk == 0)
    def _():
        acc_ref[...] = jnp.zeros_like(acc_ref)

    acc_ref[...] += jnp.dot(x_ref[...], w_ref[...], preferred_element_type=F32)

    @pl.when(kk == pl.num_programs(2) - 1)
    def _():
        h = res_ref[...] + acc_ref[...]
        o_ref[...] = h
        ob_ref[...] = h.astype(ob_ref.dtype)


def matmul_kgrid_residual(x, w, res, *, tm=1024, tn=1024, tk=2048, name):
    m, k = x.shape
    n = w.shape[1]
    tm, tn, tk = min(tm, m), min(tn, n), min(tk, k)
    assert m % tm == 0 and n % tn == 0 and k % tk == 0
    return pl.pallas_call(
        _mm_kgrid_kernel,
        out_shape=(jax.ShapeDtypeStruct((m, n), F32), jax.ShapeDtypeStruct((m, n), BF16)),
        grid=(m // tm, n // tn, k // tk),
        in_specs=[pl.BlockSpec((tm, tk), lambda i, j, kk: (i, kk)),
                  pl.BlockSpec((tk, tn), lambda i, j, kk: (kk, j)),
                  pl.BlockSpec((tm, tn), lambda i, j, kk: (i, j))],
        out_specs=(pl.BlockSpec((tm, tn), lambda i, j, kk: (i, j)),
                   pl.BlockSpec((tm, tn), lambda i, j, kk: (i, j))),
        scratch_shapes=[pltpu.VMEM((tm, tn), F32)],
        compiler_params=_params(3),
        name=name,
    )(x, w, res)


def _ep_ple(acc, cs, p_ref, wp_ref, res_ref):
    emb = jnp.dot(p_ref[...], wp_ref[:, cs], preferred_element_type=F32)
    h = res_ref[:, cs] + jax.nn.sigmoid(acc) * emb
    return h, h


def ple_update(h, hb, norm_gain, w_gate, layer, p, w_ple, *, tm=512, tn=1024):
    n = w_gate.shape[2]
    tm, tn = min(tm, h.shape[0]), min(tn, n)
    kp = p.shape[1]
    return fused_matmul(
        hb, w_gate, layer, 0, n, _ep_ple, out_dtype=(F32, BF16), tm=tm, tn=tn, norm_gain=norm_gain,
        extras=[(p, (tm, kp), lambda j, i: (i, 0)),
                (w_ple, (kp, tn), lambda j, i: (0, j)),
                (h, (tm, tn), lambda j, i: (i, j))],
        name="ple_gate")


def _rope_table_kernel(cos_ref, sin_ref, *, rows_per_residue):
    tm = cos_ref.shape[0]
    half = A_HEAD_DIM // 2
    row0 = pl.program_id(0) * tm
    residue = row0 // rows_per_residue
    n0 = row0 % rows_per_residue
    n = n0 + lax.broadcasted_iota(jnp.int32, (tm, A_HEAD_DIM), 0)
    pos = (n * A_RESIDUES + residue).astype(F32)
    lane = lax.broadcasted_iota(jnp.int32, (tm, A_HEAD_DIM), 1)
    j = jnp.where(lane >= half, lane - half, lane).astype(F32)
    inv_freq = jnp.exp(j * (-2.0 * math.log(ROPE_THETA) / A_HEAD_DIM))
    ang = pos * inv_freq
    cos_ref[...] = jnp.cos(ang)
    sin_ref[...] = jnp.where(lane >= half, 1.0, -1.0) * jnp.sin(ang)


def rope_tables(seq, tm=256):
    rows_per_residue = seq // A_RESIDUES
    tm = min(tm, rows_per_residue)
    assert rows_per_residue % tm == 0
    spec = pl.BlockSpec((tm, A_HEAD_DIM), lambda i: (i, 0))
    shape = jax.ShapeDtypeStruct((seq, A_HEAD_DIM), F32)
    return pl.pallas_call(
        functools.partial(_rope_table_kernel, rows_per_residue=rows_per_residue),
        out_shape=(shape, shape), grid=(seq // tm,),
        in_specs=[], out_specs=(spec, spec), compiler_params=_params(1),
        name="rope_tables")()


def _ep_qk_norm_rope(acc, cs, gain_ref, gain_swapped_ref, cos_ref, sin_ref):
    d = A_HEAD_DIM
    cos_g = cos_ref[...] * gain_ref[:, :d]
    sin_g = sin_ref[...] * gain_swapped_ref[:, :d]
    mean_mat = jnp.full((d, d), 1.0 / d, BF16)
    r_i = lax.broadcasted_iota(jnp.int32, (d, d), 0)
    c_i = lax.broadcasted_iota(jnp.int32, (d, d), 1)
    swap_halves = (r_i == (c_i + d // 2) % d).astype(BF16)
    outs = []
    for hh in range(acc.shape[1] // d):
        a = acc[:, hh * d:(hh + 1) * d]
        mean_sq = jnp.dot((a * a).astype(BF16), mean_mat, preferred_element_type=F32)
        a_swapped = jnp.dot(a.astype(BF16), swap_halves, preferred_element_type=F32)
        outs.append((a * cos_g + a_swapped * sin_g) * lax.rsqrt(mean_sq + RMS_EPS))
    return jnp.concatenate(outs, axis=1)


def _attn_kernel(q_ref, kc_ref, vc_ref, o_ref, lse_ref, kp_ref, vp_ref, *, parts):
    rows = q_ref.shape[-2]
    qb = parts * rows
    n = pl.program_id(1)

    @pl.when(n == 0)
    def _():
        kp_ref[...] = jnp.zeros_like(kp_ref)
        vp_ref[...] = jnp.zeros_like(vp_ref)

    def load(ref, hs):
        return ref[..., hs].reshape(qb, A_HEAD_DIM)

    def step_of(idx):
        return idx if parts == 1 else (idx % rows) * parts + idx // rows

    lq = step_of(lax.broadcasted_iota(jnp.int32, (qb, 2 * qb), 0))
    col = lax.broadcasted_iota(jnp.int32, (qb, 2 * qb), 1)
    is_prev = col < qb
    lk = step_of(jnp.where(is_prev, col, col - qb))
    dist = lq - lk + jnp.where(is_prev, qb, 0)
    ok = (dist >= 0) & (dist <= A_STEPS) & ((n > 0) | jnp.logical_not(is_prev))
    lane = lax.broadcasted_iota(jnp.int32, (qb, LANES), 1)
    scale = A_HEAD_DIM ** -0.5
    nt = (((1,), (1,)), ((), ()))
    ones = jnp.ones((2 * qb, A_HEAD_DIM), BF16)
    lse_tile = jnp.zeros((qb, LANES), F32)
    for hh in range(A_HEADS_PER_GROUP):
        hs = slice(hh * A_HEAD_DIM, (hh + 1) * A_HEAD_DIM)
        q = load(q_ref, hs)
        k_cat = jnp.concatenate([load(kp_ref, hs), load(kc_ref, hs)], axis=0)
        v_cat = jnp.concatenate([load(vp_ref, hs), load(vc_ref, hs)], axis=0)
        s = lax.dot_general(q, k_cat, nt, preferred_element_type=F32) * scale
        s = jnp.where(ok, s, -jnp.inf)
        mx = jnp.max(jnp.maximum(s[:, :qb], s[:, qb:]), axis=-1, keepdims=True)
        e = jnp.exp(s - mx).astype(BF16)
        pv = jnp.dot(e, jnp.concatenate([v_cat, ones], axis=1), preferred_element_type=F32)
        den = pv[:, A_HEAD_DIM:]
        o_ref[..., hs] = (pv[:, :A_HEAD_DIM] / den).reshape(o_ref.shape[:-1] + (A_HEAD_DIM,)).astype(o_ref.dtype)
        lse_tile = jnp.where(lane == hh, mx + jnp.log(den), lse_tile)
    lse_ref[...] = lse_tile.reshape(lse_ref.shape)
    kp_ref[...] = kc_ref[...]
    vp_ref[...] = vc_ref[...]


def dilated_attention(qk, v, group, dilation):
    seq = qk.shape[0]
    gw = A_GROUP_WIDTH
    n_groups = len(A_PATTERNS)
    parts = A_RESIDUES // dilation
    rpr = seq // A_RESIDUES
    rows = max(A_STEPS // parts, 16)
    nb = rpr // rows
    qk_w, v_w = qk.shape[1], v.shape[1]
    if parts == 1:
        vshape = lambda w: (seq, w)
        blk = lambda w: (rows, w)
        at = lambda r, n, c: (r * nb + n, c)
    else:
        vshape = lambda w: (parts, dilation, rpr, w)
        blk = lambda w: (parts, None, rows, w)
        at = lambda r, n, c: (0, r, n, c)
    view = lambda a: a.reshape(vshape(a.shape[1]))
    o, lse = pl.pallas_call(
        functools.partial(_attn_kernel, parts=parts),
        out_shape=(jax.ShapeDtypeStruct(vshape(gw), BF16),
                   jax.ShapeDtypeStruct(vshape(LANES), F32)),
        grid=(dilation, nb),
        in_specs=[pl.BlockSpec(blk(gw), lambda r, n: at(r, n, group)),
                  pl.BlockSpec(blk(gw), lambda r, n: at(r, n, n_groups + group)),
                  pl.BlockSpec(blk(gw), lambda r, n: at(r, n, group))],
        out_specs=(pl.BlockSpec(blk(gw), lambda r, n: at(r, n, 0)),
                   pl.BlockSpec(blk(LANES), lambda r, n: at(r, n, 0))),
        scratch_shapes=[pltpu.VMEM(tuple(d for d in blk(gw) if d is not None), BF16)] * 2,
        compiler_params=_params(2),
        name=f"dilated_attn_g{group}",
    )(view(qk), view(qk), view(v))
    return o.reshape(seq, gw), lse.reshape(seq, LANES)


def _attn_merge_kernel(o0_ref, o1_ref, o2_ref, l0_ref, l1_ref, l2_ref, out_ref):
    rows = o0_ref.shape[1]
    tm = A_RESIDUES * rows

    def flat(ref, sl=slice(None)):
        return ref[:, :, sl].reshape(tm, -1)

    l0, l1, l2 = flat(l0_ref), flat(l1_ref), flat(l2_ref)
    mx = jnp.maximum(jnp.maximum(l0, l1), l2)
    w0, w1, w2 = jnp.exp(l0 - mx), jnp.exp(l1 - mx), jnp.exp(l2 - mx)
    tot = w0 + w1 + w2
    w0, w1, w2 = w0 / tot, w1 / tot, w2 / tot
    to_sequence_order = _row_permutation(tm, rows, False)
    for hh in range(A_HEADS_PER_GROUP):
        hs = slice(hh * A_HEAD_DIM, (hh + 1) * A_HEAD_DIM)
        merged = (w0[:, hh:hh + 1] * flat(o0_ref, hs).astype(F32)
                  + w1[:, hh:hh + 1] * flat(o1_ref, hs).astype(F32)
                  + w2[:, hh:hh + 1] * flat(o2_ref, hs).astype(F32)).astype(BF16)
        out_ref[:, hs] = jnp.dot(to_sequence_order, merged, preferred_element_type=F32).astype(out_ref.dtype)


def attention_merge(outs, lses, rows=16):
    seq, gw = outs[0].shape
    rpr = seq // A_RESIDUES
    rows = min(rows, rpr)
    tm = A_RESIDUES * rows
    o_spec = pl.BlockSpec((A_RESIDUES, rows, gw), lambda i: (0, i, 0))
    l_spec = pl.BlockSpec((A_RESIDUES, rows, LANES), lambda i: (0, i, 0))
    return pl.pallas_call(
        _attn_merge_kernel,
        out_shape=jax.ShapeDtypeStruct((seq, gw), BF16),
        grid=(rpr // rows,),
        in_specs=[o_spec] * 3 + [l_spec] * 3,
        out_specs=pl.BlockSpec((tm, gw), lambda i: (i, 0)),
        compiler_params=_params(1),
        name="attn_merge",
    )(*[o.reshape(A_RESIDUES, rpr, gw) for o in outs],
      *[l.reshape(A_RESIDUES, rpr, LANES) for l in lses])


def dilated_attention_mixer(h, norm_gain, w_in, layer, q_gain, k_gain, w_out, rope):
    seq = h.shape[0]
    n_heads = len(A_PATTERNS) * A_HEADS_PER_GROUP
    qk_width = 2 * n_heads * A_HEAD_DIM
    v_width = n_heads * A_HEAD_DIM
    assert seq % (A_RESIDUES * A_STEPS) == 0
    assert all(w // d == A_STEPS and A_RESIDUES % d == 0 for w, d in A_PATTERNS)
    xn = rmsnorm_residue_major(h, norm_gain)
    cos, sin = rope
    per_head = lambda g_q, g_k: jnp.concatenate([jnp.tile(g_q, n_heads), jnp.tile(g_k, n_heads)]).reshape(1, qk_width)
    gain = per_head(q_gain, k_gain)
    gain_swapped = per_head(jnp.roll(q_gain, A_HEAD_DIM // 2), jnp.roll(k_gain, A_HEAD_DIM // 2))
    tm, tn = min(1024, seq), 1024
    assert (qk_width // 2) % tn == 0
    qk = fused_matmul(
        xn, w_in, layer, 0, qk_width, _ep_qk_norm_rope, out_dtype=BF16, tm=tm, tn=tn,
        extras=[(gain, (1, tn), lambda j, i: (0, j)),
                (gain_swapped, (1, tn), lambda j, i: (0, j)),
                (cos, (tm, A_HEAD_DIM), lambda j, i: (i, 0)),
                (sin, (tm, A_HEAD_DIM), lambda j, i: (i, 0))],
        name="attn_qk_proj")
    v = fused_matmul(xn, w_in, layer, qk_width, v_width, _ep_identity, out_dtype=BF16,
                     tm=tm, tn=tn, name="attn_v_proj")
    outs, lses = [], []
    for g, (_, dilation) in enumerate(A_PATTERNS):
        o, lse = dilated_attention(qk, v, g, dilation)
        outs.append(o)
        lses.append(lse)
    merged = attention_merge(outs, lses)
    return matmul_residual(merged, w_out, layer, h, name="attn_out_proj")


def _ep_gates(acc, cs, bias_ref):
    lane = lax.broadcasted_iota(jnp.int32, acc.shape, 1)
    g = jnp.where(lane < 2 * B_HEADS, acc + bias_ref[...], 0.0)
    return jnp.where(lane >= B_HEADS, jax.nn.log_sigmoid(g), g)


def _mlstm_kernel(q_ref, k_ref, v_ref, og_ref, gcol_ref, grow_ref, gain_ref, o_ref,
                  c_ref, n_ref, m_ref):
    @pl.when(pl.program_id(1) == 0)
    def _():
        c_ref[...] = jnp.zeros_like(c_ref)
        n_ref[...] = jnp.zeros_like(n_ref)
        m_ref[...] = jnp.zeros_like(m_ref)

    for hh in range(B_HEADS_PER_STEP):
        qs = slice(hh * B_QK_DIM, (hh + 1) * B_QK_DIM)
        vs = slice(hh * B_V_DIM, (hh + 1) * B_V_DIM)
        _mlstm_head(pl.program_id(0) * B_HEADS_PER_STEP + hh,
                    q_ref.at[:, qs], k_ref.at[:, qs], v_ref.at[:, vs], og_ref.at[:, vs], gcol_ref, grow_ref,
                    gain_ref.at[:, vs], o_ref.at[:, vs], c_ref.at[hh], n_ref.at[hh], m_ref.at[hh])


def _mlstm_head(hd, q_ref, k_ref, v_ref, og_ref, gcol_ref, grow_ref, gain_ref, o_ref, c_ref, n_ref, m_ref):
    L = q_ref.shape[0]
    gcol = gcol_ref[...]
    lane = lax.broadcasted_iota(jnp.int32, gcol.shape, 1)
    ig_col = jnp.sum(jnp.where(lane == hd, gcol, 0.0), axis=-1, keepdims=True)
    lf_col = jnp.sum(jnp.where(lane == hd + B_HEADS, gcol, 0.0), axis=-1, keepdims=True)
    ig_row = grow_ref[pl.ds(hd, 1), :]
    lf_row = grow_ref[pl.ds(hd + B_HEADS, 1), :]

    t_i = lax.broadcasted_iota(jnp.int32, (L, L), 0)
    s_i = lax.broadcasted_iota(jnp.int32, (L, L), 1)
    causal = s_i <= t_i
    lower = causal.astype(BF16)
    upper = (t_i <= s_i).astype(BF16)

    def split3(a):
        hi = a.astype(BF16).astype(F32)
        r1 = a - hi
        mid = r1.astype(BF16).astype(F32)
        return hi, mid, (r1 - mid).astype(BF16).astype(F32)

    hi, mid, lo = split3(lf_col)
    parts_col = jnp.where(lane == 0, hi, jnp.where(lane == 1, mid, jnp.where(lane == 2, lo, 0.0))).astype(BF16)
    cum = jnp.dot(lower, parts_col, preferred_element_type=F32)
    b_col = cum[:, 0:1] + cum[:, 1:2] + cum[:, 2:3]
    hi, mid, lo = split3(lf_row)
    sub = lax.broadcasted_iota(jnp.int32, (8, L), 0)
    parts_row = jnp.where(sub == 0, hi, jnp.where(sub == 1, mid, jnp.where(sub == 2, lo, 0.0))).astype(BF16)
    cum = jnp.dot(parts_row, upper, preferred_element_type=F32)
    b_row = cum[0:1, :] + cum[1:2, :] + cum[2:3, :]

    m_prev = m_ref[:1, :1]
    c_prev = c_ref[...]
    n_prev = n_ref[...]

    q = q_ref[...] * jnp.asarray(B_QK_DIM ** -0.5, q_ref.dtype)
    k = k_ref[...]
    v = v_ref[...]

    log_d = jnp.where(causal, b_col - b_row + ig_row, -jnp.inf)
    log_inter = b_col + m_prev
    m_t = jnp.maximum(log_inter, jnp.max(log_d, axis=-1, keepdims=True))
    scores = lax.dot_general(q, k, (((1,), (1,)), ((), ())), preferred_element_type=F32)
    w = scores * jnp.exp(log_d - m_t)
    decay = jnp.exp(log_inter - m_t)
    num = (decay * jnp.dot(q, c_prev.astype(BF16), preferred_element_type=F32)
           + jnp.dot(w.astype(BF16), v, preferred_element_type=F32))
    qn = (decay * jnp.sum(q.astype(F32) * n_prev, axis=-1, keepdims=True)
          + jnp.sum(w, axis=-1, keepdims=True))
    hval = num / jnp.maximum(jnp.abs(qn), jnp.exp(-m_t))

    inv = lax.rsqrt(jnp.mean(hval * hval, axis=-1, keepdims=True) + RMS_EPS)
    o_ref[...] = (hval * inv * gain_ref[...] * jax.nn.sigmoid(og_ref[...].astype(F32))).astype(o_ref.dtype)

    b_last = b_col[L - 1:L, :]
    log_w = b_last - b_col + ig_col
    m_new = jnp.maximum(b_last + m_prev, jnp.max(log_w, axis=0, keepdims=True))
    w_s = jnp.exp(log_w - m_new)
    carry = jnp.exp(b_last + m_prev - m_new)
    k_w = k.astype(F32) * w_s
    c_ref[...] = carry * c_prev + lax.dot_general(
        k_w.astype(BF16), v, (((0,), (0,)), ((), ())), preferred_element_type=F32)
    n_ref[...] = carry * n_prev + jnp.sum(k_w, axis=0, keepdims=True)
    m_ref[...] = jnp.broadcast_to(m_new, m_ref.shape)


def mlstm_mixer(h, hb, norm_gain, w_in, layer, gate_bias, h_gain, w_out):
    seq, d = h.shape
    qk_w = B_HEADS * B_QK_DIM
    v_w = B_HEADS * B_V_DIM
    main_w = 2 * qk_w + v_w + d
    proj = fused_matmul(hb, w_in, layer, 0, main_w, _ep_identity, out_dtype=BF16, norm_gain=norm_gain,
                        name="mlstm_in_proj")
    n_gate = 2 * B_HEADS
    assert w_in.shape[2] == main_w + n_gate
    bias = jnp.pad(gate_bias, (0, B_GATE_PAD - n_gate)).reshape(1, B_GATE_PAD)
    gates = fused_matmul(hb, w_in, layer, main_w, B_GATE_PAD, _ep_gates, out_dtype=F32, tn=B_GATE_PAD,
                         norm_gain=norm_gain, extras=[(bias, (1, B_GATE_PAD), lambda j, i: (0, 0))],
                         name="mlstm_gate_proj")
    gates_t = gates[:, :n_gate].T
    L = min(B_CHUNK, seq)
    hps = B_HEADS_PER_STEP
    qk_blk, v_blk = hps * B_QK_DIM, hps * B_V_DIM
    nq = qk_w // qk_blk
    nv = v_w // v_blk
    mixed = pl.pallas_call(
        _mlstm_kernel,
        out_shape=jax.ShapeDtypeStruct((seq, v_w), BF16),
        grid=(B_HEADS // hps, seq // L),
        in_specs=[pl.BlockSpec((L, qk_blk), lambda g, c: (c, g)),
                  pl.BlockSpec((L, qk_blk), lambda g, c: (c, nq + g)),
                  pl.BlockSpec((L, v_blk), lambda g, c: (c, (2 * qk_w) // v_blk + g)),
                  pl.BlockSpec((L, v_blk), lambda g, c: (c, (2 * qk_w) // v_blk + nv + g)),
                  pl.BlockSpec((L, B_GATE_PAD), lambda g, c: (c, 0)),
                  pl.BlockSpec((n_gate, L), lambda g, c: (0, c)),
                  pl.BlockSpec((1, v_blk), lambda g, c: (0, g))],
        out_specs=pl.BlockSpec((L, v_blk), lambda g, c: (c, g)),
        scratch_shapes=[pltpu.VMEM((hps, B_QK_DIM, B_V_DIM), F32),
                        pltpu.VMEM((hps, 1, B_QK_DIM), F32),
                        pltpu.VMEM((hps, 8, LANES), F32)],
        compiler_params=_params(2),
        name="mlstm_chunks",
    )(proj, proj, proj, proj, gates, gates_t, h_gain.reshape(1, v_w))
    return matmul_residual(mixed, w_out, layer, h, name="mlstm_out_proj")


def _pool_kernel(u_ref, halo_ref, wg_ref, scale_ref, o_ref, wb_ref):
    g = pl.program_id(0)
    i = pl.program_id(1)
    tm = u_ref.shape[0]

    @pl.when(i == 0)
    def _():
        wb_ref[...] = wg_ref[...].astype(BF16)

    win = jnp.left_shift(jnp.int32(C_WINDOWS[0]), g)
    t_i = lax.broadcasted_iota(jnp.int32, (tm, tm), 0)
    s_i = lax.broadcasted_iota(jnp.int32, (tm, tm), 1)
    dist = t_i - s_i
    band = ((dist >= 0) & (dist < win)).astype(BF16)
    t_h = lax.broadcasted_iota(jnp.int32, (tm, C_HALO), 0)
    s_h = lax.broadcasted_iota(jnp.int32, (tm, C_HALO), 1)
    band_halo = ((t_h + C_HALO - s_h < win) & (i > 0)).astype(BF16)
    u = u_ref[...]
    total = (jnp.dot(band, u, preferred_element_type=F32)
             + jnp.dot(band_halo, halo_ref[...], preferred_element_type=F32))
    t_glob = i * tm + lax.broadcasted_iota(jnp.int32, (tm, 1), 0)
    count = jnp.minimum(t_glob + 1, win).astype(F32)
    pooled = total / count - u.astype(F32)
    y = jnp.dot(pooled.astype(BF16), wb_ref[...], preferred_element_type=F32) * scale_ref[...]
    o_ref[...] = y.astype(o_ref.dtype)


def pooling_mixer(h, hb, norm_gain, w_in, layer, w_group, scale, w_out, tm=512):
    seq, d = h.shape
    assert all(C_WINDOWS[g] == C_WINDOWS[0] << g for g in range(len(C_WINDOWS)))
    assert max(C_WINDOWS) <= C_HALO
    u = fused_matmul(hb, w_in, layer, 0, d, _ep_identity, out_dtype=BF16, norm_gain=norm_gain,
                     name="pool_in_proj")
    tm = min(tm, seq)
    gd = C_GROUP_DIM
    halo_per_tile = tm // C_HALO
    y = pl.pallas_call(
        _pool_kernel,
        out_shape=jax.ShapeDtypeStruct((seq, d), BF16),
        grid=(len(C_WINDOWS), seq // tm),
        in_specs=[pl.BlockSpec((tm, gd), lambda g, i: (i, g)),
                  pl.BlockSpec((C_HALO, gd), lambda g, i: (jnp.maximum(i * halo_per_tile - 1, 0), g)),
                  pl.BlockSpec((None, None, gd, gd), lambda g, i: (layer, g, 0, 0)),
                  pl.BlockSpec((1, gd), lambda g, i: (0, g))],
        out_specs=pl.BlockSpec((tm, gd), lambda g, i: (i, g)),
        scratch_shapes=[pltpu.VMEM((gd, gd), BF16)],
        compiler_params=_params(2),
        name="pool_group",
    )(u, u, w_group, scale.reshape(1, d))
    return matmul_residual(y, w_out, layer, h, name="pool_out_proj")


def squared_relu_mlp(h, hb, norm_gain, w_in, w_out, layer):
    hidden, w_out_bf16 = fused_matmul(hb, w_in, layer, 0, w_in.shape[2], _ep_relu_sq, out_dtype=BF16,
                                      norm_gain=norm_gain, side=(w_out, layer), name="mlp_up")
    return matmul_kgrid_residual(hidden, w_out_bf16, h, name="mlp_down")


def kernel(x, p, norm_mix, norm_mlp, norm_ple, w_ple, w_ple_gate, w_mlp_in, w_mlp_out, a_w_in, a_q_norm, a_k_norm, a_w_out, b_w_in, b_gate_bias, b_h_norm, b_w_out, c_w_in, c_w_group, c_scale, c_w_out):
    bsz, seq, d = x.shape
    depth = p.shape[0]
    rope = rope_tables(seq)
    outs = []
    for b in range(bsz):
        h = x.reshape(seq, d) if bsz == 1 else x[b]
        hb = None
        for i in range(depth):
            kind, j = i % 3, i // 3
            if kind == 0:
                h, hb = dilated_attention_mixer(h, norm_mix[i], a_w_in, j, a_q_norm[j], a_k_norm[j], a_w_out, rope)
            elif kind == 1:
                h, hb = mlstm_mixer(h, hb, norm_mix[i], b_w_in, j, b_gate_bias[j], b_h_norm[j], b_w_out)
            else:
                h, hb = pooling_mixer(h, hb, norm_mix[i], c_w_in, j, c_w_group, c_scale[j], c_w_out)
            h, hb = squared_relu_mlp(h, hb, norm_mlp[i], w_mlp_in, w_mlp_out, i)
            h, hb = ple_update(h, hb, norm_ple[i], w_ple_gate, i,
                               p[i, b].astype(BF16), w_ple[i].astype(BF16))
        outs.append(h)
    return outs[0][None] if bsz == 1 else jnp.stack(outs, axis=0)
```

```python
import functools
import math

import jax
import jax.numpy as jnp
from jax import lax
from jax.experimental import pallas as pl
from jax.experimental.pallas import tpu as pltpu

F32 = jnp.float32
BF16 = jnp.bfloat16

RMS_EPS = 1e-6
ROPE_THETA = 10000.0

A_HEAD_DIM = 128
A_HEADS_PER_GROUP = 16
A_PATTERNS = ((128, 1), (512, 4), (2048, 16))
A_STEPS = 128
A_RESIDUES = 16
A_GROUP_WIDTH = A_HEADS_PER_GROUP * A_HEAD_DIM

B_HEADS = 8
B_QK_DIM = 256
B_V_DIM = 512
B_CHUNK = 256
B_HEADS_PER_STEP = 2
B_GATE_PAD = 128

C_WINDOWS = (2, 4, 8, 16)
C_GROUP_DIM = 1024
C_HALO = 128

LANES = 128
VMEM_LIMIT_BYTES = 60 * 1024 * 1024


def _sigmoid(x):
    return 0.5 * jnp.tanh(0.5 * x) + 0.5


def _params(n_grid_axes):
    return pltpu.CompilerParams(
        dimension_semantics=("arbitrary",) * n_grid_axes,
        vmem_limit_bytes=VMEM_LIMIT_BYTES)


def _rmsnorm_rows(x, gain):
    inv = lax.rsqrt(jnp.mean(x * x, axis=-1, keepdims=True) + RMS_EPS)
    return x * inv * gain


def _row_permutation(tm, rows, to_residue_major):
    a = lax.broadcasted_iota(jnp.int32, (tm, tm), 0)
    b = lax.broadcasted_iota(jnp.int32, (tm, tm), 1)
    major, seq_order = (a, b) if to_residue_major else (b, a)
    return (seq_order == (major % rows) * A_RESIDUES + major // rows).astype(BF16)


def _rmsnorm_residue_major_kernel(x_ref, g_ref, o_ref):
    tm = x_ref.shape[0]
    rows = tm // A_RESIDUES
    y = _rmsnorm_rows(x_ref[...], g_ref[...]).astype(BF16)
    y = jnp.dot(_row_permutation(tm, rows, True), y, preferred_element_type=F32)
    o_ref[...] = y.astype(o_ref.dtype).reshape(o_ref.shape)


def rmsnorm_residue_major(h, gain, tm=256):
    m, d = h.shape
    tm = min(tm, m)
    in_specs = [pl.BlockSpec((tm, d), lambda i: (i, 0)),
                pl.BlockSpec((1, d), lambda i: (0, 0))]
    out = pl.pallas_call(
        _rmsnorm_residue_major_kernel,
        out_shape=jax.ShapeDtypeStruct((A_RESIDUES, m // A_RESIDUES, d), BF16),
        grid=(m // tm,), in_specs=in_specs,
        out_specs=pl.BlockSpec((A_RESIDUES, tm // A_RESIDUES, d), lambda i: (0, i, 0)),
        compiler_params=_params(1), name="rmsnorm_residue_major",
    )(h, gain.reshape(1, d))
    return out.reshape(m, d)


SIDE_CAST_ROWS = 64
NORM_ROWS = 64


def _row_inv_rms(x_ref):
    tm, k = x_ref.shape
    rows = min(NORM_ROWS, tm)
    pieces = []
    for r in range(tm // rows):
        part = None
        for c in range(k // LANES):
            xc = x_ref[r * rows:(r + 1) * rows, c * LANES:(c + 1) * LANES].astype(F32)
            part = xc * xc if part is None else part + xc * xc
        pieces.append(jnp.sum(part, axis=-1, keepdims=True))
    return lax.rsqrt(jnp.concatenate(pieces, axis=0) * (1.0 / k) + RMS_EPS)


def _mm_kernel(x_ref, wchunk_ref, *rest, epilogue, n_extra, n_out, n_slabs, n_side_steps, normed):
    if normed:
        gcol_ref, rest = rest[0], rest[1:]
    extra_refs, rest = rest[:n_extra], rest[n_extra:]
    if n_side_steps:
        side_ref, rest = rest[0], rest[1:]
    o_refs, rest = rest[:n_out], rest[n_out:]
    if n_side_steps:
        side_out_ref, rest = rest[0], rest[1:]
    if normed:
        wb_refs, inv_ref = rest[:2], rest[2]
    else:
        wb_refs = rest
    jj, i = pl.program_id(0), pl.program_id(1)
    chunk = wchunk_ref.shape[0]

    def cast_jobs(dst_ref):
        row0 = pl.multiple_of(i * chunk, chunk)
        wc = wchunk_ref[...]
        if normed:
            wc = wc * gcol_ref[...]
        dst_ref[pl.ds(row0, chunk), :] = wc.astype(BF16)
        if n_side_steps:
            side_out_ref[...] = side_ref[...].astype(BF16)

    @pl.when(jj == 0)
    def _():
        cast_jobs(wb_refs[0])

    if normed:
        tile_lane = lax.broadcasted_iota(jnp.int32, inv_ref.shape, 1) == i

        @pl.when((jj == 0) & (i == 0))
        def _():
            inv_ref[...] = jnp.zeros_like(inv_ref)

        @pl.when(jj == 1)
        def _():
            inv_ref[...] = jnp.where(tile_lane, _row_inv_rms(x_ref), inv_ref[...])

    def multiply(cur_ref, next_ref):
        cast_jobs(next_ref)
        acc = jnp.dot(x_ref[...], cur_ref[...], preferred_element_type=F32)
        if normed:
            acc = acc * jnp.sum(jnp.where(tile_lane, inv_ref[...], 0.0), axis=-1, keepdims=True)
        vals = epilogue(acc, slice(None), *extra_refs)
        vals = vals if isinstance(vals, tuple) else (vals,)
        for o_ref, val in zip(o_refs, vals, strict=True):
            o_ref[...] = val.astype(o_ref.dtype)

    @pl.when(jj % 2 == 1)
    def _():
        multiply(wb_refs[0], wb_refs[1])

    @pl.when((jj > 0) & (jj % 2 == 0))
    def _():
        multiply(wb_refs[1], wb_refs[0])


def fused_matmul(x, w, layer, col0, n, epilogue, *, out_dtype, tm=1024, tn=1024, extras=(), side=None,
                 norm_gain=None, name):
    m, k = x.shape
    tm, tn = min(tm, m), min(tn, n)
    n_slabs, n_rows = n // tn, m // tm
    assert w.shape[1] == k and m % tm == 0 and n % tn == 0 and col0 % tn == 0 and k % n_rows == 0, (
        w.shape, m, n, tm, tn, col0)
    jb = col0 // tn
    chunk = k // n_rows
    out_dtypes = out_dtype if isinstance(out_dtype, tuple) else (out_dtype,)

    def lag(im):
        return lambda jj, i: im(jnp.maximum(jj - 1, 0), jnp.where(jj > 0, i, 0))

    chunk_row = lambda jj, i: jnp.where(jj < n_slabs, i, n_rows - 1)
    in_specs = [pl.BlockSpec((tm, k), lag(lambda j, i: (i, 0))),
                pl.BlockSpec((None, chunk, tn),
                             lambda jj, i: (layer, chunk_row(jj, i), jb + jnp.minimum(jj, n_slabs - 1)))]
    operands = [x, w]
    if norm_gain is not None:
        assert n_rows <= LANES
        in_specs.append(pl.BlockSpec((chunk, 1), lambda jj, i: (chunk_row(jj, i), 0)))
        operands.append(norm_gain.reshape(k, 1))
    in_specs += [pl.BlockSpec(bs, lag(im)) for _, bs, im in extras]
    operands += [a for a, _, _ in extras]
    out_shape = [jax.ShapeDtypeStruct((m, n), dt) for dt in out_dtypes]
    out_specs = [pl.BlockSpec((tm, tn), lag(lambda j, i: (i, j))) for _ in out_dtypes]
    n_side_steps = 0
    if side is not None:
        side_w, side_layer = side
        side_rows, side_cols = side_w.shape[1:]
        rows_per_step = SIDE_CAST_ROWS
        while side_rows // rows_per_step > (n_slabs + 1) * n_rows:
            rows_per_step *= 2
        n_side_steps = side_rows // rows_per_step
        assert side_rows % rows_per_step == 0
        side_block = lambda jj, i: jnp.minimum(jj * n_rows + i, n_side_steps - 1)
        in_specs.append(pl.BlockSpec((None, rows_per_step, side_cols),
                                     lambda jj, i: (side_layer, side_block(jj, i), 0)))
        operands.append(side_w)
        out_shape.append(jax.ShapeDtypeStruct((side_rows, side_cols), BF16))
        out_specs.append(pl.BlockSpec((rows_per_step, side_cols), lambda jj, i: (side_block(jj, i), 0)))
    outs = pl.pallas_call(
        functools.partial(_mm_kernel, epilogue=epilogue, n_extra=len(extras), n_out=len(out_dtypes),
                          n_slabs=n_slabs, n_side_steps=n_side_steps, normed=norm_gain is not None),
        out_shape=out_shape,
        grid=(n_slabs + 1, n_rows),
        in_specs=in_specs,
        out_specs=out_specs,
        scratch_shapes=[pltpu.VMEM((k, tn), BF16), pltpu.VMEM((k, tn), BF16)] + (
            [pltpu.VMEM((tm, LANES), F32)] if norm_gain is not None else []),
        compiler_params=_params(2),
        name=name,
    )(*operands)
    return outs[0] if len(outs) == 1 else tuple(outs)


def _ep_identity(acc, cs):
    return acc


def _ep_relu_sq(acc, cs):
    r = jnp.maximum(acc, 0.0)
    return r * r


def _ep_residual(acc, cs, res_ref):
    h = res_ref[:, cs] + acc
    return h, h


def matmul_residual(x, w, layer, res, *, tm=512, tn=1024, name):
    n = w.shape[2]
    tm, tn = min(tm, x.shape[0]), min(tn, n)
    return fused_matmul(x, w, layer, 0, n, _ep_residual, out_dtype=(F32, BF16), tm=tm, tn=tn,
                        extras=[(res, (tm, tn), lambda j, i: (i, j))], name=name)


def _mm_kgrid_kernel(x_ref, w_ref, res_ref, o_ref, ob_ref, acc_ref):
    kk = pl.program_id(2)

    @pl.when(kk == 0)
    def _():
        acc_ref[...] = jnp.zeros_like(acc_ref)

    acc_ref[...] += jnp.dot(x_ref[...], w_ref[...], preferred_element_type=F32)

    @pl.when(kk == pl.num_programs(2) - 1)
    def _():
        h = res_ref[...] + acc_ref[...]
        o_ref[...] = h
        ob_ref[...] = h.astype(ob_ref.dtype)


def matmul_kgrid_residual(x, w, res, *, tm=1024, tn=1024, tk=4096, name):
    m, k = x.shape
    n = w.shape[1]
    tm, tn, tk = min(tm, m), min(tn, n), min(tk, k)
    assert m % tm == 0 and n % tn == 0 and k % tk == 0
    single = pl.Buffered(1) if k // tk > 1 else None
    return pl.pallas_call(
        _mm_kgrid_kernel,
        out_shape=(jax.ShapeDtypeStruct((m, n), F32), jax.ShapeDtypeStruct((m, n), BF16)),
        grid=(m // tm, n // tn, k // tk),
        in_specs=[pl.BlockSpec((tm, tk), lambda i, j, kk: (i, kk)),
                  pl.BlockSpec((tk, tn), lambda i, j, kk: (kk, j)),
                  pl.BlockSpec((tm, tn), lambda i, j, kk: (i, j))],
        out_specs=(pl.BlockSpec((tm, tn), lambda i, j, kk: (i, j), pipeline_mode=single),
                   pl.BlockSpec((tm, tn), lambda i, j, kk: (i, j), pipeline_mode=single)),
        scratch_shapes=[pltpu.VMEM((tm, tn), F32)],
        compiler_params=_params(3),
        name=name,
    )(x, w, res)


def _ep_ple(acc, cs, p_ref, wp_ref, res_ref):
    emb = jnp.dot(p_ref[...], wp_ref[:, cs], preferred_element_type=F32)
    h = res_ref[:, cs] + _sigmoid(acc) * emb
    return h, h


def ple_update(h, hb, norm_gain, w_gate, layer, p, w_ple, *, tm=512, tn=1024):
    n = w_gate.shape[2]
    tm, tn = min(tm, h.shape[0]), min(tn, n)
    kp = p.shape[1]
    return fused_matmul(
        hb, w_gate, layer, 0, n, _ep_ple, out_dtype=(F32, BF16), tm=tm, tn=tn, norm_gain=norm_gain,
        extras=[(p, (tm, kp), lambda j, i: (i, 0)),
                (w_ple, (kp, tn), lambda j, i: (0, j)),
                (h, (tm, tn), lambda j, i: (i, j))],
        name="ple_gate")


def _rope_table_kernel(cos_ref, sin_ref, *, rows_per_residue):
    tm = cos_ref.shape[0]
    half = A_HEAD_DIM // 2
    row0 = pl.program_id(0) * tm
    residue = row0 // rows_per_residue
    n0 = row0 % rows_per_residue
    n = n0 + lax.broadcasted_iota(jnp.int32, (tm, A_HEAD_DIM), 0)
    pos = (n * A_RESIDUES + residue).astype(F32)
    lane = lax.broadcasted_iota(jnp.int32, (tm, A_HEAD_DIM), 1)
    j = jnp.where(lane >= half, lane - half, lane).astype(F32)
    inv_freq = jnp.exp(j * (-2.0 * math.log(ROPE_THETA) / A_HEAD_DIM))
    ang = pos * inv_freq
    cos_ref[...] = jnp.cos(ang)
    sin_ref[...] = jnp.where(lane >= half, 1.0, -1.0) * jnp.sin(ang)


def rope_tables(seq, tm=256):
    rows_per_residue = seq // A_RESIDUES
    tm = min(tm, rows_per_residue)
    assert rows_per_residue % tm == 0
    spec = pl.BlockSpec((tm, A_HEAD_DIM), lambda i: (i, 0))
    shape = jax.ShapeDtypeStruct((seq, A_HEAD_DIM), F32)
    return pl.pallas_call(
        functools.partial(_rope_table_kernel, rows_per_residue=rows_per_residue),
        out_shape=(shape, shape), grid=(seq // tm,),
        in_specs=[], out_specs=(spec, spec), compiler_params=_params(1),
        name="rope_tables")()


def _ep_qk_norm_rope(acc, cs, gain_ref, gain_swapped_ref, cos_ref, sin_ref):
    d = A_HEAD_DIM
    cos_g = cos_ref[...] * gain_ref[:, :d]
    sin_g = sin_ref[...] * gain_swapped_ref[:, :d]
    mean_mat = jnp.full((d, d), 1.0 / d, BF16)
    r_i = lax.broadcasted_iota(jnp.int32, (d, d), 0)
    c_i = lax.broadcasted_iota(jnp.int32, (d, d), 1)
    swap_halves = (r_i == (c_i + d // 2) % d).astype(BF16)
    outs = []
    for hh in range(acc.shape[1] // d):
        a = acc[:, hh * d:(hh + 1) * d]
        mean_sq = jnp.dot((a * a).astype(BF16), mean_mat, preferred_element_type=F32)
        a_swapped = jnp.dot(a.astype(BF16), swap_halves, preferred_element_type=F32)
        outs.append((a * cos_g + a_swapped * sin_g) * lax.rsqrt(mean_sq + RMS_EPS))
    return jnp.concatenate(outs, axis=1)


def _attn_kernel(q_ref, kc_ref, vc_ref, o_ref, lse_ref, kp_ref, vp_ref, *, parts):
    rows = q_ref.shape[-2]
    qb = parts * rows
    n = pl.program_id(1)

    @pl.when(n == 0)
    def _():
        kp_ref[...] = jnp.zeros_like(kp_ref)
        vp_ref[...] = jnp.zeros_like(vp_ref)

    def load(ref, hs):
        return ref[..., hs].reshape(qb, A_HEAD_DIM)

    def step_of(idx):
        return idx if parts == 1 else (idx % rows) * parts + idx // rows

    lq = step_of(lax.broadcasted_iota(jnp.int32, (qb, 2 * qb), 0))
    col = lax.broadcasted_iota(jnp.int32, (qb, 2 * qb), 1)
    is_prev = col < qb
    lk = step_of(jnp.where(is_prev, col, col - qb))
    dist = lq - lk + jnp.where(is_prev, qb, 0)
    ok = (dist >= 0) & (dist <= A_STEPS) & ((n > 0) | jnp.logical_not(is_prev))
    lane = lax.broadcasted_iota(jnp.int32, (qb, LANES), 1)
    scale = A_HEAD_DIM ** -0.5
    nt = (((1,), (1,)), ((), ()))
    ones = jnp.ones((2 * qb, A_HEAD_DIM), BF16)
    lse_tile = jnp.zeros((qb, LANES), F32)
    for hh in range(A_HEADS_PER_GROUP):
        hs = slice(hh * A_HEAD_DIM, (hh + 1) * A_HEAD_DIM)
        q = load(q_ref, hs)
        k_cat = jnp.concatenate([load(kp_ref, hs), load(kc_ref, hs)], axis=0)
        v_cat = jnp.concatenate([load(vp_ref, hs), load(vc_ref, hs)], axis=0)
        s = lax.dot_general(q, k_cat, nt, preferred_element_type=F32) * scale
        s = jnp.where(ok, s, -jnp.inf)
        mx = jnp.max(jnp.maximum(s[:, :qb], s[:, qb:]), axis=-1, keepdims=True)
        e = jnp.exp(s - mx).astype(BF16)
        pv = jnp.dot(e, jnp.concatenate([v_cat, ones], axis=1), preferred_element_type=F32)
        den = pv[:, A_HEAD_DIM:]
        o_ref[..., hs] = (pv[:, :A_HEAD_DIM] / den).reshape(o_ref.shape[:-1] + (A_HEAD_DIM,)).astype(o_ref.dtype)
        lse_tile = jnp.where(lane == hh, mx + jnp.log(den), lse_tile)
    lse_ref[...] = lse_tile.reshape(lse_ref.shape)
    kp_ref[...] = kc_ref[...]
    vp_ref[...] = vc_ref[...]


def dilated_attention(qk, v, group, dilation):
    seq = qk.shape[0]
    gw = A_GROUP_WIDTH
    n_groups = len(A_PATTERNS)
    parts = A_RESIDUES // dilation
    rpr = seq // A_RESIDUES
    rows = max(A_STEPS // parts, 16)
    nb = rpr // rows
    qk_w, v_w = qk.shape[1], v.shape[1]
    if parts == 1:
        vshape = lambda w: (seq, w)
        blk = lambda w: (rows, w)
        at = lambda r, n, c: (r * nb + n, c)
    else:
        vshape = lambda w: (parts, dilation, rpr, w)
        blk = lambda w: (parts, None, rows, w)
        at = lambda r, n, c: (0, r, n, c)
    view = lambda a: a.reshape(vshape(a.shape[1]))
    o, lse = pl.pallas_call(
        functools.partial(_attn_kernel, parts=parts),
        out_shape=(jax.ShapeDtypeStruct(vshape(gw), BF16),
                   jax.ShapeDtypeStruct(vshape(LANES), F32)),
        grid=(dilation, nb),
        in_specs=[pl.BlockSpec(blk(gw), lambda r, n: at(r, n, group)),
                  pl.BlockSpec(blk(gw), lambda r, n: at(r, n, n_groups + group)),
                  pl.BlockSpec(blk(gw), lambda r, n: at(r, n, group))],
        out_specs=(pl.BlockSpec(blk(gw), lambda r, n: at(r, n, 0)),
                   pl.BlockSpec(blk(LANES), lambda r, n: at(r, n, 0))),
        scratch_shapes=[pltpu.VMEM(tuple(d for d in blk(gw) if d is not None), BF16)] * 2,
        compiler_params=_params(2),
        name=f"dilated_attn_g{group}",
    )(view(qk), view(qk), view(v))
    return o.reshape(seq, gw), lse.reshape(seq, LANES)


def _attn_merge_kernel(o0_ref, o1_ref, o2_ref, l0_ref, l1_ref, l2_ref, out_ref):
    rows = o0_ref.shape[1]
    tm = A_RESIDUES * rows

    def flat(ref, sl=slice(None)):
        return ref[:, :, sl].reshape(tm, -1)

    l0, l1, l2 = flat(l0_ref), flat(l1_ref), flat(l2_ref)
    mx = jnp.maximum(jnp.maximum(l0, l1), l2)
    w0, w1, w2 = jnp.exp(l0 - mx), jnp.exp(l1 - mx), jnp.exp(l2 - mx)
    tot = w0 + w1 + w2
    w0, w1, w2 = w0 / tot, w1 / tot, w2 / tot
    to_sequence_order = _row_permutation(tm, rows, False)
    for hh in range(A_HEADS_PER_GROUP):
        hs = slice(hh * A_HEAD_DIM, (hh + 1) * A_HEAD_DIM)
        merged = (w0[:, hh:hh + 1] * flat(o0_ref, hs).astype(F32)
                  + w1[:, hh:hh + 1] * flat(o1_ref, hs).astype(F32)
                  + w2[:, hh:hh + 1] * flat(o2_ref, hs).astype(F32)).astype(BF16)
        out_ref[:, hs] = jnp.dot(to_sequence_order, merged, preferred_element_type=F32).astype(out_ref.dtype)


def attention_merge(outs, lses, rows=16):
    seq, gw = outs[0].shape
    rpr = seq // A_RESIDUES
    rows = min(rows, rpr)
    tm = A_RESIDUES * rows
    o_spec = pl.BlockSpec((A_RESIDUES, rows, gw), lambda i: (0, i, 0))
    l_spec = pl.BlockSpec((A_RESIDUES, rows, LANES), lambda i: (0, i, 0))
    return pl.pallas_call(
        _attn_merge_kernel,
        out_shape=jax.ShapeDtypeStruct((seq, gw), BF16),
        grid=(rpr // rows,),
        in_specs=[o_spec] * 3 + [l_spec] * 3,
        out_specs=pl.BlockSpec((tm, gw), lambda i: (i, 0)),
        compiler_params=_params(1),
        name="attn_merge",
    )(*[o.reshape(A_RESIDUES, rpr, gw) for o in outs],
      *[l.reshape(A_RESIDUES, rpr, LANES) for l in lses])


def dilated_attention_mixer(h, norm_gain, w_in, layer, q_gain, k_gain, w_out, rope):
    seq = h.shape[0]
    n_heads = len(A_PATTERNS) * A_HEADS_PER_GROUP
    qk_width = 2 * n_heads * A_HEAD_DIM
    v_width = n_heads * A_HEAD_DIM
    assert seq % (A_RESIDUES * A_STEPS) == 0
    assert all(w // d == A_STEPS and A_RESIDUES % d == 0 for w, d in A_PATTERNS)
    xn = rmsnorm_residue_major(h, norm_gain)
    cos, sin = rope
    per_head = lambda g_q, g_k: jnp.concatenate([jnp.tile(g_q, n_heads), jnp.tile(g_k, n_heads)]).reshape(1, qk_width)
    gain = per_head(q_gain, k_gain)
    gain_swapped = per_head(jnp.roll(q_gain, A_HEAD_DIM // 2), jnp.roll(k_gain, A_HEAD_DIM // 2))
    tm, tn = min(1024, seq), 1024
    assert (qk_width // 2) % tn == 0
    qk = fused_matmul(
        xn, w_in, layer, 0, qk_width, _ep_qk_norm_rope, out_dtype=BF16, tm=tm, tn=tn,
        extras=[(gain, (1, tn), lambda j, i: (0, j)),
                (gain_swapped, (1, tn), lambda j, i: (0, j)),
                (cos, (tm, A_HEAD_DIM), lambda j, i: (i, 0)),
                (sin, (tm, A_HEAD_DIM), lambda j, i: (i, 0))],
        name="attn_qk_proj")
    v = fused_matmul(xn, w_in, layer, qk_width, v_width, _ep_identity, out_dtype=BF16,
                     tm=tm, tn=tn, name="attn_v_proj")
    outs, lses = [], []
    for g, (_, dilation) in enumerate(A_PATTERNS):
        o, lse = dilated_attention(qk, v, g, dilation)
        outs.append(o)
        lses.append(lse)
    merged = attention_merge(outs, lses)
    return matmul_residual(merged, w_out, layer, h, tm=1024, name="attn_out_proj")


def _ep_gates(acc, cs, bias_ref):
    lane = lax.broadcasted_iota(jnp.int32, acc.shape, 1)
    g = jnp.where(lane < 2 * B_HEADS, acc + bias_ref[...], 0.0)
    return jnp.where(lane >= B_HEADS, jax.nn.log_sigmoid(g), g)


def _mlstm_kernel(q_ref, k_ref, v_ref, og_ref, gcol_ref, grow_ref, gain_ref, o_ref,
                  c_ref, n_ref, m_ref):
    @pl.when(pl.program_id(1) == 0)
    def _():
        c_ref[...] = jnp.zeros_like(c_ref)
        n_ref[...] = jnp.zeros_like(n_ref)
        m_ref[...] = jnp.zeros_like(m_ref)

    for hh in range(B_HEADS_PER_STEP):
        qs = slice(hh * B_QK_DIM, (hh + 1) * B_QK_DIM)
        vs = slice(hh * B_V_DIM, (hh + 1) * B_V_DIM)
        _mlstm_head(pl.program_id(0) * B_HEADS_PER_STEP + hh,
                    q_ref.at[:, qs], k_ref.at[:, qs], v_ref.at[:, vs], og_ref.at[:, vs], gcol_ref, grow_ref,
                    gain_ref.at[:, vs], o_ref.at[:, vs], c_ref.at[hh], n_ref.at[hh], m_ref.at[hh])


def _mlstm_head(hd, q_ref, k_ref, v_ref, og_ref, gcol_ref, grow_ref, gain_ref, o_ref, c_ref, n_ref, m_ref):
    L = q_ref.shape[0]
    gcol = gcol_ref[...]
    lane = lax.broadcasted_iota(jnp.int32, gcol.shape, 1)
    ig_col = jnp.sum(jnp.where(lane == hd, gcol, 0.0), axis=-1, keepdims=True)
    lf_col = jnp.sum(jnp.where(lane == hd + B_HEADS, gcol, 0.0), axis=-1, keepdims=True)
    ig_row = grow_ref[pl.ds(hd, 1), :]
    lf_row = grow_ref[pl.ds(hd + B_HEADS, 1), :]

    t_i = lax.broadcasted_iota(jnp.int32, (L, L), 0)
    s_i = lax.broadcasted_iota(jnp.int32, (L, L), 1)
    causal = s_i <= t_i
    lower = causal.astype(BF16)
    upper = (t_i <= s_i).astype(BF16)

    def split3(a):
        hi = a.astype(BF16).astype(F32)
        r1 = a - hi
        mid = r1.astype(BF16).astype(F32)
        return hi, mid, (r1 - mid).astype(BF16).astype(F32)

    hi, mid, lo = split3(lf_col)
    parts_col = jnp.where(lane == 0, hi, jnp.where(lane == 1, mid, jnp.where(lane == 2, lo, 0.0))).astype(BF16)
    cum = jnp.dot(lower, parts_col, preferred_element_type=F32)
    b_col = cum[:, 0:1] + cum[:, 1:2] + cum[:, 2:3]
    hi, mid, lo = split3(lf_row)
    sub = lax.broadcasted_iota(jnp.int32, (8, L), 0)
    parts_row = jnp.where(sub == 0, hi, jnp.where(sub == 1, mid, jnp.where(sub == 2, lo, 0.0))).astype(BF16)
    cum = jnp.dot(parts_row, upper, preferred_element_type=F32)
    b_row = cum[0:1, :] + cum[1:2, :] + cum[2:3, :]

    m_prev = m_ref[:1, :1]
    c_prev = c_ref[...]
    n_prev = n_ref[...]

    q = q_ref[...] * jnp.asarray(B_QK_DIM ** -0.5, q_ref.dtype)
    k = k_ref[...]
    v = v_ref[...]

    log_d = jnp.where(causal, b_col - b_row + ig_row, -jnp.inf)
    log_inter = b_col + m_prev
    m_t = jnp.maximum(log_inter, jnp.max(log_d, axis=-1, keepdims=True))
    scores = lax.dot_general(q, k, (((1,), (1,)), ((), ())), preferred_element_type=F32)
    w = scores * jnp.exp(log_d - m_t)
    decay = jnp.exp(log_inter - m_t)
    num = (decay * jnp.dot(q, c_prev.astype(BF16), preferred_element_type=F32)
           + jnp.dot(w.astype(BF16), v, preferred_element_type=F32))
    qn = (decay * jnp.sum(q.astype(F32) * n_prev, axis=-1, keepdims=True)
          + jnp.sum(w, axis=-1, keepdims=True))
    hval = num / jnp.maximum(jnp.abs(qn), jnp.exp(-m_t))

    inv = lax.rsqrt(jnp.mean(hval * hval, axis=-1, keepdims=True) + RMS_EPS)
    o_ref[...] = (hval * inv * gain_ref[...] * _sigmoid(og_ref[...].astype(F32))).astype(o_ref.dtype)

    b_last = b_col[L - 1:L, :]
    log_w = b_last - b_col + ig_col
    m_new = jnp.maximum(b_last + m_prev, jnp.max(log_w, axis=0, keepdims=True))
    w_s = jnp.exp(log_w - m_new)
    carry = jnp.exp(b_last + m_prev - m_new)
    k_w = k.astype(F32) * w_s
    c_ref[...] = carry * c_prev + lax.dot_general(
        k_w.astype(BF16), v, (((0,), (0,)), ((), ())), preferred_element_type=F32)
    n_ref[...] = carry * n_prev + jnp.sum(k_w, axis=0, keepdims=True)
    m_ref[...] = jnp.broadcast_to(m_new, m_ref.shape)


def mlstm_mixer(h, hb, norm_gain, w_in, layer, gate_bias, h_gain, w_out):
    seq, d = h.shape
    qk_w = B_HEADS * B_QK_DIM
    v_w = B_HEADS * B_V_DIM
    main_w = 2 * qk_w + v_w + d
    proj = fused_matmul(hb, w_in, layer, 0, main_w, _ep_identity, out_dtype=BF16, norm_gain=norm_gain,
                        name="mlstm_in_proj")
    n_gate = 2 * B_HEADS
    assert w_in.shape[2] == main_w + n_gate
    bias = jnp.pad(gate_bias, (0, B_GATE_PAD - n_gate)).reshape(1, B_GATE_PAD)
    gates = fused_matmul(hb, w_in, layer, main_w, B_GATE_PAD, _ep_gates, out_dtype=F32, tn=B_GATE_PAD,
                         norm_gain=norm_gain, extras=[(bias, (1, B_GATE_PAD), lambda j, i: (0, 0))],
                         name="mlstm_gate_proj")
    gates_t = gates[:, :n_gate].T
    L = min(B_CHUNK, seq)
    hps = B_HEADS_PER_STEP
    qk_blk, v_blk = hps * B_QK_DIM, hps * B_V_DIM
    nq = qk_w // qk_blk
    nv = v_w // v_blk
    mixed = pl.pallas_call(
        _mlstm_kernel,
        out_shape=jax.ShapeDtypeStruct((seq, v_w), BF16),
        grid=(B_HEADS // hps, seq // L),
        in_specs=[pl.BlockSpec((L, qk_blk), lambda g, c: (c, g)),
                  pl.BlockSpec((L, qk_blk), lambda g, c: (c, nq + g)),
                  pl.BlockSpec((L, v_blk), lambda g, c: (c, (2 * qk_w) // v_blk + g)),
                  pl.BlockSpec((L, v_blk), lambda g, c: (c, (2 * qk_w) // v_blk + nv + g)),
                  pl.BlockSpec((L, B_GATE_PAD), lambda g, c: (c, 0)),
                  pl.BlockSpec((n_gate, L), lambda g, c: (0, c)),
                  pl.BlockSpec((1, v_blk), lambda g, c: (0, g))],
        out_specs=pl.BlockSpec((L, v_blk), lambda g, c: (c, g)),
        scratch_shapes=[pltpu.VMEM((hps, B_QK_DIM, B_V_DIM), F32),
                        pltpu.VMEM((hps, 1, B_QK_DIM), F32),
                        pltpu.VMEM((hps, 8, LANES), F32)],
        compiler_params=_params(2),
        name="mlstm_chunks",
    )(proj, proj, proj, proj, gates, gates_t, h_gain.reshape(1, v_w))
    return matmul_residual(mixed, w_out, layer, h, name="mlstm_out_proj")


def _pool_kernel(u_ref, halo_ref, wg_ref, scale_ref, o_ref, wb_ref):
    g = pl.program_id(0)
    i = pl.program_id(1)
    tm = u_ref.shape[0]

    @pl.when(i == 0)
    def _():
        wb_ref[...] = wg_ref[...].astype(BF16)

    win = jnp.left_shift(jnp.int32(C_WINDOWS[0]), g)
    t_i = lax.broadcasted_iota(jnp.int32, (tm, tm), 0)
    s_i = lax.broadcasted_iota(jnp.int32, (tm, tm), 1)
    dist = t_i - s_i
    band = ((dist >= 0) & (dist < win)).astype(BF16)
    t_h = lax.broadcasted_iota(jnp.int32, (tm, C_HALO), 0)
    s_h = lax.broadcasted_iota(jnp.int32, (tm, C_HALO), 1)
    band_halo = ((t_h + C_HALO - s_h < win) & (i > 0)).astype(BF16)
    u = u_ref[...]
    total = (jnp.dot(band, u, preferred_element_type=F32)
             + jnp.dot(band_halo, halo_ref[...], preferred_element_type=F32))
    t_glob = i * tm + lax.broadcasted_iota(jnp.int32, (tm, 1), 0)
    count = jnp.minimum(t_glob + 1, win).astype(F32)
    pooled = total / count - u.astype(F32)
    y = jnp.dot(pooled.astype(BF16), wb_ref[...], preferred_element_type=F32) * scale_ref[...]
    o_ref[...] = y.astype(o_ref.dtype)


def pooling_mixer(h, hb, norm_gain, w_in, layer, w_group, scale, w_out, tm=512):
    seq, d = h.shape
    assert all(C_WINDOWS[g] == C_WINDOWS[0] << g for g in range(len(C_WINDOWS)))
    assert max(C_WINDOWS) <= C_HALO
    u = fused_matmul(hb, w_in, layer, 0, d, _ep_identity, out_dtype=BF16, norm_gain=norm_gain,
                     name="pool_in_proj")
    tm = min(tm, seq)
    gd = C_GROUP_DIM
    halo_per_tile = tm // C_HALO
    y = pl.pallas_call(
        _pool_kernel,
        out_shape=jax.ShapeDtypeStruct((seq, d), BF16),
        grid=(len(C_WINDOWS), seq // tm),
        in_specs=[pl.BlockSpec((tm, gd), lambda g, i: (i, g)),
                  pl.BlockSpec((C_HALO, gd), lambda g, i: (jnp.maximum(i * halo_per_tile - 1, 0), g)),
                  pl.BlockSpec((None, None, gd, gd), lambda g, i: (layer, g, 0, 0)),
                  pl.BlockSpec((1, gd), lambda g, i: (0, g))],
        out_specs=pl.BlockSpec((tm, gd), lambda g, i: (i, g)),
        scratch_shapes=[pltpu.VMEM((gd, gd), BF16)],
        compiler_params=_params(2),
        name="pool_group",
    )(u, u, w_group, scale.reshape(1, d))
    return matmul_residual(y, w_out, layer, h, name="pool_out_proj")


def squared_relu_mlp(h, hb, norm_gain, w_in, w_out, layer):
    hidden, w_out_bf16 = fused_matmul(hb, w_in, layer, 0, w_in.shape[2], _ep_relu_sq, out_dtype=BF16,
                                      norm_gain=norm_gain, side=(w_out, layer), name="mlp_up")
    return matmul_kgrid_residual(hidden, w_out_bf16, h, name="mlp_down")


def kernel(x, p, norm_mix, norm_mlp, norm_ple, w_ple, w_ple_gate, w_mlp_in, w_mlp_out, a_w_in, a_q_norm, a_k_norm, a_w_out, b_w_in, b_gate_bias, b_h_norm, b_w_out, c_w_in, c_w_group, c_scale, c_w_out):
    bsz, seq, d = x.shape
    depth = p.shape[0]
    rope = rope_tables(seq)
    outs = []
    for b in range(bsz):
        h = x.reshape(seq, d) if bsz == 1 else x[b]
        hb = None
        for i in range(depth):
            kind, j = i % 3, i // 3
            if kind == 0:
                h, hb = dilated_attention_mixer(h, norm_mix[i], a_w_in, j, a_q_norm[j], a_k_norm[j], a_w_out, rope)
            elif kind == 1:
                h, hb = mlstm_mixer(h, hb, norm_mix[i], b_w_in, j, b_gate_bias[j], b_h_norm[j], b_w_out)
            else:
                h, hb = pooling_mixer(h, hb, norm_mix[i], c_w_in, j, c_w_group, c_scale[j], c_w_out)
            h, hb = squared_relu_mlp(h, hb, norm_mlp[i], w_mlp_in, w_mlp_out, i)
            h, hb = ple_update(h, hb, norm_ple[i], w_ple_gate, i,
                               p[i, b].astype(BF16), w_ple[i].astype(BF16))
        outs.append(h)
    return outs[0][None] if bsz == 1 else jnp.stack(outs, axis=0)
```

```python
import functools
import math

import jax
import jax.numpy as jnp
from jax import lax
from jax.experimental import pallas as pl
from jax.experimental.pallas import tpu as pltpu

F32 = jnp.float32
BF16 = jnp.bfloat16

RMS_EPS = 1e-6
ROPE_THETA = 10000.0

A_HEAD_DIM = 128
A_HEADS_PER_GROUP = 16
A_PATTERNS = ((128, 1), (512, 4), (2048, 16))
A_STEPS = 128
A_RESIDUES = 16
A_GROUP_WIDTH = A_HEADS_PER_GROUP * A_HEAD_DIM

B_HEADS = 8
B_QK_DIM = 256
B_V_DIM = 512
B_CHUNK = 256
B_HEADS_PER_STEP = 2
B_GATE_PAD = 128

C_WINDOWS = (2, 4, 8, 16)
C_GROUP_DIM = 1024
C_HALO = 128

LANES = 128
VMEM_LIMIT_BYTES = 60 * 1024 * 1024


def _sigmoid(x):
    return 0.5 * jnp.tanh(0.5 * x) + 0.5


def _params(n_grid_axes):
    return pltpu.CompilerParams(
        dimension_semantics=("arbitrary",) * n_grid_axes,
        vmem_limit_bytes=VMEM_LIMIT_BYTES)


def _rmsnorm_rows(x, gain):
    inv = lax.rsqrt(jnp.mean(x * x, axis=-1, keepdims=True) + RMS_EPS)
    return x * inv * gain


def _row_permutation(tm, rows, to_residue_major):
    a = lax.broadcasted_iota(jnp.int32, (tm, tm), 0)
    b = lax.broadcasted_iota(jnp.int32, (tm, tm), 1)
    major, seq_order = (a, b) if to_residue_major else (b, a)
    return (seq_order == (major % rows) * A_RESIDUES + major // rows).astype(BF16)


def _rmsnorm_residue_major_kernel(x_ref, g_ref, o_ref):
    tm = x_ref.shape[0]
    rows = tm // A_RESIDUES
    y = _rmsnorm_rows(x_ref[...], g_ref[...]).astype(BF16)
    y = jnp.dot(_row_permutation(tm, rows, True), y, preferred_element_type=F32)
    o_ref[...] = y.astype(o_ref.dtype).reshape(o_ref.shape)


def rmsnorm_residue_major(h, gain, tm=256):
    m, d = h.shape
    tm = min(tm, m)
    in_specs = [pl.BlockSpec((tm, d), lambda i: (i, 0)),
                pl.BlockSpec((1, d), lambda i: (0, 0))]
    out = pl.pallas_call(
        _rmsnorm_residue_major_kernel,
        out_shape=jax.ShapeDtypeStruct((A_RESIDUES, m // A_RESIDUES, d), BF16),
        grid=(m // tm,), in_specs=in_specs,
        out_specs=pl.BlockSpec((A_RESIDUES, tm // A_RESIDUES, d), lambda i: (0, i, 0)),
        compiler_params=_params(1), name="rmsnorm_residue_major",
    )(h, gain.reshape(1, d))
    return out.reshape(m, d)


SIDE_CAST_ROWS = 64
NORM_ROWS = 64


def _row_inv_rms(x_ref):
    tm, k = x_ref.shape
    rows = min(NORM_ROWS, tm)
    pieces = []
    for r in range(tm // rows):
        part = None
        for c in range(k // LANES):
            xc = x_ref[r * rows:(r + 1) * rows, c * LANES:(c + 1) * LANES].astype(F32)
            part = xc * xc if part is None else part + xc * xc
        pieces.append(jnp.sum(part, axis=-1, keepdims=True))
    return lax.rsqrt(jnp.concatenate(pieces, axis=0) * (1.0 / k) + RMS_EPS)


def _mm_kernel(x_ref, wchunk_ref, *rest, epilogue, n_extra, n_out, n_slabs, n_side_steps, normed):
    if normed:
        gcol_ref, rest = rest[0], rest[1:]
    extra_refs, rest = rest[:n_extra], rest[n_extra:]
    if n_side_steps:
        side_ref, rest = rest[0], rest[1:]
    o_refs, rest = rest[:n_out], rest[n_out:]
    if n_side_steps:
        side_out_ref, rest = rest[0], rest[1:]
    if normed:
        wb_refs, inv_ref = rest[:2], rest[2]
    else:
        wb_refs = rest
    jj, i = pl.program_id(0), pl.program_id(1)
    chunk = wchunk_ref.shape[0]

    def cast_jobs(dst_ref):
        row0 = pl.multiple_of(i * chunk, chunk)
        wc = wchunk_ref[...]
        if normed:
            wc = wc * gcol_ref[...]
        dst_ref[pl.ds(row0, chunk), :] = wc.astype(BF16)
        if n_side_steps:
            side_out_ref[...] = side_ref[...].astype(BF16)

    @pl.when(jj == 0)
    def _():
        cast_jobs(wb_refs[0])

    if normed:
        tile_lane = lax.broadcasted_iota(jnp.int32, inv_ref.shape, 1) == i

        @pl.when((jj == 0) & (i == 0))
        def _():
            inv_ref[...] = jnp.zeros_like(inv_ref)

        @pl.when(jj == 1)
        def _():
            inv_ref[...] = jnp.where(tile_lane, _row_inv_rms(x_ref), inv_ref[...])

    def multiply(cur_ref, next_ref):
        cast_jobs(next_ref)
        acc = jnp.dot(x_ref[...], cur_ref[...], preferred_element_type=F32)
        if normed:
            acc = acc * jnp.sum(jnp.where(tile_lane, inv_ref[...], 0.0), axis=-1, keepdims=True)
        vals = epilogue(acc, slice(None), *extra_refs)
        vals = vals if isinstance(vals, tuple) else (vals,)
        for o_ref, val in zip(o_refs, vals, strict=True):
            o_ref[...] = val.astype(o_ref.dtype)

    @pl.when(jj % 2 == 1)
    def _():
        multiply(wb_refs[0], wb_refs[1])

    @pl.when((jj > 0) & (jj % 2 == 0))
    def _():
        multiply(wb_refs[1], wb_refs[0])


def fused_matmul(x, w, layer, col0, n, epilogue, *, out_dtype, tm=1024, tn=1024, extras=(), side=None,
                 norm_gain=None, name):
    m, k = x.shape
    tm, tn = min(tm, m), min(tn, n)
    n_slabs, n_rows = n // tn, m // tm
    assert w.shape[1] == k and m % tm == 0 and n % tn == 0 and col0 % tn == 0 and k % n_rows == 0, (
        w.shape, m, n, tm, tn, col0)
    jb = col0 // tn
    chunk = k // n_rows
    out_dtypes = out_dtype if isinstance(out_dtype, tuple) else (out_dtype,)

    def lag(im):
        return lambda jj, i: im(jnp.maximum(jj - 1, 0), jnp.where(jj > 0, i, 0))

    chunk_row = lambda jj, i: jnp.where(jj < n_slabs, i, n_rows - 1)
    in_specs = [pl.BlockSpec((tm, k), lag(lambda j, i: (i, 0))),
                pl.BlockSpec((None, chunk, tn),
                             lambda jj, i: (layer, chunk_row(jj, i), jb + jnp.minimum(jj, n_slabs - 1)))]
    operands = [x, w]
    if norm_gain is not None:
        assert n_rows <= LANES
        in_specs.append(pl.BlockSpec((chunk, 1), lambda jj, i: (chunk_row(jj, i), 0)))
        operands.append(norm_gain.reshape(k, 1))
    in_specs += [pl.BlockSpec(bs, lag(im)) for _, bs, im in extras]
    operands += [a for a, _, _ in extras]
    out_shape = [jax.ShapeDtypeStruct((m, n), dt) for dt in out_dtypes]
    out_specs = [pl.BlockSpec((tm, tn), lag(lambda j, i: (i, j))) for _ in out_dtypes]
    n_side_steps = 0
    if side is not None:
        side_w, side_layer = side
        side_rows, side_cols = side_w.shape[1:]
        rows_per_step = SIDE_CAST_ROWS
        while side_rows // rows_per_step > (n_slabs + 1) * n_rows:
            rows_per_step *= 2
        n_side_steps = side_rows // rows_per_step
        assert side_rows % rows_per_step == 0
        side_block = lambda jj, i: jnp.minimum(jj * n_rows + i, n_side_steps - 1)
        in_specs.append(pl.BlockSpec((None, rows_per_step, side_cols),
                                     lambda jj, i: (side_layer, side_block(jj, i), 0)))
        operands.append(side_w)
        out_shape.append(jax.ShapeDtypeStruct((side_rows, side_cols), BF16))
        out_specs.append(pl.BlockSpec((rows_per_step, side_cols), lambda jj, i: (side_block(jj, i), 0)))
    outs = pl.pallas_call(
        functools.partial(_mm_kernel, epilogue=epilogue, n_extra=len(extras), n_out=len(out_dtypes),
                          n_slabs=n_slabs, n_side_steps=n_side_steps, normed=norm_gain is not None),
        out_shape=out_shape,
        grid=(n_slabs + 1, n_rows),
        in_specs=in_specs,
        out_specs=out_specs,
        scratch_shapes=[pltpu.VMEM((k, tn), BF16), pltpu.VMEM((k, tn), BF16)] + (
            [pltpu.VMEM((tm, LANES), F32)] if norm_gain is not None else []),
        compiler_params=_params(2),
        name=name,
    )(*operands)
    return outs[0] if len(outs) == 1 else tuple(outs)


def _ep_identity(acc, cs):
    return acc


def _ep_relu_sq(acc, cs):
    r = jnp.maximum(acc, 0.0)
    return r * r


def _ep_residual(acc, cs, res_ref):
    h = res_ref[:, cs] + acc
    return h, h


def matmul_residual(x, w, layer, res, *, tm=512, tn=1024, name):
    n = w.shape[2]
    tm, tn = min(tm, x.shape[0]), min(tn, n)
    return fused_matmul(x, w, layer, 0, n, _ep_residual, out_dtype=(F32, BF16), tm=tm, tn=tn,
                        extras=[(res, (tm, tn), lambda j, i: (i, j))], name=name)


def _mm_kgrid_kernel(x_ref, w_ref, res_ref, o_ref, ob_ref, acc_ref):
    kk = pl.program_id(2)
    band = res_ref.shape[0]

    @pl.when(kk == 0)
    def _():
        acc_ref[...] = jnp.zeros_like(acc_ref)

    acc_ref[...] += jnp.dot(x_ref[...], w_ref[...], preferred_element_type=F32)
    row0 = pl.multiple_of(kk * band, band)
    acc_ref[pl.ds(row0, band), :] += res_ref[...]

    @pl.when(kk == pl.num_programs(2) - 1)
    def _():
        h = acc_ref[...]
        o_ref[...] = h
        ob_ref[...] = h.astype(ob_ref.dtype)


def matmul_kgrid_residual(x, w, res, *, tm=1024, tn=1024, tk=4096, name):
    m, k = x.shape
    n = w.shape[1]
    tm, tn, tk = min(tm, m), min(tn, n), min(tk, k)
    nk = k // tk
    assert m % tm == 0 and n % tn == 0 and k % tk == 0 and tm % (8 * nk) == 0
    return pl.pallas_call(
        _mm_kgrid_kernel,
        out_shape=(jax.ShapeDtypeStruct((m, n), F32), jax.ShapeDtypeStruct((m, n), BF16)),
        grid=(m // tm, n // tn, k // tk),
        in_specs=[pl.BlockSpec((tm, tk), lambda i, j, kk: (i, kk)),
                  pl.BlockSpec((tk, tn), lambda i, j, kk: (kk, j)),
                  pl.BlockSpec((tm // nk, tn), lambda i, j, kk: (i * nk + kk, j))],
        out_specs=(pl.BlockSpec((tm, tn), lambda i, j, kk: (i, j)),
                   pl.BlockSpec((tm, tn), lambda i, j, kk: (i, j))),
        scratch_shapes=[pltpu.VMEM((tm, tn), F32)],
        compiler_params=_params(3),
        name=name,
    )(x, w, res)


def _ep_ple(acc, cs, p_ref, wp_ref, res_ref):
    emb = jnp.dot(p_ref[...], wp_ref[:, cs], preferred_element_type=F32)
    h = res_ref[:, cs] + _sigmoid(acc) * emb
    return h, h


def ple_update(h, hb, norm_gain, w_gate, layer, p, w_ple, *, tm=512, tn=1024):
    n = w_gate.shape[2]
    tm, tn = min(tm, h.shape[0]), min(tn, n)
    kp = p.shape[1]
    return fused_matmul(
        hb, w_gate, layer, 0, n, _ep_ple, out_dtype=(F32, BF16), tm=tm, tn=tn, norm_gain=norm_gain,
        extras=[(p, (tm, kp), lambda j, i: (i, 0)),
                (w_ple, (kp, tn), lambda j, i: (0, j)),
                (h, (tm, tn), lambda j, i: (i, j))],
        name="ple_gate")


def _rope_table_kernel(cos_ref, sin_ref, *, rows_per_residue):
    tm = cos_ref.shape[0]
    half = A_HEAD_DIM // 2
    row0 = pl.program_id(0) * tm
    residue = row0 // rows_per_residue
    n0 = row0 % rows_per_residue
    n = n0 + lax.broadcasted_iota(jnp.int32, (tm, A_HEAD_DIM), 0)
    pos = (n * A_RESIDUES + residue).astype(F32)
    lane = lax.broadcasted_iota(jnp.int32, (tm, A_HEAD_DIM), 1)
    j = jnp.where(lane >= half, lane - half, lane).astype(F32)
    inv_freq = jnp.exp(j * (-2.0 * math.log(ROPE_THETA) / A_HEAD_DIM))
    ang = pos * inv_freq
    cos_ref[...] = jnp.cos(ang)
    sin_ref[...] = jnp.where(lane >= half, 1.0, -1.0) * jnp.sin(ang)


def rope_tables(seq, tm=256):
    rows_per_residue = seq // A_RESIDUES
    tm = min(tm, rows_per_residue)
    assert rows_per_residue % tm == 0
    spec = pl.BlockSpec((tm, A_HEAD_DIM), lambda i: (i, 0))
    shape = jax.ShapeDtypeStruct((seq, A_HEAD_DIM), F32)
    return pl.pallas_call(
        functools.partial(_rope_table_kernel, rows_per_residue=rows_per_residue),
        out_shape=(shape, shape), grid=(seq // tm,),
        in_specs=[], out_specs=(spec, spec), compiler_params=_params(1),
        name="rope_tables")()


def _ep_qk_norm_rope(acc, cs, gain_ref, gain_swapped_ref, cos_ref, sin_ref):
    d = A_HEAD_DIM
    cos_g = cos_ref[...] * gain_ref[:, :d]
    sin_g = sin_ref[...] * gain_swapped_ref[:, :d]
    mean_mat = jnp.full((d, d), 1.0 / d, BF16)
    r_i = lax.broadcasted_iota(jnp.int32, (d, d), 0)
    c_i = lax.broadcasted_iota(jnp.int32, (d, d), 1)
    swap_halves = (r_i == (c_i + d // 2) % d).astype(BF16)
    outs = []
    for hh in range(acc.shape[1] // d):
        a = acc[:, hh * d:(hh + 1) * d]
        mean_sq = jnp.dot((a * a).astype(BF16), mean_mat, preferred_element_type=F32)
        a_swapped = jnp.dot(a.astype(BF16), swap_halves, preferred_element_type=F32)
        outs.append((a * cos_g + a_swapped * sin_g) * lax.rsqrt(mean_sq + RMS_EPS))
    return jnp.concatenate(outs, axis=1)


def _attn_kernel(q_ref, kc_ref, vc_ref, o_ref, lse_ref, kp_ref, vp_ref, *, parts):
    rows = q_ref.shape[-2]
    qb = parts * rows
    n = pl.program_id(1)

    @pl.when(n == 0)
    def _():
        kp_ref[...] = jnp.zeros_like(kp_ref)
        vp_ref[...] = jnp.zeros_like(vp_ref)

    def load(ref, hs):
        return ref[..., hs].reshape(qb, A_HEAD_DIM)

    def step_of(idx):
        return idx if parts == 1 else (idx % rows) * parts + idx // rows

    lq = step_of(lax.broadcasted_iota(jnp.int32, (qb, 2 * qb), 0))
    col = lax.broadcasted_iota(jnp.int32, (qb, 2 * qb), 1)
    is_prev = col < qb
    lk = step_of(jnp.where(is_prev, col, col - qb))
    dist = lq - lk + jnp.where(is_prev, qb, 0)
    ok = (dist >= 0) & (dist <= A_STEPS) & ((n > 0) | jnp.logical_not(is_prev))
    lane = lax.broadcasted_iota(jnp.int32, (qb, LANES), 1)
    scale = A_HEAD_DIM ** -0.5
    nt = (((1,), (1,)), ((), ()))
    ones = jnp.ones((2 * qb, A_HEAD_DIM), BF16)
    lse_tile = jnp.zeros((qb, LANES), F32)
    for hh in range(A_HEADS_PER_GROUP):
        hs = slice(hh * A_HEAD_DIM, (hh + 1) * A_HEAD_DIM)
        q = load(q_ref, hs)
        k_cat = jnp.concatenate([load(kp_ref, hs), load(kc_ref, hs)], axis=0)
        v_cat = jnp.concatenate([load(vp_ref, hs), load(vc_ref, hs)], axis=0)
        s = lax.dot_general(q, k_cat, nt, preferred_element_type=F32) * scale
        s = jnp.where(ok, s, -jnp.inf)
        mx = jnp.max(jnp.maximum(s[:, :qb], s[:, qb:]), axis=-1, keepdims=True)
        e = jnp.exp(s - mx).astype(BF16)
        pv = jnp.dot(e, jnp.concatenate([v_cat, ones], axis=1), preferred_element_type=F32)
        den = pv[:, A_HEAD_DIM:]
        o_ref[..., hs] = (pv[:, :A_HEAD_DIM] / den).reshape(o_ref.shape[:-1] + (A_HEAD_DIM,)).astype(o_ref.dtype)
        lse_tile = jnp.where(lane == hh, mx + jnp.log(den), lse_tile)
    lse_ref[...] = lse_tile.reshape(lse_ref.shape)
    kp_ref[...] = kc_ref[...]
    vp_ref[...] = vc_ref[...]


def dilated_attention(qk, v, group, dilation):
    seq = qk.shape[0]
    gw = A_GROUP_WIDTH
    n_groups = len(A_PATTERNS)
    parts = A_RESIDUES // dilation
    rpr = seq // A_RESIDUES
    rows = max(A_STEPS // parts, 16)
    nb = rpr // rows
    qk_w, v_w = qk.shape[1], v.shape[1]
    if parts == 1:
        vshape = lambda w: (seq, w)
        blk = lambda w: (rows, w)
        at = lambda r, n, c: (r * nb + n, c)
    else:
        vshape = lambda w: (parts, dilation, rpr, w)
        blk = lambda w: (parts, None, rows, w)
        at = lambda r, n, c: (0, r, n, c)
    view = lambda a: a.reshape(vshape(a.shape[1]))
    o, lse = pl.pallas_call(
        functools.partial(_attn_kernel, parts=parts),
        out_shape=(jax.ShapeDtypeStruct(vshape(gw), BF16),
                   jax.ShapeDtypeStruct(vshape(LANES), F32)),
        grid=(dilation, nb),
        in_specs=[pl.BlockSpec(blk(gw), lambda r, n: at(r, n, group)),
                  pl.BlockSpec(blk(gw), lambda r, n: at(r, n, n_groups + group)),
                  pl.BlockSpec(blk(gw), lambda r, n: at(r, n, group))],
        out_specs=(pl.BlockSpec(blk(gw), lambda r, n: at(r, n, 0)),
                   pl.BlockSpec(blk(LANES), lambda r, n: at(r, n, 0))),
        scratch_shapes=[pltpu.VMEM(tuple(d for d in blk(gw) if d is not None), BF16)] * 2,
        compiler_params=_params(2),
        name=f"dilated_attn_g{group}",
    )(view(qk), view(qk), view(v))
    return o.reshape(seq, gw), lse.reshape(seq, LANES)


def _attn_merge_kernel(o0_ref, o1_ref, o2_ref, l0_ref, l1_ref, l2_ref, out_ref):
    rows = o0_ref.shape[1]
    tm = A_RESIDUES * rows

    def flat(ref, sl=slice(None)):
        return ref[:, :, sl].reshape(tm, -1)

    l0, l1, l2 = flat(l0_ref), flat(l1_ref), flat(l2_ref)
    mx = jnp.maximum(jnp.maximum(l0, l1), l2)
    w0, w1, w2 = jnp.exp(l0 - mx), jnp.exp(l1 - mx), jnp.exp(l2 - mx)
    tot = w0 + w1 + w2
    w0, w1, w2 = w0 / tot, w1 / tot, w2 / tot
    to_sequence_order = _row_permutation(tm, rows, False)
    for hh in range(A_HEADS_PER_GROUP):
        hs = slice(hh * A_HEAD_DIM, (hh + 1) * A_HEAD_DIM)
        merged = (w0[:, hh:hh + 1] * flat(o0_ref, hs).astype(F32)
                  + w1[:, hh:hh + 1] * flat(o1_ref, hs).astype(F32)
                  + w2[:, hh:hh + 1] * flat(o2_ref, hs).astype(F32)).astype(BF16)
        out_ref[:, hs] = jnp.dot(to_sequence_order, merged, preferred_element_type=F32).astype(out_ref.dtype)


def attention_merge(outs, lses, rows=16):
    seq, gw = outs[0].shape
    rpr = seq // A_RESIDUES
    rows = min(rows, rpr)
    tm = A_RESIDUES * rows
    o_spec = pl.BlockSpec((A_RESIDUES, rows, gw), lambda i: (0, i, 0))
    l_spec = pl.BlockSpec((A_RESIDUES, rows, LANES), lambda i: (0, i, 0))
    return pl.pallas_call(
        _attn_merge_kernel,
        out_shape=jax.ShapeDtypeStruct((seq, gw), BF16),
        grid=(rpr // rows,),
        in_specs=[o_spec] * 3 + [l_spec] * 3,
        out_specs=pl.BlockSpec((tm, gw), lambda i: (i, 0)),
        compiler_params=_params(1),
        name="attn_merge",
    )(*[o.reshape(A_RESIDUES, rpr, gw) for o in outs],
      *[l.reshape(A_RESIDUES, rpr, LANES) for l in lses])


def dilated_attention_mixer(h, norm_gain, w_in, layer, q_gain, k_gain, w_out, rope):
    seq = h.shape[0]
    n_heads = len(A_PATTERNS) * A_HEADS_PER_GROUP
    qk_width = 2 * n_heads * A_HEAD_DIM
    v_width = n_heads * A_HEAD_DIM
    assert seq % (A_RESIDUES * A_STEPS) == 0
    assert all(w // d == A_STEPS and A_RESIDUES % d == 0 for w, d in A_PATTERNS)
    xn = rmsnorm_residue_major(h, norm_gain)
    cos, sin = rope
    per_head = lambda g_q, g_k: jnp.concatenate([jnp.tile(g_q, n_heads), jnp.tile(g_k, n_heads)]).reshape(1, qk_width)
    gain = per_head(q_gain, k_gain)
    gain_swapped = per_head(jnp.roll(q_gain, A_HEAD_DIM // 2), jnp.roll(k_gain, A_HEAD_DIM // 2))
    tm, tn = min(1024, seq), 1024
    assert (qk_width // 2) % tn == 0
    qk = fused_matmul(
        xn, w_in, layer, 0, qk_width, _ep_qk_norm_rope, out_dtype=BF16, tm=tm, tn=tn,
        extras=[(gain, (1, tn), lambda j, i: (0, j)),
                (gain_swapped, (1, tn), lambda j, i: (0, j)),
                (cos, (tm, A_HEAD_DIM), lambda j, i: (i, 0)),
                (sin, (tm, A_HEAD_DIM), lambda j, i: (i, 0))],
        name="attn_qk_proj")
    v = fused_matmul(xn, w_in, layer, qk_width, v_width, _ep_identity, out_dtype=BF16,
                     tm=tm, tn=tn, name="attn_v_proj")
    outs, lses = [], []
    for g, (_, dilation) in enumerate(A_PATTERNS):
        o, lse = dilated_attention(qk, v, g, dilation)
        outs.append(o)
        lses.append(lse)
    merged = attention_merge(outs, lses)
    return matmul_residual(merged, w_out, layer, h, tm=1024, name="attn_out_proj")


def _ep_gates(acc, cs, bias_ref):
    lane = lax.broadcasted_iota(jnp.int32, acc.shape, 1)
    g = jnp.where(lane < 2 * B_HEADS, acc + bias_ref[...], 0.0)
    return jnp.where(lane >= B_HEADS, jax.nn.log_sigmoid(g), g)


def _mlstm_kernel(q_ref, k_ref, v_ref, og_ref, gcol_ref, grow_ref, gain_ref, o_ref,
                  c_ref, n_ref, m_ref):
    @pl.when(pl.program_id(1) == 0)
    def _():
        c_ref[...] = jnp.zeros_like(c_ref)
        n_ref[...] = jnp.zeros_like(n_ref)
        m_ref[...] = jnp.zeros_like(m_ref)

    for hh in range(B_HEADS_PER_STEP):
        qs = slice(hh * B_QK_DIM, (hh + 1) * B_QK_DIM)
        vs = slice(hh * B_V_DIM, (hh + 1) * B_V_DIM)
        _mlstm_head(pl.program_id(0) * B_HEADS_PER_STEP + hh,
                    q_ref.at[:, qs], k_ref.at[:, qs], v_ref.at[:, vs], og_ref.at[:, vs], gcol_ref, grow_ref,
                    gain_ref.at[:, vs], o_ref.at[:, vs], c_ref.at[hh], n_ref.at[hh], m_ref.at[hh])


def _mlstm_head(hd, q_ref, k_ref, v_ref, og_ref, gcol_ref, grow_ref, gain_ref, o_ref, c_ref, n_ref, m_ref):
    L = q_ref.shape[0]
    gcol = gcol_ref[...]
    lane = lax.broadcasted_iota(jnp.int32, gcol.shape, 1)
    ig_col = jnp.sum(jnp.where(lane == hd, gcol, 0.0), axis=-1, keepdims=True)
    lf_col = jnp.sum(jnp.where(lane == hd + B_HEADS, gcol, 0.0), axis=-1, keepdims=True)
    ig_row = grow_ref[pl.ds(hd, 1), :]
    lf_row = grow_ref[pl.ds(hd + B_HEADS, 1), :]

    t_i = lax.broadcasted_iota(jnp.int32, (L, L), 0)
    s_i = lax.broadcasted_iota(jnp.int32, (L, L), 1)
    causal = s_i <= t_i
    lower = causal.astype(BF16)
    upper = (t_i <= s_i).astype(BF16)

    def split3(a):
        hi = a.astype(BF16).astype(F32)
        r1 = a - hi
        mid = r1.astype(BF16).astype(F32)
        return hi, mid, (r1 - mid).astype(BF16).astype(F32)

    hi, mid, lo = split3(lf_col)
    parts_col = jnp.where(lane == 0, hi, jnp.where(lane == 1, mid, jnp.where(lane == 2, lo, 0.0))).astype(BF16)
    cum = jnp.dot(lower, parts_col, preferred_element_type=F32)
    b_col = cum[:, 0:1] + cum[:, 1:2] + cum[:, 2:3]
    hi, mid, lo = split3(lf_row)
    sub = lax.broadcasted_iota(jnp.int32, (8, L), 0)
    parts_row = jnp.where(sub == 0, hi, jnp.where(sub == 1, mid, jnp.where(sub == 2, lo, 0.0))).astype(BF16)
    cum = jnp.dot(parts_row, upper, preferred_element_type=F32)
    b_row = cum[0:1, :] + cum[1:2, :] + cum[2:3, :]

    m_prev = m_ref[:1, :1]
    c_prev = c_ref[...]
    n_prev = n_ref[...]

    q = q_ref[...] * jnp.asarray(B_QK_DIM ** -0.5, q_ref.dtype)
    k = k_ref[...]
    v = v_ref[...]

    log_d = jnp.where(causal, b_col - b_row + ig_row, -jnp.inf)
    log_inter = b_col + m_prev
    m_t = jnp.maximum(log_inter, jnp.max(log_d, axis=-1, keepdims=True))
    scores = lax.dot_general(q, k, (((1,), (1,)), ((), ())), preferred_element_type=F32)
    w = scores * jnp.exp(log_d - m_t)
    decay = jnp.exp(log_inter - m_t)
    num = (decay * jnp.dot(q, c_prev.astype(BF16), preferred_element_type=F32)
           + jnp.dot(w.astype(BF16), v, preferred_element_type=F32))
    qn = (decay * jnp.sum(q.astype(F32) * n_prev, axis=-1, keepdims=True)
          + jnp.sum(w, axis=-1, keepdims=True))
    hval = num / jnp.maximum(jnp.abs(qn), jnp.exp(-m_t))

    inv = lax.rsqrt(jnp.mean(hval * hval, axis=-1, keepdims=True) + RMS_EPS)
    o_ref[...] = (hval * inv * gain_ref[...] * _sigmoid(og_ref[...].astype(F32))).astype(o_ref.dtype)

    b_last = b_col[L - 1:L, :]
    log_w = b_last - b_col + ig_col
    m_new = jnp.maximum(b_last + m_prev, jnp.max(log_w, axis=0, keepdims=True))
    w_s = jnp.exp(log_w - m_new)
    carry = jnp.exp(b_last + m_prev - m_new)
    k_w = k.astype(F32) * w_s
    c_ref[...] = carry * c_prev + lax.dot_general(
        k_w.astype(BF16), v, (((0,), (0,)), ((), ())), preferred_element_type=F32)
    n_ref[...] = carry * n_prev + jnp.sum(k_w, axis=0, keepdims=True)
    m_ref[...] = jnp.broadcast_to(m_new, m_ref.shape)


def mlstm_mixer(h, hb, norm_gain, w_in, layer, gate_bias, h_gain, w_out):
    seq, d = h.shape
    qk_w = B_HEADS * B_QK_DIM
    v_w = B_HEADS * B_V_DIM
    main_w = 2 * qk_w + v_w + d
    proj = fused_matmul(hb, w_in, layer, 0, main_w, _ep_identity, out_dtype=BF16, norm_gain=norm_gain,
                        name="mlstm_in_proj")
    n_gate = 2 * B_HEADS
    assert w_in.shape[2] == main_w + n_gate
    bias = jnp.pad(gate_bias, (0, B_GATE_PAD - n_gate)).reshape(1, B_GATE_PAD)
    gates = fused_matmul(hb, w_in, layer, main_w, B_GATE_PAD, _ep_gates, out_dtype=F32, tn=B_GATE_PAD,
                         norm_gain=norm_gain, extras=[(bias, (1, B_GATE_PAD), lambda j, i: (0, 0))],
                         name="mlstm_gate_proj")
    gates_t = gates[:, :n_gate].T
    L = min(B_CHUNK, seq)
    hps = B_HEADS_PER_STEP
    qk_blk, v_blk = hps * B_QK_DIM, hps * B_V_DIM
    nq = qk_w // qk_blk
    nv = v_w // v_blk
    mixed = pl.pallas_call(
        _mlstm_kernel,
        out_shape=jax.ShapeDtypeStruct((seq, v_w), BF16),
        grid=(B_HEADS // hps, seq // L),
        in_specs=[pl.BlockSpec((L, qk_blk), lambda g, c: (c, g)),
                  pl.BlockSpec((L, qk_blk), lambda g, c: (c, nq + g)),
                  pl.BlockSpec((L, v_blk), lambda g, c: (c, (2 * qk_w) // v_blk + g)),
                  pl.BlockSpec((L, v_blk), lambda g, c: (c, (2 * qk_w) // v_blk + nv + g)),
                  pl.BlockSpec((L, B_GATE_PAD), lambda g, c: (c, 0)),
                  pl.BlockSpec((n_gate, L), lambda g, c: (0, c)),
                  pl.BlockSpec((1, v_blk), lambda g, c: (0, g))],
        out_specs=pl.BlockSpec((L, v_blk), lambda g, c: (c, g)),
        scratch_shapes=[pltpu.VMEM((hps, B_QK_DIM, B_V_DIM), F32),
                        pltpu.VMEM((hps, 1, B_QK_DIM), F32),
                        pltpu.VMEM((hps, 8, LANES), F32)],
        compiler_params=_params(2),
        name="mlstm_chunks",
    )(proj, proj, proj, proj, gates, gates_t, h_gain.reshape(1, v_w))
    return matmul_residual(mixed, w_out, layer, h, name="mlstm_out_proj")


def _pool_kernel(u_ref, halo_ref, wg_ref, scale_ref, o_ref, wb_ref):
    g = pl.program_id(0)
    i = pl.program_id(1)
    tm = u_ref.shape[0]

    @pl.when(i == 0)
    def _():
        wb_ref[...] = wg_ref[...].astype(BF16)

    win = jnp.left_shift(jnp.int32(C_WINDOWS[0]), g)
    t_i = lax.broadcasted_iota(jnp.int32, (tm, tm), 0)
    s_i = lax.broadcasted_iota(jnp.int32, (tm, tm), 1)
    dist = t_i - s_i
    band = ((dist >= 0) & (dist < win)).astype(BF16)
    t_h = lax.broadcasted_iota(jnp.int32, (tm, C_HALO), 0)
    s_h = lax.broadcasted_iota(jnp.int32, (tm, C_HALO), 1)
    band_halo = ((t_h + C_HALO - s_h < win) & (i > 0)).astype(BF16)
    u = u_ref[...]
    total = (jnp.dot(band, u, preferred_element_type=F32)
             + jnp.dot(band_halo, halo_ref[...], preferred_element_type=F32))
    t_glob = i * tm + lax.broadcasted_iota(jnp.int32, (tm, 1), 0)
    count = jnp.minimum(t_glob + 1, win).astype(F32)
    pooled = total / count - u.astype(F32)
    y = jnp.dot(pooled.astype(BF16), wb_ref[...], preferred_element_type=F32) * scale_ref[...]
    o_ref[...] = y.astype(o_ref.dtype)


def pooling_mixer(h, hb, norm_gain, w_in, layer, w_group, scale, w_out, tm=512):
    seq, d = h.shape
    assert all(C_WINDOWS[g] == C_WINDOWS[0] << g for g in range(len(C_WINDOWS)))
    assert max(C_WINDOWS) <= C_HALO
    u = fused_matmul(hb, w_in, layer, 0, d, _ep_identity, out_dtype=BF16, norm_gain=norm_gain,
                     name="pool_in_proj")
    tm = min(tm, seq)
    gd = C_GROUP_DIM
    halo_per_tile = tm // C_HALO
    y = pl.pallas_call(
        _pool_kernel,
        out_shape=jax.ShapeDtypeStruct((seq, d), BF16),
        grid=(len(C_WINDOWS), seq // tm),
        in_specs=[pl.BlockSpec((tm, gd), lambda g, i: (i, g)),
                  pl.BlockSpec((C_HALO, gd), lambda g, i: (jnp.maximum(i * halo_per_tile - 1, 0), g)),
                  pl.BlockSpec((None, None, gd, gd), lambda g, i: (layer, g, 0, 0)),
                  pl.BlockSpec((1, gd), lambda g, i: (0, g))],
        out_specs=pl.BlockSpec((tm, gd), lambda g, i: (i, g)),
        scratch_shapes=[pltpu.VMEM((gd, gd), BF16)],
        compiler_params=_params(2),
        name="pool_group",
    )(u, u, w_group, scale.reshape(1, d))
    return matmul_residual(y, w_out, layer, h, name="pool_out_proj")


def squared_relu_mlp(h, hb, norm_gain, w_in, w_out, layer):
    hidden, w_out_bf16 = fused_matmul(hb, w_in, layer, 0, w_in.shape[2], _ep_relu_sq, out_dtype=BF16,
                                      norm_gain=norm_gain, side=(w_out, layer), name="mlp_up")
    return matmul_kgrid_residual(hidden, w_out_bf16, h, name="mlp_down")


def kernel(x, p, norm_mix, norm_mlp, norm_ple, w_ple, w_ple_gate, w_mlp_in, w_mlp_out, a_w_in, a_q_norm, a_k_norm, a_w_out, b_w_in, b_gate_bias, b_h_norm, b_w_out, c_w_in, c_w_group, c_scale, c_w_out):
    bsz, seq, d = x.shape
    depth = p.shape[0]
    rope = rope_tables(seq)
    outs = []
    for b in range(bsz):
        h = x.reshape(seq, d) if bsz == 1 else x[b]
        hb = None
        for i in range(depth):
            kind, j = i % 3, i // 3
            if kind == 0:
                h, hb = dilated_attention_mixer(h, norm_mix[i], a_w_in, j, a_q_norm[j], a_k_norm[j], a_w_out, rope)
            elif kind == 1:
                h, hb = mlstm_mixer(h, hb, norm_mix[i], b_w_in, j, b_gate_bias[j], b_h_norm[j], b_w_out)
            else:
                h, hb = pooling_mixer(h, hb, norm_mix[i], c_w_in, j, c_w_group, c_scale[j], c_w_out)
            h, hb = squared_relu_mlp(h, hb, norm_mlp[i], w_mlp_in, w_mlp_out, i)
            h, hb = ple_update(h, hb, norm_ple[i], w_ple_gate, i,
                               p[i, b].astype(BF16), w_ple[i].astype(BF16))
        outs.append(h)
    return outs[0][None] if bsz == 1 else jnp.stack(outs, axis=0)
```

```python
import functools
import math

import jax
import jax.numpy as jnp
from jax import lax
from jax.experimental import pallas as pl
from jax.experimental.pallas import tpu as pltpu

F32 = jnp.float32
BF16 = jnp.bfloat16

RMS_EPS = 1e-6
ROPE_THETA = 10000.0

A_HEAD_DIM = 128
A_HEADS_PER_GROUP = 16
A_PATTERNS = ((128, 1), (512, 4), (2048, 16))
A_STEPS = 128
A_RESIDUES = 16
A_GROUP_WIDTH = A_HEADS_PER_GROUP * A_HEAD_DIM

B_HEADS = 8
B_QK_DIM = 256
B_V_DIM = 512
B_CHUNK = 256
B_HEADS_PER_STEP = 4
B_GATE_PAD = 128

C_WINDOWS = (2, 4, 8, 16)
C_GROUP_DIM = 1024
C_HALO = 128

LANES = 128
VMEM_LIMIT_BYTES = 60 * 1024 * 1024


def _sigmoid(x):
    return 0.5 * jnp.tanh(0.5 * x) + 0.5


def _params(n_grid_axes):
    return pltpu.CompilerParams(
        dimension_semantics=("arbitrary",) * n_grid_axes,
        vmem_limit_bytes=VMEM_LIMIT_BYTES)


def _rmsnorm_rows(x, gain):
    inv = lax.rsqrt(jnp.mean(x * x, axis=-1, keepdims=True) + RMS_EPS)
    return x * inv * gain


def _row_permutation(tm, rows, to_residue_major):
    a = lax.broadcasted_iota(jnp.int32, (tm, tm), 0)
    b = lax.broadcasted_iota(jnp.int32, (tm, tm), 1)
    major, seq_order = (a, b) if to_residue_major else (b, a)
    return (seq_order == (major % rows) * A_RESIDUES + major // rows).astype(BF16)


def _rmsnorm_residue_major_kernel(x_ref, g_ref, o_ref):
    tm = x_ref.shape[0]
    rows = tm // A_RESIDUES
    y = _rmsnorm_rows(x_ref[...], g_ref[...]).astype(BF16)
    y = jnp.dot(_row_permutation(tm, rows, True), y, preferred_element_type=F32)
    o_ref[...] = y.astype(o_ref.dtype).reshape(o_ref.shape)


def rmsnorm_residue_major(h, gain, tm=256):
    m, d = h.shape
    tm = min(tm, m)
    in_specs = [pl.BlockSpec((tm, d), lambda i: (i, 0)),
                pl.BlockSpec((1, d), lambda i: (0, 0))]
    out = pl.pallas_call(
        _rmsnorm_residue_major_kernel,
        out_shape=jax.ShapeDtypeStruct((A_RESIDUES, m // A_RESIDUES, d), BF16),
        grid=(m // tm,), in_specs=in_specs,
        out_specs=pl.BlockSpec((A_RESIDUES, tm // A_RESIDUES, d), lambda i: (0, i, 0)),
        compiler_params=_params(1), name="rmsnorm_residue_major",
    )(h, gain.reshape(1, d))
    return out.reshape(m, d)


SIDE_CAST_ROWS = 64
NORM_ROWS = 64


def _row_inv_rms(x_ref):
    tm, k = x_ref.shape
    rows = min(NORM_ROWS, tm)
    pieces = []
    for r in range(tm // rows):
        part = None
        for c in range(k // LANES):
            xc = x_ref[r * rows:(r + 1) * rows, c * LANES:(c + 1) * LANES].astype(F32)
            part = xc * xc if part is None else part + xc * xc
        pieces.append(jnp.sum(part, axis=-1, keepdims=True))
    return lax.rsqrt(jnp.concatenate(pieces, axis=0) * (1.0 / k) + RMS_EPS)


def _mm_kernel(x_ref, wchunk_ref, *rest, epilogue, n_extra, n_out, n_slabs, n_side_steps, normed):
    if normed:
        gcol_ref, rest = rest[0], rest[1:]
    extra_refs, rest = rest[:n_extra], rest[n_extra:]
    if n_side_steps:
        side_ref, rest = rest[0], rest[1:]
    o_refs, rest = rest[:n_out], rest[n_out:]
    if n_side_steps:
        side_out_ref, rest = rest[0], rest[1:]
    if normed:
        wb_refs, inv_ref = rest[:2], rest[2]
    else:
        wb_refs = rest
    jj, i = pl.program_id(0), pl.program_id(1)
    chunk = wchunk_ref.shape[0]

    def cast_jobs(dst_ref):
        row0 = pl.multiple_of(i * chunk, chunk)
        wc = wchunk_ref[...]
        if normed:
            wc = wc * gcol_ref[...]
        dst_ref[pl.ds(row0, chunk), :] = wc.astype(BF16)
        if n_side_steps:
            side_out_ref[...] = side_ref[...].astype(BF16)

    @pl.when(jj == 0)
    def _():
        cast_jobs(wb_refs[0])

    if normed:
        tile_lane = lax.broadcasted_iota(jnp.int32, inv_ref.shape, 1) == i

        @pl.when((jj == 0) & (i == 0))
        def _():
            inv_ref[...] = jnp.zeros_like(inv_ref)

        @pl.when(jj == 1)
        def _():
            inv_ref[...] = jnp.where(tile_lane, _row_inv_rms(x_ref), inv_ref[...])

    def multiply(cur_ref, next_ref):
        cast_jobs(next_ref)
        acc = jnp.dot(x_ref[...], cur_ref[...], preferred_element_type=F32)
        if normed:
            acc = acc * jnp.sum(jnp.where(tile_lane, inv_ref[...], 0.0), axis=-1, keepdims=True)
        vals = epilogue(acc, slice(None), *extra_refs)
        vals = vals if isinstance(vals, tuple) else (vals,)
        for o_ref, val in zip(o_refs, vals, strict=True):
            o_ref[...] = val.astype(o_ref.dtype)

    @pl.when(jj % 2 == 1)
    def _():
        multiply(wb_refs[0], wb_refs[1])

    @pl.when((jj > 0) & (jj % 2 == 0))
    def _():
        multiply(wb_refs[1], wb_refs[0])


def fused_matmul(x, w, layer, col0, n, epilogue, *, out_dtype, tm=1024, tn=1024, extras=(), side=None,
                 norm_gain=None, name):
    m, k = x.shape
    tm, tn = min(tm, m), min(tn, n)
    n_slabs, n_rows = n // tn, m // tm
    assert w.shape[1] == k and m % tm == 0 and n % tn == 0 and col0 % tn == 0 and k % n_rows == 0, (
        w.shape, m, n, tm, tn, col0)
    jb = col0 // tn
    chunk = k // n_rows
    out_dtypes = out_dtype if isinstance(out_dtype, tuple) else (out_dtype,)

    def lag(im):
        return lambda jj, i: im(jnp.maximum(jj - 1, 0), jnp.where(jj > 0, i, 0))

    chunk_row = lambda jj, i: jnp.where(jj < n_slabs, i, n_rows - 1)
    in_specs = [pl.BlockSpec((tm, k), lag(lambda j, i: (i, 0))),
                pl.BlockSpec((None, chunk, tn),
                             lambda jj, i: (layer, chunk_row(jj, i), jb + jnp.minimum(jj, n_slabs - 1)))]
    operands = [x, w]
    if norm_gain is not None:
        assert n_rows <= LANES
        in_specs.append(pl.BlockSpec((chunk, 1), lambda jj, i: (chunk_row(jj, i), 0)))
        operands.append(norm_gain.reshape(k, 1))
    in_specs += [pl.BlockSpec(bs, lag(im)) for _, bs, im in extras]
    operands += [a for a, _, _ in extras]
    out_shape = [jax.ShapeDtypeStruct((m, n), dt) for dt in out_dtypes]
    out_specs = [pl.BlockSpec((tm, tn), lag(lambda j, i: (i, j))) for _ in out_dtypes]
    n_side_steps = 0
    if side is not None:
        side_w, side_layer = side
        side_rows, side_cols = side_w.shape[1:]
        rows_per_step = SIDE_CAST_ROWS
        while side_rows // rows_per_step > (n_slabs + 1) * n_rows:
            rows_per_step *= 2
        n_side_steps = side_rows // rows_per_step
        assert side_rows % rows_per_step == 0
        side_block = lambda jj, i: jnp.minimum(jj * n_rows + i, n_side_steps - 1)
        in_specs.append(pl.BlockSpec((None, rows_per_step, side_cols),
                                     lambda jj, i: (side_layer, side_block(jj, i), 0)))
        operands.append(side_w)
        out_shape.append(jax.ShapeDtypeStruct((side_rows, side_cols), BF16))
        out_specs.append(pl.BlockSpec((rows_per_step, side_cols), lambda jj, i: (side_block(jj, i), 0)))
    outs = pl.pallas_call(
        functools.partial(_mm_kernel, epilogue=epilogue, n_extra=len(extras), n_out=len(out_dtypes),
                          n_slabs=n_slabs, n_side_steps=n_side_steps, normed=norm_gain is not None),
        out_shape=out_shape,
        grid=(n_slabs + 1, n_rows),
        in_specs=in_specs,
        out_specs=out_specs,
        scratch_shapes=[pltpu.VMEM((k, tn), BF16), pltpu.VMEM((k, tn), BF16)] + (
            [pltpu.VMEM((tm, LANES), F32)] if norm_gain is not None else []),
        compiler_params=_params(2),
        name=name,
    )(*operands)
    return outs[0] if len(outs) == 1 else tuple(outs)


def _ep_identity(acc, cs):
    return acc


def _ep_relu_sq(acc, cs):
    r = jnp.maximum(acc, 0.0)
    return r * r


def _ep_residual(acc, cs, res_ref):
    h = res_ref[:, cs] + acc
    return h, h


def matmul_residual(x, w, layer, res, *, tm=512, tn=1024, name):
    n = w.shape[2]
    tm, tn = min(tm, x.shape[0]), min(tn, n)
    return fused_matmul(x, w, layer, 0, n, _ep_residual, out_dtype=(F32, BF16), tm=tm, tn=tn,
                        extras=[(res, (tm, tn), lambda j, i: (i, j))], name=name)


def _mm_kgrid_kernel(x_ref, w_ref, res_ref, o_ref, ob_ref, acc_ref):
    kk = pl.program_id(2)
    band = res_ref.shape[0]

    @pl.when(kk == 0)
    def _():
        acc_ref[...] = jnp.zeros_like(acc_ref)

    acc_ref[...] += jnp.dot(x_ref[...], w_ref[...], preferred_element_type=F32)
    row0 = pl.multiple_of(kk * band, band)
    acc_ref[pl.ds(row0, band), :] += res_ref[...]

    @pl.when(kk == pl.num_programs(2) - 1)
    def _():
        h = acc_ref[...]
        o_ref[...] = h
        ob_ref[...] = h.astype(ob_ref.dtype)


def matmul_kgrid_residual(x, w, res, *, tm=1024, tn=1024, tk=4096, name):
    m, k = x.shape
    n = w.shape[1]
    tm, tn, tk = min(tm, m), min(tn, n), min(tk, k)
    nk = k // tk
    assert m % tm == 0 and n % tn == 0 and k % tk == 0 and tm % (8 * nk) == 0
    return pl.pallas_call(
        _mm_kgrid_kernel,
        out_shape=(jax.ShapeDtypeStruct((m, n), F32), jax.ShapeDtypeStruct((m, n), BF16)),
        grid=(m // tm, n // tn, k // tk),
        in_specs=[pl.BlockSpec((tm, tk), lambda i, j, kk: (i, kk)),
                  pl.BlockSpec((tk, tn), lambda i, j, kk: (kk, j)),
                  pl.BlockSpec((tm // nk, tn), lambda i, j, kk: (i * nk + kk, j))],
        out_specs=(pl.BlockSpec((tm, tn), lambda i, j, kk: (i, j)),
                   pl.BlockSpec((tm, tn), lambda i, j, kk: (i, j))),
        scratch_shapes=[pltpu.VMEM((tm, tn), F32)],
        compiler_params=_params(3),
        name=name,
    )(x, w, res)


def _ep_ple(acc, cs, p_ref, wp_ref, res_ref):
    emb = jnp.dot(p_ref[...], wp_ref[:, cs], preferred_element_type=F32)
    h = res_ref[:, cs] + _sigmoid(acc) * emb
    return h, h


def ple_update(h, hb, norm_gain, w_gate, layer, p, w_ple, *, tm=512, tn=1024):
    n = w_gate.shape[2]
    tm, tn = min(tm, h.shape[0]), min(tn, n)
    kp = p.shape[1]
    return fused_matmul(
        hb, w_gate, layer, 0, n, _ep_ple, out_dtype=(F32, BF16), tm=tm, tn=tn, norm_gain=norm_gain,
        extras=[(p, (tm, kp), lambda j, i: (i, 0)),
                (w_ple, (kp, tn), lambda j, i: (0, j)),
                (h, (tm, tn), lambda j, i: (i, j))],
        name="ple_gate")


def _rope_table_kernel(cos_ref, sin_ref, *, rows_per_residue):
    tm = cos_ref.shape[0]
    half = A_HEAD_DIM // 2
    row0 = pl.program_id(0) * tm
    residue = row0 // rows_per_residue
    n0 = row0 % rows_per_residue
    n = n0 + lax.broadcasted_iota(jnp.int32, (tm, A_HEAD_DIM), 0)
    pos = (n * A_RESIDUES + residue).astype(F32)
    lane = lax.broadcasted_iota(jnp.int32, (tm, A_HEAD_DIM), 1)
    j = jnp.where(lane >= half, lane - half, lane).astype(F32)
    inv_freq = jnp.exp(j * (-2.0 * math.log(ROPE_THETA) / A_HEAD_DIM))
    ang = pos * inv_freq
    cos_ref[...] = jnp.cos(ang)
    sin_ref[...] = jnp.where(lane >= half, 1.0, -1.0) * jnp.sin(ang)


def rope_tables(seq, tm=256):
    rows_per_residue = seq // A_RESIDUES
    tm = min(tm, rows_per_residue)
    assert rows_per_residue % tm == 0
    spec = pl.BlockSpec((tm, A_HEAD_DIM), lambda i: (i, 0))
    shape = jax.ShapeDtypeStruct((seq, A_HEAD_DIM), F32)
    return pl.pallas_call(
        functools.partial(_rope_table_kernel, rows_per_residue=rows_per_residue),
        out_shape=(shape, shape), grid=(seq // tm,),
        in_specs=[], out_specs=(spec, spec), compiler_params=_params(1),
        name="rope_tables")()


def _ep_qk_norm_rope(acc, cs, gain_ref, gain_swapped_ref, cos_ref, sin_ref):
    d = A_HEAD_DIM
    cos_g = cos_ref[...] * gain_ref[:, :d]
    sin_g = sin_ref[...] * gain_swapped_ref[:, :d]
    mean_mat = jnp.full((d, d), 1.0 / d, BF16)
    r_i = lax.broadcasted_iota(jnp.int32, (d, d), 0)
    c_i = lax.broadcasted_iota(jnp.int32, (d, d), 1)
    swap_halves = (r_i == (c_i + d // 2) % d).astype(BF16)
    outs = []
    for hh in range(acc.shape[1] // d):
        a = acc[:, hh * d:(hh + 1) * d]
        mean_sq = jnp.dot((a * a).astype(BF16), mean_mat, preferred_element_type=F32)
        a_swapped = jnp.dot(a.astype(BF16), swap_halves, preferred_element_type=F32)
        outs.append((a * cos_g + a_swapped * sin_g) * lax.rsqrt(mean_sq + RMS_EPS))
    return jnp.concatenate(outs, axis=1)


def _attn_kernel(q_ref, kc_ref, vc_ref, o_ref, lse_ref, kp_ref, vp_ref, *, parts):
    rows = q_ref.shape[-2]
    qb = parts * rows
    n = pl.program_id(1)

    @pl.when(n == 0)
    def _():
        kp_ref[...] = jnp.zeros_like(kp_ref)
        vp_ref[...] = jnp.zeros_like(vp_ref)

    def load(ref, hs):
        return ref[..., hs].reshape(qb, A_HEAD_DIM)

    def step_of(idx):
        return idx if parts == 1 else (idx % rows) * parts + idx // rows

    lq = step_of(lax.broadcasted_iota(jnp.int32, (qb, 2 * qb), 0))
    col = lax.broadcasted_iota(jnp.int32, (qb, 2 * qb), 1)
    is_prev = col < qb
    lk = step_of(jnp.where(is_prev, col, col - qb))
    dist = lq - lk + jnp.where(is_prev, qb, 0)
    ok = (dist >= 0) & (dist <= A_STEPS) & ((n > 0) | jnp.logical_not(is_prev))
    lane = lax.broadcasted_iota(jnp.int32, (qb, LANES), 1)
    scale = A_HEAD_DIM ** -0.5
    nt = (((1,), (1,)), ((), ()))
    ones = jnp.ones((2 * qb, A_HEAD_DIM), BF16)
    lse_tile = jnp.zeros((qb, LANES), F32)
    for hh in range(A_HEADS_PER_GROUP):
        hs = slice(hh * A_HEAD_DIM, (hh + 1) * A_HEAD_DIM)
        q = load(q_ref, hs)
        k_cat = jnp.concatenate([load(kp_ref, hs), load(kc_ref, hs)], axis=0)
        v_cat = jnp.concatenate([load(vp_ref, hs), load(vc_ref, hs)], axis=0)
        s = lax.dot_general(q, k_cat, nt, preferred_element_type=F32) * scale
        s = jnp.where(ok, s, -jnp.inf)
        mx = jnp.max(jnp.maximum(s[:, :qb], s[:, qb:]), axis=-1, keepdims=True)
        e = jnp.exp(s - mx).astype(BF16)
        pv = jnp.dot(e, jnp.concatenate([v_cat, ones], axis=1), preferred_element_type=F32)
        den = pv[:, A_HEAD_DIM:]
        o_ref[..., hs] = (pv[:, :A_HEAD_DIM] / den).reshape(o_ref.shape[:-1] + (A_HEAD_DIM,)).astype(o_ref.dtype)
        lse_tile = jnp.where(lane == hh, mx + jnp.log(den), lse_tile)
    lse_ref[...] = lse_tile.reshape(lse_ref.shape)
    kp_ref[...] = kc_ref[...]
    vp_ref[...] = vc_ref[...]


def dilated_attention(qk, v, group, dilation):
    seq = qk.shape[0]
    gw = A_GROUP_WIDTH
    n_groups = len(A_PATTERNS)
    parts = A_RESIDUES // dilation
    rpr = seq // A_RESIDUES
    rows = max(A_STEPS // parts, 16)
    nb = rpr // rows
    qk_w, v_w = qk.shape[1], v.shape[1]
    if parts == 1:
        vshape = lambda w: (seq, w)
        blk = lambda w: (rows, w)
        at = lambda r, n, c: (r * nb + n, c)
    else:
        vshape = lambda w: (parts, dilation, rpr, w)
        blk = lambda w: (parts, None, rows, w)
        at = lambda r, n, c: (0, r, n, c)
    view = lambda a: a.reshape(vshape(a.shape[1]))
    o, lse = pl.pallas_call(
        functools.partial(_attn_kernel, parts=parts),
        out_shape=(jax.ShapeDtypeStruct(vshape(gw), BF16),
                   jax.ShapeDtypeStruct(vshape(LANES), F32)),
        grid=(dilation, nb),
        in_specs=[pl.BlockSpec(blk(gw), lambda r, n: at(r, n, group)),
                  pl.BlockSpec(blk(gw), lambda r, n: at(r, n, n_groups + group)),
                  pl.BlockSpec(blk(gw), lambda r, n: at(r, n, group))],
        out_specs=(pl.BlockSpec(blk(gw), lambda r, n: at(r, n, 0)),
                   pl.BlockSpec(blk(LANES), lambda r, n: at(r, n, 0))),
        scratch_shapes=[pltpu.VMEM(tuple(d for d in blk(gw) if d is not None), BF16)] * 2,
        compiler_params=_params(2),
        name=f"dilated_attn_g{group}",
    )(view(qk), view(qk), view(v))
    return o.reshape(seq, gw), lse.reshape(seq, LANES)


def _attn_merge_kernel(o0_ref, o1_ref, o2_ref, l0_ref, l1_ref, l2_ref, out_ref):
    rows = o0_ref.shape[1]
    tm = A_RESIDUES * rows

    def flat(ref, sl=slice(None)):
        return ref[:, :, sl].reshape(tm, -1)

    l0, l1, l2 = flat(l0_ref), flat(l1_ref), flat(l2_ref)
    mx = jnp.maximum(jnp.maximum(l0, l1), l2)
    w0, w1, w2 = jnp.exp(l0 - mx), jnp.exp(l1 - mx), jnp.exp(l2 - mx)
    tot = w0 + w1 + w2
    w0, w1, w2 = w0 / tot, w1 / tot, w2 / tot
    to_sequence_order = _row_permutation(tm, rows, False)
    for hh in range(A_HEADS_PER_GROUP):
        hs = slice(hh * A_HEAD_DIM, (hh + 1) * A_HEAD_DIM)
        merged = (w0[:, hh:hh + 1] * flat(o0_ref, hs).astype(F32)
                  + w1[:, hh:hh + 1] * flat(o1_ref, hs).astype(F32)
                  + w2[:, hh:hh + 1] * flat(o2_ref, hs).astype(F32)).astype(BF16)
        out_ref[:, hs] = jnp.dot(to_sequence_order, merged, preferred_element_type=F32).astype(out_ref.dtype)


def attention_merge(outs, lses, rows=16):
    seq, gw = outs[0].shape
    rpr = seq // A_RESIDUES
    rows = min(rows, rpr)
    tm = A_RESIDUES * rows
    o_spec = pl.BlockSpec((A_RESIDUES, rows, gw), lambda i: (0, i, 0))
    l_spec = pl.BlockSpec((A_RESIDUES, rows, LANES), lambda i: (0, i, 0))
    return pl.pallas_call(
        _attn_merge_kernel,
        out_shape=jax.ShapeDtypeStruct((seq, gw), BF16),
        grid=(rpr // rows,),
        in_specs=[o_spec] * 3 + [l_spec] * 3,
        out_specs=pl.BlockSpec((tm, gw), lambda i: (i, 0)),
        compiler_params=_params(1),
        name="attn_merge",
    )(*[o.reshape(A_RESIDUES, rpr, gw) for o in outs],
      *[l.reshape(A_RESIDUES, rpr, LANES) for l in lses])


def dilated_attention_mixer(h, norm_gain, w_in, layer, q_gain, k_gain, w_out, rope):
    seq = h.shape[0]
    n_heads = len(A_PATTERNS) * A_HEADS_PER_GROUP
    qk_width = 2 * n_heads * A_HEAD_DIM
    v_width = n_heads * A_HEAD_DIM
    assert seq % (A_RESIDUES * A_STEPS) == 0
    assert all(w // d == A_STEPS and A_RESIDUES % d == 0 for w, d in A_PATTERNS)
    xn = rmsnorm_residue_major(h, norm_gain)
    cos, sin = rope
    per_head = lambda g_q, g_k: jnp.concatenate([jnp.tile(g_q, n_heads), jnp.tile(g_k, n_heads)]).reshape(1, qk_width)
    gain = per_head(q_gain, k_gain)
    gain_swapped = per_head(jnp.roll(q_gain, A_HEAD_DIM // 2), jnp.roll(k_gain, A_HEAD_DIM // 2))
    tm, tn = min(1024, seq), 1024
    assert (qk_width // 2) % tn == 0
    qk = fused_matmul(
        xn, w_in, layer, 0, qk_width, _ep_qk_norm_rope, out_dtype=BF16, tm=tm, tn=tn,
        extras=[(gain, (1, tn), lambda j, i: (0, j)),
                (gain_swapped, (1, tn), lambda j, i: (0, j)),
                (cos, (tm, A_HEAD_DIM), lambda j, i: (i, 0)),
                (sin, (tm, A_HEAD_DIM), lambda j, i: (i, 0))],
        name="attn_qk_proj")
    v = fused_matmul(xn, w_in, layer, qk_width, v_width, _ep_identity, out_dtype=BF16,
                     tm=tm, tn=tn, name="attn_v_proj")
    outs, lses = [], []
    for g, (_, dilation) in enumerate(A_PATTERNS):
        o, lse = dilated_attention(qk, v, g, dilation)
        outs.append(o)
        lses.append(lse)
    merged = attention_merge(outs, lses)
    return matmul_residual(merged, w_out, layer, h, tm=1024, name="attn_out_proj")


def _ep_gates(acc, cs, bias_ref):
    lane = lax.broadcasted_iota(jnp.int32, acc.shape, 1)
    g = jnp.where(lane < 2 * B_HEADS, acc + bias_ref[...], 0.0)
    return jnp.where(lane >= B_HEADS, jax.nn.log_sigmoid(g), g)


def _mlstm_kernel(q_ref, k_ref, v_ref, og_ref, gcol_ref, grow_ref, gain_ref, o_ref,
                  c_ref, n_ref, m_ref):
    @pl.when(pl.program_id(1) == 0)
    def _():
        c_ref[...] = jnp.zeros_like(c_ref)
        n_ref[...] = jnp.zeros_like(n_ref)
        m_ref[...] = jnp.zeros_like(m_ref)

    for hh in range(B_HEADS_PER_STEP):
        qs = slice(hh * B_QK_DIM, (hh + 1) * B_QK_DIM)
        vs = slice(hh * B_V_DIM, (hh + 1) * B_V_DIM)
        _mlstm_head(pl.program_id(0) * B_HEADS_PER_STEP + hh,
                    q_ref.at[:, qs], k_ref.at[:, qs], v_ref.at[:, vs], og_ref.at[:, vs], gcol_ref, grow_ref,
                    gain_ref.at[:, vs], o_ref.at[:, vs], c_ref.at[hh], n_ref.at[hh], m_ref.at[hh])


def _mlstm_head(hd, q_ref, k_ref, v_ref, og_ref, gcol_ref, grow_ref, gain_ref, o_ref, c_ref, n_ref, m_ref):
    L = q_ref.shape[0]
    gcol = gcol_ref[...]
    lane = lax.broadcasted_iota(jnp.int32, gcol.shape, 1)
    ig_col = jnp.sum(jnp.where(lane == hd, gcol, 0.0), axis=-1, keepdims=True)
    lf_col = jnp.sum(jnp.where(lane == hd + B_HEADS, gcol, 0.0), axis=-1, keepdims=True)
    ig_row = grow_ref[pl.ds(hd, 1), :]
    lf_row = grow_ref[pl.ds(hd + B_HEADS, 1), :]

    t_i = lax.broadcasted_iota(jnp.int32, (L, L), 0)
    s_i = lax.broadcasted_iota(jnp.int32, (L, L), 1)
    causal = s_i <= t_i
    lower = causal.astype(BF16)
    upper = (t_i <= s_i).astype(BF16)

    def split3(a):
        hi = a.astype(BF16).astype(F32)
        r1 = a - hi
        mid = r1.astype(BF16).astype(F32)
        return hi, mid, (r1 - mid).astype(BF16).astype(F32)

    hi, mid, lo = split3(lf_col)
    parts_col = jnp.where(lane == 0, hi, jnp.where(lane == 1, mid, jnp.where(lane == 2, lo, 0.0))).astype(BF16)
    cum = jnp.dot(lower, parts_col, preferred_element_type=F32)
    b_col = cum[:, 0:1] + cum[:, 1:2] + cum[:, 2:3]
    hi, mid, lo = split3(lf_row)
    sub = lax.broadcasted_iota(jnp.int32, (8, L), 0)
    parts_row = jnp.where(sub == 0, hi, jnp.where(sub == 1, mid, jnp.where(sub == 2, lo, 0.0))).astype(BF16)
    cum = jnp.dot(parts_row, upper, preferred_element_type=F32)
    b_row = cum[0:1, :] + cum[1:2, :] + cum[2:3, :]

    m_prev = m_ref[:1, :1]
    c_prev = c_ref[...]
    n_prev = n_ref[...]

    q = q_ref[...] * jnp.asarray(B_QK_DIM ** -0.5, q_ref.dtype)
    k = k_ref[...]
    v = v_ref[...]

    log_d = jnp.where(causal, b_col - b_row + ig_row, -jnp.inf)
    log_inter = b_col + m_prev
    m_t = jnp.maximum(log_inter, jnp.max(log_d, axis=-1, keepdims=True))
    scores = lax.dot_general(q, k, (((1,), (1,)), ((), ())), preferred_element_type=F32)
    w = scores * jnp.exp(log_d - m_t)
    decay = jnp.exp(log_inter - m_t)
    num = (decay * jnp.dot(q, c_prev.astype(BF16), preferred_element_type=F32)
           + jnp.dot(w.astype(BF16), v, preferred_element_type=F32))
    qn = (decay * jnp.sum(q.astype(F32) * n_prev, axis=-1, keepdims=True)
          + jnp.sum(w, axis=-1, keepdims=True))
    hval = num / jnp.maximum(jnp.abs(qn), jnp.exp(-m_t))

    inv = lax.rsqrt(jnp.mean(hval * hval, axis=-1, keepdims=True) + RMS_EPS)
    o_ref[...] = (hval * inv * gain_ref[...] * _sigmoid(og_ref[...].astype(F32))).astype(o_ref.dtype)

    b_last = b_col[L - 1:L, :]
    log_w = b_last - b_col + ig_col
    m_new = jnp.maximum(b_last + m_prev, jnp.max(log_w, axis=0, keepdims=True))
    w_s = jnp.exp(log_w - m_new)
    carry = jnp.exp(b_last + m_prev - m_new)
    k_w = k.astype(F32) * w_s
    c_ref[...] = carry * c_prev + lax.dot_general(
        k_w.astype(BF16), v, (((0,), (0,)), ((), ())), preferred_element_type=F32)
    n_ref[...] = carry * n_prev + jnp.sum(k_w, axis=0, keepdims=True)
    m_ref[...] = jnp.broadcast_to(m_new, m_ref.shape)


def mlstm_mixer(h, hb, norm_gain, w_in, layer, gate_bias, h_gain, w_out):
    seq, d = h.shape
    qk_w = B_HEADS * B_QK_DIM
    v_w = B_HEADS * B_V_DIM
    main_w = 2 * qk_w + v_w + d
    proj = fused_matmul(hb, w_in, layer, 0, main_w, _ep_identity, out_dtype=BF16, norm_gain=norm_gain,
                        name="mlstm_in_proj")
    n_gate = 2 * B_HEADS
    assert w_in.shape[2] == main_w + n_gate
    bias = jnp.pad(gate_bias, (0, B_GATE_PAD - n_gate)).reshape(1, B_GATE_PAD)
    gates = fused_matmul(hb, w_in, layer, main_w, B_GATE_PAD, _ep_gates, out_dtype=F32, tn=B_GATE_PAD,
                         norm_gain=norm_gain, extras=[(bias, (1, B_GATE_PAD), lambda j, i: (0, 0))],
                         name="mlstm_gate_proj")
    gates_t = gates[:, :n_gate].T
    L = min(B_CHUNK, seq)
    hps = B_HEADS_PER_STEP
    qk_blk, v_blk = hps * B_QK_DIM, hps * B_V_DIM
    nq = qk_w // qk_blk
    nv = v_w // v_blk
    mixed = pl.pallas_call(
        _mlstm_kernel,
        out_shape=jax.ShapeDtypeStruct((seq, v_w), BF16),
        grid=(B_HEADS // hps, seq // L),
        in_specs=[pl.BlockSpec((L, qk_blk), lambda g, c: (c, g)),
                  pl.BlockSpec((L, qk_blk), lambda g, c: (c, nq + g)),
                  pl.BlockSpec((L, v_blk), lambda g, c: (c, (2 * qk_w) // v_blk + g)),
                  pl.BlockSpec((L, v_blk), lambda g, c: (c, (2 * qk_w) // v_blk + nv + g)),
                  pl.BlockSpec((L, B_GATE_PAD), lambda g, c: (c, 0)),
                  pl.BlockSpec((n_gate, L), lambda g, c: (0, c)),
                  pl.BlockSpec((1, v_blk), lambda g, c: (0, g))],
        out_specs=pl.BlockSpec((L, v_blk), lambda g, c: (c, g)),
        scratch_shapes=[pltpu.VMEM((hps, B_QK_DIM, B_V_DIM), F32),
                        pltpu.VMEM((hps, 1, B_QK_DIM), F32),
                        pltpu.VMEM((hps, 8, LANES), F32)],
        compiler_params=_params(2),
        name="mlstm_chunks",
    )(proj, proj, proj, proj, gates, gates_t, h_gain.reshape(1, v_w))
    return matmul_residual(mixed, w_out, layer, h, name="mlstm_out_proj")


def _pool_kernel(u_ref, halo_ref, wg_ref, scale_ref, o_ref, wb_ref):
    g = pl.program_id(0)
    i = pl.program_id(1)
    tm = u_ref.shape[0]

    @pl.when(i == 0)
    def _():
        wb_ref[...] = wg_ref[...].astype(BF16)

    win = jnp.left_shift(jnp.int32(C_WINDOWS[0]), g)
    t_i = lax.broadcasted_iota(jnp.int32, (tm, tm), 0)
    s_i = lax.broadcasted_iota(jnp.int32, (tm, tm), 1)
    dist = t_i - s_i
    band = ((dist >= 0) & (dist < win)).astype(BF16)
    t_h = lax.broadcasted_iota(jnp.int32, (tm, C_HALO), 0)
    s_h = lax.broadcasted_iota(jnp.int32, (tm, C_HALO), 1)
    band_halo = ((t_h + C_HALO - s_h < win) & (i > 0)).astype(BF16)
    u = u_ref[...]
    total = (jnp.dot(band, u, preferred_element_type=F32)
             + jnp.dot(band_halo, halo_ref[...], preferred_element_type=F32))
    t_glob = i * tm + lax.broadcasted_iota(jnp.int32, (tm, 1), 0)
    count = jnp.minimum(t_glob + 1, win).astype(F32)
    pooled = total / count - u.astype(F32)
    y = jnp.dot(pooled.astype(BF16), wb_ref[...], preferred_element_type=F32) * scale_ref[...]
    o_ref[...] = y.astype(o_ref.dtype)


def pooling_mixer(h, hb, norm_gain, w_in, layer, w_group, scale, w_out, tm=512):
    seq, d = h.shape
    assert all(C_WINDOWS[g] == C_WINDOWS[0] << g for g in range(len(C_WINDOWS)))
    assert max(C_WINDOWS) <= C_HALO
    u = fused_matmul(hb, w_in, layer, 0, d, _ep_identity, out_dtype=BF16, norm_gain=norm_gain,
                     name="pool_in_proj")
    tm = min(tm, seq)
    gd = C_GROUP_DIM
    halo_per_tile = tm // C_HALO
    y = pl.pallas_call(
        _pool_kernel,
        out_shape=jax.ShapeDtypeStruct((seq, d), BF16),
        grid=(len(C_WINDOWS), seq // tm),
        in_specs=[pl.BlockSpec((tm, gd), lambda g, i: (i, g)),
                  pl.BlockSpec((C_HALO, gd), lambda g, i: (jnp.maximum(i * halo_per_tile - 1, 0), g)),
                  pl.BlockSpec((None, None, gd, gd), lambda g, i: (layer, g, 0, 0)),
                  pl.BlockSpec((1, gd), lambda g, i: (0, g))],
        out_specs=pl.BlockSpec((tm, gd), lambda g, i: (i, g)),
        scratch_shapes=[pltpu.VMEM((gd, gd), BF16)],
        compiler_params=_params(2),
        name="pool_group",
    )(u, u, w_group, scale.reshape(1, d))
    return matmul_residual(y, w_out, layer, h, name="pool_out_proj")


def squared_relu_mlp(h, hb, norm_gain, w_in, w_out, layer):
    hidden, w_out_bf16 = fused_matmul(hb, w_in, layer, 0, w_in.shape[2], _ep_relu_sq, out_dtype=BF16,
                                      norm_gain=norm_gain, side=(w_out, layer), name="mlp_up")
    return matmul_kgrid_residual(hidden, w_out_bf16, h, name="mlp_down")


def kernel(x, p, norm_mix, norm_mlp, norm_ple, w_ple, w_ple_gate, w_mlp_in, w_mlp_out, a_w_in, a_q_norm, a_k_norm, a_w_out, b_w_in, b_gate_bias, b_h_norm, b_w_out, c_w_in, c_w_group, c_scale, c_w_out):
    bsz, seq, d = x.shape
    depth = p.shape[0]
    rope = rope_tables(seq)
    outs = []
    for b in range(bsz):
        h = x.reshape(seq, d) if bsz == 1 else x[b]
        hb = None
        for i in range(depth):
            kind, j = i % 3, i // 3
            if kind == 0:
                h, hb = dilated_attention_mixer(h, norm_mix[i], a_w_in, j, a_q_norm[j], a_k_norm[j], a_w_out, rope)
            elif kind == 1:
                h, hb = mlstm_mixer(h, hb, norm_mix[i], b_w_in, j, b_gate_bias[j], b_h_norm[j], b_w_out)
            else:
                h, hb = pooling_mixer(h, hb, norm_mix[i], c_w_in, j, c_w_group, c_scale[j], c_w_out)
            h, hb = squared_relu_mlp(h, hb, norm_mlp[i], w_mlp_in, w_mlp_out, i)
            h, hb = ple_update(h, hb, norm_ple[i], w_ple_gate, i,
                               p[i, b].astype(BF16), w_ple[i].astype(BF16))
        outs.append(h)
    return outs[0][None] if bsz == 1 else jnp.stack(outs, axis=0)
```

```python
import functools
import math

import jax
import jax.numpy as jnp
from jax import lax
from jax.experimental import pallas as pl
from jax.experimental.pallas import tpu as pltpu

F32 = jnp.float32
BF16 = jnp.bfloat16

RMS_EPS = 1e-6
ROPE_THETA = 10000.0

A_HEAD_DIM = 128
A_HEADS_PER_GROUP = 16
A_PATTERNS = ((128, 1), (512, 4), (2048, 16))
A_STEPS = 128
A_RESIDUES = 16
A_GROUP_WIDTH = A_HEADS_PER_GROUP * A_HEAD_DIM

B_HEADS = 8
B_QK_DIM = 256
B_V_DIM = 512
B_CHUNK = 256
B_HEADS_PER_STEP = 4
B_GATE_PAD = 128

C_WINDOWS = (2, 4, 8, 16)
C_GROUP_DIM = 1024
C_HALO = 128

LANES = 128
VMEM_LIMIT_BYTES = 60 * 1024 * 1024


def _sigmoid(x):
    return 0.5 * jnp.tanh(0.5 * x) + 0.5


def _params(n_grid_axes):
    return pltpu.CompilerParams(
        dimension_semantics=("arbitrary",) * n_grid_axes,
        vmem_limit_bytes=VMEM_LIMIT_BYTES)


def _rmsnorm_rows(x, gain):
    inv = lax.rsqrt(jnp.mean(x * x, axis=-1, keepdims=True) + RMS_EPS)
    return x * inv * gain


def _row_permutation(tm, rows, to_residue_major):
    a = lax.broadcasted_iota(jnp.int32, (tm, tm), 0)
    b = lax.broadcasted_iota(jnp.int32, (tm, tm), 1)
    major, seq_order = (a, b) if to_residue_major else (b, a)
    return (seq_order == (major % rows) * A_RESIDUES + major // rows).astype(BF16)


def _rmsnorm_residue_major_kernel(x_ref, g_ref, o_ref):
    tm = x_ref.shape[0]
    rows = tm // A_RESIDUES
    y = _rmsnorm_rows(x_ref[...].astype(F32), g_ref[...]).astype(BF16)
    y = jnp.dot(_row_permutation(tm, rows, True), y, preferred_element_type=F32)
    o_ref[...] = y.astype(o_ref.dtype).reshape(o_ref.shape)


def rmsnorm_residue_major(h, gain, tm=256):
    m, d = h.shape
    tm = min(tm, m)
    in_specs = [pl.BlockSpec((tm, d), lambda i: (i, 0)),
                pl.BlockSpec((1, d), lambda i: (0, 0))]
    out = pl.pallas_call(
        _rmsnorm_residue_major_kernel,
        out_shape=jax.ShapeDtypeStruct((A_RESIDUES, m // A_RESIDUES, d), BF16),
        grid=(m // tm,), in_specs=in_specs,
        out_specs=pl.BlockSpec((A_RESIDUES, tm // A_RESIDUES, d), lambda i: (0, i, 0)),
        compiler_params=_params(1), name="rmsnorm_residue_major",
    )(h, gain.reshape(1, d))
    return out.reshape(m, d)


SIDE_CAST_ROWS = 64
NORM_ROWS = 64


def _row_inv_rms(x_ref):
    tm, k = x_ref.shape
    rows = min(NORM_ROWS, tm)
    pieces = []
    for r in range(tm // rows):
        part = None
        for c in range(k // LANES):
            xc = x_ref[r * rows:(r + 1) * rows, c * LANES:(c + 1) * LANES].astype(F32)
            part = xc * xc if part is None else part + xc * xc
        pieces.append(jnp.sum(part, axis=-1, keepdims=True))
    return lax.rsqrt(jnp.concatenate(pieces, axis=0) * (1.0 / k) + RMS_EPS)


def _mm_kernel(x_ref, wchunk_ref, *rest, epilogue, n_extra, n_out, n_slabs, n_side_steps, normed):
    if normed:
        gcol_ref, rest = rest[0], rest[1:]
    extra_refs, rest = rest[:n_extra], rest[n_extra:]
    if n_side_steps:
        side_ref, rest = rest[0], rest[1:]
    o_refs, rest = rest[:n_out], rest[n_out:]
    if n_side_steps:
        side_out_ref, rest = rest[0], rest[1:]
    if normed:
        wb_refs, inv_ref = rest[:2], rest[2]
    else:
        wb_refs = rest
    jj, i = pl.program_id(0), pl.program_id(1)
    chunk = wchunk_ref.shape[0]

    def cast_jobs(dst_ref):
        row0 = pl.multiple_of(i * chunk, chunk)
        wc = wchunk_ref[...]
        if normed:
            wc = wc * gcol_ref[...]
        dst_ref[pl.ds(row0, chunk), :] = wc.astype(BF16)
        if n_side_steps:
            side_out_ref[...] = side_ref[...].astype(BF16)

    @pl.when(jj == 0)
    def _():
        cast_jobs(wb_refs[0])

    if normed:
        tile_lane = lax.broadcasted_iota(jnp.int32, inv_ref.shape, 1) == i

        @pl.when((jj == 0) & (i == 0))
        def _():
            inv_ref[...] = jnp.zeros_like(inv_ref)

        @pl.when(jj == 1)
        def _():
            inv_ref[...] = jnp.where(tile_lane, _row_inv_rms(x_ref), inv_ref[...])

    def multiply(cur_ref, next_ref):
        cast_jobs(next_ref)
        acc = jnp.dot(x_ref[...], cur_ref[...], preferred_element_type=F32)
        if normed:
            acc = acc * jnp.sum(jnp.where(tile_lane, inv_ref[...], 0.0), axis=-1, keepdims=True)
        vals = epilogue(acc, slice(None), *extra_refs)
        vals = vals if isinstance(vals, tuple) else (vals,)
        for o_ref, val in zip(o_refs, vals, strict=True):
            o_ref[...] = val.astype(o_ref.dtype)

    @pl.when(jj % 2 == 1)
    def _():
        multiply(wb_refs[0], wb_refs[1])

    @pl.when((jj > 0) & (jj % 2 == 0))
    def _():
        multiply(wb_refs[1], wb_refs[0])


def fused_matmul(x, w, layer, col0, n, epilogue, *, out_dtype, tm=1024, tn=1024, extras=(), side=None,
                 norm_gain=None, name):
    m, k = x.shape
    tm, tn = min(tm, m), min(tn, n)
    n_slabs, n_rows = n // tn, m // tm
    assert w.shape[1] == k and m % tm == 0 and n % tn == 0 and col0 % tn == 0 and k % n_rows == 0, (
        w.shape, m, n, tm, tn, col0)
    jb = col0 // tn
    chunk = k // n_rows
    out_dtypes = out_dtype if isinstance(out_dtype, tuple) else (out_dtype,)

    def lag(im):
        return lambda jj, i: im(jnp.maximum(jj - 1, 0), jnp.where(jj > 0, i, 0))

    chunk_row = lambda jj, i: jnp.where(jj < n_slabs, i, n_rows - 1)
    in_specs = [pl.BlockSpec((tm, k), lag(lambda j, i: (i, 0))),
                pl.BlockSpec((None, chunk, tn),
                             lambda jj, i: (layer, chunk_row(jj, i), jb + jnp.minimum(jj, n_slabs - 1)))]
    operands = [x, w]
    if norm_gain is not None:
        assert n_rows <= LANES
        in_specs.append(pl.BlockSpec((chunk, 1), lambda jj, i: (chunk_row(jj, i), 0)))
        operands.append(norm_gain.reshape(k, 1))
    in_specs += [pl.BlockSpec(bs, lag(im)) for _, bs, im in extras]
    operands += [a for a, _, _ in extras]
    out_shape = [jax.ShapeDtypeStruct((m, n), dt) for dt in out_dtypes]
    out_specs = [pl.BlockSpec((tm, tn), lag(lambda j, i: (i, j))) for _ in out_dtypes]
    n_side_steps = 0
    if side is not None:
        side_w, side_layer = side
        side_rows, side_cols = side_w.shape[1:]
        rows_per_step = SIDE_CAST_ROWS
        while side_rows // rows_per_step > (n_slabs + 1) * n_rows:
            rows_per_step *= 2
        n_side_steps = side_rows // rows_per_step
        assert side_rows % rows_per_step == 0
        side_block = lambda jj, i: jnp.minimum(jj * n_rows + i, n_side_steps - 1)
        in_specs.append(pl.BlockSpec((None, rows_per_step, side_cols),
                                     lambda jj, i: (side_layer, side_block(jj, i), 0)))
        operands.append(side_w)
        out_shape.append(jax.ShapeDtypeStruct((side_rows, side_cols), BF16))
        out_specs.append(pl.BlockSpec((rows_per_step, side_cols), lambda jj, i: (side_block(jj, i), 0)))
    outs = pl.pallas_call(
        functools.partial(_mm_kernel, epilogue=epilogue, n_extra=len(extras), n_out=len(out_dtypes),
                          n_slabs=n_slabs, n_side_steps=n_side_steps, normed=norm_gain is not None),
        out_shape=out_shape,
        grid=(n_slabs + 1, n_rows),
        in_specs=in_specs,
        out_specs=out_specs,
        scratch_shapes=[pltpu.VMEM((k, tn), BF16), pltpu.VMEM((k, tn), BF16)] + (
            [pltpu.VMEM((tm, LANES), F32)] if norm_gain is not None else []),
        compiler_params=_params(2),
        name=name,
    )(*operands)
    return outs[0] if len(outs) == 1 else tuple(outs)


def _ep_identity(acc, cs):
    return acc


def _ep_relu_sq(acc, cs):
    r = jnp.maximum(acc, 0.0)
    return r * r


def _ep_residual(acc, cs, res_ref):
    h = res_ref[:, cs] + acc
    return h, h


def matmul_residual(x, w, layer, res, *, tm=512, tn=1024, name):
    n = w.shape[2]
    tm, tn = min(tm, x.shape[0]), min(tn, n)
    return fused_matmul(x, w, layer, 0, n, _ep_residual, out_dtype=(F32, BF16), tm=tm, tn=tn,
                        extras=[(res, (tm, tn), lambda j, i: (i, j))], name=name)


def _mm_kgrid_kernel(x_ref, w_ref, res_ref, o_ref, ob_ref, acc_ref):
    kk = pl.program_id(2)
    band = res_ref.shape[0]

    @pl.when(kk == 0)
    def _():
        acc_ref[...] = jnp.zeros_like(acc_ref)

    acc_ref[...] += jnp.dot(x_ref[...], w_ref[...], preferred_element_type=F32)
    row0 = pl.multiple_of(kk * band, band)
    acc_ref[pl.ds(row0, band), :] += res_ref[...]

    @pl.when(kk == pl.num_programs(2) - 1)
    def _():
        h = acc_ref[...]
        o_ref[...] = h
        ob_ref[...] = h.astype(ob_ref.dtype)


def matmul_kgrid_residual(x, w, res, *, tm=1024, tn=1024, tk=4096, name):
    m, k = x.shape
    n = w.shape[1]
    tm, tn, tk = min(tm, m), min(tn, n), min(tk, k)
    nk = k // tk
    assert m % tm == 0 and n % tn == 0 and k % tk == 0 and tm % (8 * nk) == 0
    return pl.pallas_call(
        _mm_kgrid_kernel,
        out_shape=(jax.ShapeDtypeStruct((m, n), F32), jax.ShapeDtypeStruct((m, n), BF16)),
        grid=(m // tm, n // tn, k // tk),
        in_specs=[pl.BlockSpec((tm, tk), lambda i, j, kk: (i, kk)),
                  pl.BlockSpec((tk, tn), lambda i, j, kk: (kk, j)),
                  pl.BlockSpec((tm // nk, tn), lambda i, j, kk: (i * nk + kk, j))],
        out_specs=(pl.BlockSpec((tm, tn), lambda i, j, kk: (i, j)),
                   pl.BlockSpec((tm, tn), lambda i, j, kk: (i, j))),
        scratch_shapes=[pltpu.VMEM((tm, tn), F32)],
        compiler_params=_params(3),
        name=name,
    )(x, w, res)


def _ep_ple(acc, cs, p_ref, wp_ref, res_ref):
    emb = jnp.dot(p_ref[...], wp_ref[:, cs], preferred_element_type=F32)
    h = res_ref[:, cs] + _sigmoid(acc) * emb
    return h, h


def ple_update(h, hb, norm_gain, w_gate, layer, p, w_ple, *, tm=1024, tn=512):
    n = w_gate.shape[2]
    tm, tn = min(tm, h.shape[0]), min(tn, n)
    kp = p.shape[1]
    return fused_matmul(
        hb, w_gate, layer, 0, n, _ep_ple, out_dtype=(F32, BF16), tm=tm, tn=tn, norm_gain=norm_gain,
        extras=[(p, (tm, kp), lambda j, i: (i, 0)),
                (w_ple, (kp, tn), lambda j, i: (0, j)),
                (h, (tm, tn), lambda j, i: (i, j))],
        name="ple_gate")


def _rope_table_kernel(cos_ref, sin_ref, *, rows_per_residue):
    tm = cos_ref.shape[0]
    half = A_HEAD_DIM // 2
    row0 = pl.program_id(0) * tm
    residue = row0 // rows_per_residue
    n0 = row0 % rows_per_residue
    n = n0 + lax.broadcasted_iota(jnp.int32, (tm, A_HEAD_DIM), 0)
    pos = (n * A_RESIDUES + residue).astype(F32)
    lane = lax.broadcasted_iota(jnp.int32, (tm, A_HEAD_DIM), 1)
    j = jnp.where(lane >= half, lane - half, lane).astype(F32)
    inv_freq = jnp.exp(j * (-2.0 * math.log(ROPE_THETA) / A_HEAD_DIM))
    ang = pos * inv_freq
    cos_ref[...] = jnp.cos(ang)
    sin_ref[...] = jnp.where(lane >= half, 1.0, -1.0) * jnp.sin(ang)


def rope_tables(seq, tm=256):
    rows_per_residue = seq // A_RESIDUES
    tm = min(tm, rows_per_residue)
    assert rows_per_residue % tm == 0
    spec = pl.BlockSpec((tm, A_HEAD_DIM), lambda i: (i, 0))
    shape = jax.ShapeDtypeStruct((seq, A_HEAD_DIM), F32)
    return pl.pallas_call(
        functools.partial(_rope_table_kernel, rows_per_residue=rows_per_residue),
        out_shape=(shape, shape), grid=(seq // tm,),
        in_specs=[], out_specs=(spec, spec), compiler_params=_params(1),
        name="rope_tables")()


def _ep_qk_norm_rope(acc, cs, gain_ref, gain_swapped_ref, cos_ref, sin_ref):
    d = A_HEAD_DIM
    cos_g = cos_ref[...] * gain_ref[:, :d]
    sin_g = sin_ref[...] * gain_swapped_ref[:, :d]
    mean_mat = jnp.full((d, d), 1.0 / d, BF16)
    r_i = lax.broadcasted_iota(jnp.int32, (d, d), 0)
    c_i = lax.broadcasted_iota(jnp.int32, (d, d), 1)
    swap_halves = (r_i == (c_i + d // 2) % d).astype(BF16)
    outs = []
    for hh in range(acc.shape[1] // d):
        a = acc[:, hh * d:(hh + 1) * d]
        mean_sq = jnp.dot((a * a).astype(BF16), mean_mat, preferred_element_type=F32)
        a_swapped = jnp.dot(a.astype(BF16), swap_halves, preferred_element_type=F32)
        outs.append((a * cos_g + a_swapped * sin_g) * lax.rsqrt(mean_sq + RMS_EPS))
    return jnp.concatenate(outs, axis=1)


def _attn_kernel(q_ref, kc_ref, vc_ref, o_ref, lse_ref, kp_ref, vp_ref, *, parts):
    rows = q_ref.shape[-2]
    qb = parts * rows
    n = pl.program_id(1)

    @pl.when(n == 0)
    def _():
        kp_ref[...] = jnp.zeros_like(kp_ref)
        vp_ref[...] = jnp.zeros_like(vp_ref)

    def load(ref, hs):
        return ref[..., hs].reshape(qb, A_HEAD_DIM)

    def step_of(idx):
        return idx if parts == 1 else (idx % rows) * parts + idx // rows

    lq = step_of(lax.broadcasted_iota(jnp.int32, (qb, 2 * qb), 0))
    col = lax.broadcasted_iota(jnp.int32, (qb, 2 * qb), 1)
    is_prev = col < qb
    lk = step_of(jnp.where(is_prev, col, col - qb))
    dist = lq - lk + jnp.where(is_prev, qb, 0)
    ok = (dist >= 0) & (dist <= A_STEPS) & ((n > 0) | jnp.logical_not(is_prev))
    lane = lax.broadcasted_iota(jnp.int32, (qb, LANES), 1)
    scale = A_HEAD_DIM ** -0.5
    nt = (((1,), (1,)), ((), ()))
    ones = jnp.ones((2 * qb, A_HEAD_DIM), BF16)
    lse_tile = jnp.zeros((qb, LANES), F32)
    for hh in range(A_HEADS_PER_GROUP):
        hs = slice(hh * A_HEAD_DIM, (hh + 1) * A_HEAD_DIM)
        q = load(q_ref, hs)
        k_cat = jnp.concatenate([load(kp_ref, hs), load(kc_ref, hs)], axis=0)
        v_cat = jnp.concatenate([load(vp_ref, hs), load(vc_ref, hs)], axis=0)
        s = lax.dot_general(q, k_cat, nt, preferred_element_type=F32) * scale
        s = jnp.where(ok, s, -jnp.inf)
        mx = jnp.max(jnp.maximum(s[:, :qb], s[:, qb:]), axis=-1, keepdims=True)
        e = jnp.exp(s - mx).astype(BF16)
        pv = jnp.dot(e, jnp.concatenate([v_cat, ones], axis=1), preferred_element_type=F32)
        den = pv[:, A_HEAD_DIM:]
        o_ref[..., hs] = (pv[:, :A_HEAD_DIM] / den).reshape(o_ref.shape[:-1] + (A_HEAD_DIM,)).astype(o_ref.dtype)
        lse_tile = jnp.where(lane == hh, mx + jnp.log(den), lse_tile)
    lse_ref[...] = lse_tile.reshape(lse_ref.shape)
    kp_ref[...] = kc_ref[...]
    vp_ref[...] = vc_ref[...]


def dilated_attention(qk, v, group, dilation):
    seq = qk.shape[0]
    gw = A_GROUP_WIDTH
    n_groups = len(A_PATTERNS)
    parts = A_RESIDUES // dilation
    rpr = seq // A_RESIDUES
    rows = max(A_STEPS // parts, 16)
    nb = rpr // rows
    qk_w, v_w = qk.shape[1], v.shape[1]
    if parts == 1:
        vshape = lambda w: (seq, w)
        blk = lambda w: (rows, w)
        at = lambda r, n, c: (r * nb + n, c)
    else:
        vshape = lambda w: (parts, dilation, rpr, w)
        blk = lambda w: (parts, None, rows, w)
        at = lambda r, n, c: (0, r, n, c)
    view = lambda a: a.reshape(vshape(a.shape[1]))
    o, lse = pl.pallas_call(
        functools.partial(_attn_kernel, parts=parts),
        out_shape=(jax.ShapeDtypeStruct(vshape(gw), BF16),
                   jax.ShapeDtypeStruct(vshape(LANES), F32)),
        grid=(dilation, nb),
        in_specs=[pl.BlockSpec(blk(gw), lambda r, n: at(r, n, group)),
                  pl.BlockSpec(blk(gw), lambda r, n: at(r, n, n_groups + group)),
                  pl.BlockSpec(blk(gw), lambda r, n: at(r, n, group))],
        out_specs=(pl.BlockSpec(blk(gw), lambda r, n: at(r, n, 0)),
                   pl.BlockSpec(blk(LANES), lambda r, n: at(r, n, 0))),
        scratch_shapes=[pltpu.VMEM(tuple(d for d in blk(gw) if d is not None), BF16)] * 2,
        compiler_params=_params(2),
        name=f"dilated_attn_g{group}",
    )(view(qk), view(qk), view(v))
    return o.reshape(seq, gw), lse.reshape(seq, LANES)


def _attn_merge_kernel(o0_ref, o1_ref, o2_ref, l0_ref, l1_ref, l2_ref, out_ref):
    rows = o0_ref.shape[1]
    tm = A_RESIDUES * rows

    def flat(ref, sl=slice(None)):
        return ref[:, :, sl].reshape(tm, -1)

    l0, l1, l2 = flat(l0_ref), flat(l1_ref), flat(l2_ref)
    mx = jnp.maximum(jnp.maximum(l0, l1), l2)
    w0, w1, w2 = jnp.exp(l0 - mx), jnp.exp(l1 - mx), jnp.exp(l2 - mx)
    tot = w0 + w1 + w2
    w0, w1, w2 = w0 / tot, w1 / tot, w2 / tot
    to_sequence_order = _row_permutation(tm, rows, False)
    for hh in range(A_HEADS_PER_GROUP):
        hs = slice(hh * A_HEAD_DIM, (hh + 1) * A_HEAD_DIM)
        merged = (w0[:, hh:hh + 1] * flat(o0_ref, hs).astype(F32)
                  + w1[:, hh:hh + 1] * flat(o1_ref, hs).astype(F32)
                  + w2[:, hh:hh + 1] * flat(o2_ref, hs).astype(F32)).astype(BF16)
        out_ref[:, hs] = jnp.dot(to_sequence_order, merged, preferred_element_type=F32).astype(out_ref.dtype)


def attention_merge(outs, lses, rows=16):
    seq, gw = outs[0].shape
    rpr = seq // A_RESIDUES
    rows = min(rows, rpr)
    tm = A_RESIDUES * rows
    o_spec = pl.BlockSpec((A_RESIDUES, rows, gw), lambda i: (0, i, 0))
    l_spec = pl.BlockSpec((A_RESIDUES, rows, LANES), lambda i: (0, i, 0))
    return pl.pallas_call(
        _attn_merge_kernel,
        out_shape=jax.ShapeDtypeStruct((seq, gw), BF16),
        grid=(rpr // rows,),
        in_specs=[o_spec] * 3 + [l_spec] * 3,
        out_specs=pl.BlockSpec((tm, gw), lambda i: (i, 0)),
        compiler_params=_params(1),
        name="attn_merge",
    )(*[o.reshape(A_RESIDUES, rpr, gw) for o in outs],
      *[l.reshape(A_RESIDUES, rpr, LANES) for l in lses])


def dilated_attention_mixer(h, hb, norm_gain, w_in, layer, q_gain, k_gain, w_out, rope):
    seq = h.shape[0]
    n_heads = len(A_PATTERNS) * A_HEADS_PER_GROUP
    qk_width = 2 * n_heads * A_HEAD_DIM
    v_width = n_heads * A_HEAD_DIM
    assert seq % (A_RESIDUES * A_STEPS) == 0
    assert all(w // d == A_STEPS and A_RESIDUES % d == 0 for w, d in A_PATTERNS)
    xn = rmsnorm_residue_major(h if hb is None else hb, norm_gain)
    cos, sin = rope
    per_head = lambda g_q, g_k: jnp.concatenate([jnp.tile(g_q, n_heads), jnp.tile(g_k, n_heads)]).reshape(1, qk_width)
    gain = per_head(q_gain, k_gain)
    gain_swapped = per_head(jnp.roll(q_gain, A_HEAD_DIM // 2), jnp.roll(k_gain, A_HEAD_DIM // 2))
    tm, tn = min(1024, seq), 1024
    assert (qk_width // 2) % tn == 0
    qk = fused_matmul(
        xn, w_in, layer, 0, qk_width, _ep_qk_norm_rope, out_dtype=BF16, tm=tm, tn=tn,
        extras=[(gain, (1, tn), lambda j, i: (0, j)),
                (gain_swapped, (1, tn), lambda j, i: (0, j)),
                (cos, (tm, A_HEAD_DIM), lambda j, i: (i, 0)),
                (sin, (tm, A_HEAD_DIM), lambda j, i: (i, 0))],
        name="attn_qk_proj")
    v = fused_matmul(xn, w_in, layer, qk_width, v_width, _ep_identity, out_dtype=BF16,
                     tm=tm, tn=tn, name="attn_v_proj")
    outs, lses = [], []
    for g, (_, dilation) in enumerate(A_PATTERNS):
        o, lse = dilated_attention(qk, v, g, dilation)
        outs.append(o)
        lses.append(lse)
    merged = attention_merge(outs, lses)
    return matmul_residual(merged, w_out, layer, h, tm=1024, name="attn_out_proj")


def _ep_gates(acc, cs, bias_ref):
    lane = lax.broadcasted_iota(jnp.int32, acc.shape, 1)
    g = jnp.where(lane < 2 * B_HEADS, acc + bias_ref[...], 0.0)
    return jnp.where(lane >= B_HEADS, jax.nn.log_sigmoid(g), g)


def _mlstm_kernel(q_ref, k_ref, v_ref, og_ref, gcol_ref, grow_ref, gain_ref, o_ref,
                  c_ref, n_ref, m_ref):
    @pl.when(pl.program_id(1) == 0)
    def _():
        c_ref[...] = jnp.zeros_like(c_ref)
        n_ref[...] = jnp.zeros_like(n_ref)
        m_ref[...] = jnp.zeros_like(m_ref)

    for hh in range(B_HEADS_PER_STEP):
        qs = slice(hh * B_QK_DIM, (hh + 1) * B_QK_DIM)
        vs = slice(hh * B_V_DIM, (hh + 1) * B_V_DIM)
        _mlstm_head(pl.program_id(0) * B_HEADS_PER_STEP + hh,
                    q_ref.at[:, qs], k_ref.at[:, qs], v_ref.at[:, vs], og_ref.at[:, vs], gcol_ref, grow_ref,
                    gain_ref.at[:, vs], o_ref.at[:, vs], c_ref.at[hh], n_ref.at[hh], m_ref.at[hh])


def _mlstm_head(hd, q_ref, k_ref, v_ref, og_ref, gcol_ref, grow_ref, gain_ref, o_ref, c_ref, n_ref, m_ref):
    L = q_ref.shape[0]
    gcol = gcol_ref[...]
    lane = lax.broadcasted_iota(jnp.int32, gcol.shape, 1)
    ig_col = jnp.sum(jnp.where(lane == hd, gcol, 0.0), axis=-1, keepdims=True)
    lf_col = jnp.sum(jnp.where(lane == hd + B_HEADS, gcol, 0.0), axis=-1, keepdims=True)
    ig_row = grow_ref[pl.ds(hd, 1), :]
    lf_row = grow_ref[pl.ds(hd + B_HEADS, 1), :]

    t_i = lax.broadcasted_iota(jnp.int32, (L, L), 0)
    s_i = lax.broadcasted_iota(jnp.int32, (L, L), 1)
    causal = s_i <= t_i
    lower = causal.astype(BF16)
    upper = (t_i <= s_i).astype(BF16)

    def split3(a):
        hi = a.astype(BF16).astype(F32)
        r1 = a - hi
        mid = r1.astype(BF16).astype(F32)
        return hi, mid, (r1 - mid).astype(BF16).astype(F32)

    hi, mid, lo = split3(lf_col)
    parts_col = jnp.where(lane == 0, hi, jnp.where(lane == 1, mid, jnp.where(lane == 2, lo, 0.0))).astype(BF16)
    cum = jnp.dot(lower, parts_col, preferred_element_type=F32)
    b_col = cum[:, 0:1] + cum[:, 1:2] + cum[:, 2:3]
    hi, mid, lo = split3(lf_row)
    sub = lax.broadcasted_iota(jnp.int32, (8, L), 0)
    parts_row = jnp.where(sub == 0, hi, jnp.where(sub == 1, mid, jnp.where(sub == 2, lo, 0.0))).astype(BF16)
    cum = jnp.dot(parts_row, upper, preferred_element_type=F32)
    b_row = cum[0:1, :] + cum[1:2, :] + cum[2:3, :]

    m_prev = m_ref[:1, :1]
    c_prev = c_ref[...]
    n_prev = n_ref[...]

    q = q_ref[...] * jnp.asarray(B_QK_DIM ** -0.5, q_ref.dtype)
    k = k_ref[...]
    v = v_ref[...]

    log_d = jnp.where(causal, b_col - b_row + ig_row, -jnp.inf)
    log_inter = b_col + m_prev
    m_t = jnp.maximum(log_inter, jnp.max(log_d, axis=-1, keepdims=True))
    scores = lax.dot_general(q, k, (((1,), (1,)), ((), ())), preferred_element_type=F32)
    w = scores * jnp.exp(log_d - m_t)
    decay = jnp.exp(log_inter - m_t)
    num = (decay * jnp.dot(q, c_prev.astype(BF16), preferred_element_type=F32)
           + jnp.dot(w.astype(BF16), v, preferred_element_type=F32))
    qn = (decay * jnp.sum(q.astype(F32) * n_prev, axis=-1, keepdims=True)
          + jnp.sum(w, axis=-1, keepdims=True))
    hval = num / jnp.maximum(jnp.abs(qn), jnp.exp(-m_t))

    inv = lax.rsqrt(jnp.mean(hval * hval, axis=-1, keepdims=True) + RMS_EPS)
    o_ref[...] = (hval * inv * gain_ref[...] * _sigmoid(og_ref[...].astype(F32))).astype(o_ref.dtype)

    b_last = b_col[L - 1:L, :]
    log_w = b_last - b_col + ig_col
    m_new = jnp.maximum(b_last + m_prev, jnp.max(log_w, axis=0, keepdims=True))
    w_s = jnp.exp(log_w - m_new)
    carry = jnp.exp(b_last + m_prev - m_new)
    k_w = k.astype(F32) * w_s
    c_ref[...] = carry * c_prev + lax.dot_general(
        k_w.astype(BF16), v, (((0,), (0,)), ((), ())), preferred_element_type=F32)
    n_ref[...] = carry * n_prev + jnp.sum(k_w, axis=0, keepdims=True)
    m_ref[...] = jnp.broadcast_to(m_new, m_ref.shape)


def mlstm_mixer(h, hb, norm_gain, w_in, layer, gate_bias, h_gain, w_out):
    seq, d = h.shape
    qk_w = B_HEADS * B_QK_DIM
    v_w = B_HEADS * B_V_DIM
    main_w = 2 * qk_w + v_w + d
    proj = fused_matmul(hb, w_in, layer, 0, main_w, _ep_identity, out_dtype=BF16, norm_gain=norm_gain,
                        name="mlstm_in_proj")
    n_gate = 2 * B_HEADS
    assert w_in.shape[2] == main_w + n_gate
    bias = jnp.pad(gate_bias, (0, B_GATE_PAD - n_gate)).reshape(1, B_GATE_PAD)
    gates = fused_matmul(hb, w_in, layer, main_w, B_GATE_PAD, _ep_gates, out_dtype=F32, tn=B_GATE_PAD,
                         norm_gain=norm_gain, extras=[(bias, (1, B_GATE_PAD), lambda j, i: (0, 0))],
                         name="mlstm_gate_proj")
    gates_t = gates[:, :n_gate].T
    L = min(B_CHUNK, seq)
    hps = B_HEADS_PER_STEP
    qk_blk, v_blk = hps * B_QK_DIM, hps * B_V_DIM
    nq = qk_w // qk_blk
    nv = v_w // v_blk
    mixed = pl.pallas_call(
        _mlstm_kernel,
        out_shape=jax.ShapeDtypeStruct((seq, v_w), BF16),
        grid=(B_HEADS // hps, seq // L),
        in_specs=[pl.BlockSpec((L, qk_blk), lambda g, c: (c, g)),
                  pl.BlockSpec((L, qk_blk), lambda g, c: (c, nq + g)),
                  pl.BlockSpec((L, v_blk), lambda g, c: (c, (2 * qk_w) // v_blk + g)),
                  pl.BlockSpec((L, v_blk), lambda g, c: (c, (2 * qk_w) // v_blk + nv + g)),
                  pl.BlockSpec((L, B_GATE_PAD), lambda g, c: (c, 0)),
                  pl.BlockSpec((n_gate, L), lambda g, c: (0, c)),
                  pl.BlockSpec((1, v_blk), lambda g, c: (0, g))],
        out_specs=pl.BlockSpec((L, v_blk), lambda g, c: (c, g)),
        scratch_shapes=[pltpu.VMEM((hps, B_QK_DIM, B_V_DIM), F32),
                        pltpu.VMEM((hps, 1, B_QK_DIM), F32),
                        pltpu.VMEM((hps, 8, LANES), F32)],
        compiler_params=_params(2),
        name="mlstm_chunks",
    )(proj, proj, proj, proj, gates, gates_t, h_gain.reshape(1, v_w))
    return matmul_residual(mixed, w_out, layer, h, name="mlstm_out_proj")


def _pool_kernel(u_ref, halo_ref, wg_ref, scale_ref, o_ref, wb_ref):
    g = pl.program_id(0)
    i = pl.program_id(1)
    tm = u_ref.shape[0]

    @pl.when(i == 0)
    def _():
        wb_ref[...] = wg_ref[...].astype(BF16)

    win = jnp.left_shift(jnp.int32(C_WINDOWS[0]), g)
    t_i = lax.broadcasted_iota(jnp.int32, (tm, tm), 0)
    s_i = lax.broadcasted_iota(jnp.int32, (tm, tm), 1)
    dist = t_i - s_i
    band = ((dist >= 0) & (dist < win)).astype(BF16)
    t_h = lax.broadcasted_iota(jnp.int32, (tm, C_HALO), 0)
    s_h = lax.broadcasted_iota(jnp.int32, (tm, C_HALO), 1)
    band_halo = ((t_h + C_HALO - s_h < win) & (i > 0)).astype(BF16)
    u = u_ref[...]
    total = (jnp.dot(band, u, preferred_element_type=F32)
             + jnp.dot(band_halo, halo_ref[...], preferred_element_type=F32))
    t_glob = i * tm + lax.broadcasted_iota(jnp.int32, (tm, 1), 0)
    count = jnp.minimum(t_glob + 1, win).astype(F32)
    pooled = total / count - u.astype(F32)
    y = jnp.dot(pooled.astype(BF16), wb_ref[...], preferred_element_type=F32) * scale_ref[...]
    o_ref[...] = y.astype(o_ref.dtype)


def pooling_mixer(h, hb, norm_gain, w_in, layer, w_group, scale, w_out, tm=512):
    seq, d = h.shape
    assert all(C_WINDOWS[g] == C_WINDOWS[0] << g for g in range(len(C_WINDOWS)))
    assert max(C_WINDOWS) <= C_HALO
    u = fused_matmul(hb, w_in, layer, 0, d, _ep_identity, out_dtype=BF16, norm_gain=norm_gain,
                     name="pool_in_proj")
    tm = min(tm, seq)
    gd = C_GROUP_DIM
    halo_per_tile = tm // C_HALO
    y = pl.pallas_call(
        _pool_kernel,
        out_shape=jax.ShapeDtypeStruct((seq, d), BF16),
        grid=(len(C_WINDOWS), seq // tm),
        in_specs=[pl.BlockSpec((tm, gd), lambda g, i: (i, g)),
                  pl.BlockSpec((C_HALO, gd), lambda g, i: (jnp.maximum(i * halo_per_tile - 1, 0), g)),
                  pl.BlockSpec((None, None, gd, gd), lambda g, i: (layer, g, 0, 0)),
                  pl.BlockSpec((1, gd), lambda g, i: (0, g))],
        out_specs=pl.BlockSpec((tm, gd), lambda g, i: (i, g)),
        scratch_shapes=[pltpu.VMEM((gd, gd), BF16)],
        compiler_params=_params(2),
        name="pool_group",
    )(u, u, w_group, scale.reshape(1, d))
    return matmul_residual(y, w_out, layer, h, name="pool_out_proj")


def squared_relu_mlp(h, hb, norm_gain, w_in, w_out, layer):
    hidden, w_out_bf16 = fused_matmul(hb, w_in, layer, 0, w_in.shape[2], _ep_relu_sq, out_dtype=BF16,
                                      norm_gain=norm_gain, side=(w_out, layer), name="mlp_up")
    return matmul_kgrid_residual(hidden, w_out_bf16, h, name="mlp_down")


def kernel(x, p, norm_mix, norm_mlp, norm_ple, w_ple, w_ple_gate, w_mlp_in, w_mlp_out, a_w_in, a_q_norm, a_k_norm, a_w_out, b_w_in, b_gate_bias, b_h_norm, b_w_out, c_w_in, c_w_group, c_scale, c_w_out):
    bsz, seq, d = x.shape
    depth = p.shape[0]
    rope = rope_tables(seq)
    outs = []
    for b in range(bsz):
        h = x.reshape(seq, d) if bsz == 1 else x[b]
        hb = None
        for i in range(depth):
            kind, j = i % 3, i // 3
            if kind == 0:
                h, hb = dilated_attention_mixer(h, hb, norm_mix[i], a_w_in, j, a_q_norm[j], a_k_norm[j], a_w_out, rope)
            elif kind == 1:
                h, hb = mlstm_mixer(h, hb, norm_mix[i], b_w_in, j, b_gate_bias[j], b_h_norm[j], b_w_out)
            else:
                h, hb = pooling_mixer(h, hb, norm_mix[i], c_w_in, j, c_w_group, c_scale[j], c_w_out)
            h, hb = squared_relu_mlp(h, hb, norm_mlp[i], w_mlp_in, w_mlp_out, i)
            h, hb = ple_update(h, hb, norm_ple[i], w_ple_gate, i,
                               p[i, b].astype(BF16), w_ple[i].astype(BF16))
        outs.append(h)
    return outs[0][None] if bsz == 1 else jnp.stack(outs, axis=0)
```

```python
import functools
import math

import jax
import jax.numpy as jnp
from jax import lax
from jax.experimental import pallas as pl
from jax.experimental.pallas import tpu as pltpu

F32 = jnp.float32
BF16 = jnp.bfloat16

RMS_EPS = 1e-6
ROPE_THETA = 10000.0

A_HEAD_DIM = 128
A_HEADS_PER_GROUP = 16
A_PATTERNS = ((128, 1), (512, 4), (2048, 16))
A_STEPS = 128
A_RESIDUES = 16
A_GROUP_WIDTH = A_HEADS_PER_GROUP * A_HEAD_DIM

B_HEADS = 8
B_QK_DIM = 256
B_V_DIM = 512
B_CHUNK = 256
B_HEADS_PER_STEP = 4
B_GATE_PAD = 128

C_WINDOWS = (2, 4, 8, 16)
C_GROUP_DIM = 1024
C_HALO = 128

LANES = 128
VMEM_LIMIT_BYTES = 60 * 1024 * 1024


def _sigmoid(x):
    return 0.5 * jnp.tanh(0.5 * x) + 0.5


def _params(n_grid_axes):
    return pltpu.CompilerParams(
        dimension_semantics=("arbitrary",) * n_grid_axes,
        vmem_limit_bytes=VMEM_LIMIT_BYTES)


def _rmsnorm_rows(x, gain):
    inv = lax.rsqrt(jnp.mean(x * x, axis=-1, keepdims=True) + RMS_EPS)
    return x * inv * gain


def _row_permutation(tm, rows, to_residue_major):
    a = lax.broadcasted_iota(jnp.int32, (tm, tm), 0)
    b = lax.broadcasted_iota(jnp.int32, (tm, tm), 1)
    major, seq_order = (a, b) if to_residue_major else (b, a)
    return (seq_order == (major % rows) * A_RESIDUES + major // rows).astype(BF16)


def _rmsnorm_residue_major_kernel(x_ref, g_ref, o_ref):
    tm = x_ref.shape[0]
    rows = tm // A_RESIDUES
    y = _rmsnorm_rows(x_ref[...], g_ref[...]).astype(BF16)
    y = jnp.dot(_row_permutation(tm, rows, True), y, preferred_element_type=F32)
    o_ref[...] = y.astype(o_ref.dtype).reshape(o_ref.shape)


def rmsnorm_residue_major(h, gain, tm=512):
    m, d = h.shape
    tm = min(tm, m)
    in_specs = [pl.BlockSpec((tm, d), lambda i: (i, 0)),
                pl.BlockSpec((1, d), lambda i: (0, 0))]
    out = pl.pallas_call(
        _rmsnorm_residue_major_kernel,
        out_shape=jax.ShapeDtypeStruct((A_RESIDUES, m // A_RESIDUES, d), BF16),
        grid=(m // tm,), in_specs=in_specs,
        out_specs=pl.BlockSpec((A_RESIDUES, tm // A_RESIDUES, d), lambda i: (0, i, 0)),
        compiler_params=_params(1), name="rmsnorm_residue_major",
    )(h, gain.reshape(1, d))
    return out.reshape(m, d)


SIDE_CAST_ROWS = 64
NORM_ROWS = 64


def _row_inv_rms(x_ref):
    tm, k = x_ref.shape
    rows = min(NORM_ROWS, tm)
    pieces = []
    for r in range(tm // rows):
        part = None
        for c in range(k // LANES):
            xc = x_ref[r * rows:(r + 1) * rows, c * LANES:(c + 1) * LANES].astype(F32)
            part = xc * xc if part is None else part + xc * xc
        pieces.append(jnp.sum(part, axis=-1, keepdims=True))
    return lax.rsqrt(jnp.concatenate(pieces, axis=0) * (1.0 / k) + RMS_EPS)


def _mm_kernel(x_ref, wchunk_ref, *rest, epilogue, n_extra, n_out, n_slabs, n_side_steps, normed):
    if normed:
        gcol_ref, rest = rest[0], rest[1:]
    extra_refs, rest = rest[:n_extra], rest[n_extra:]
    if n_side_steps:
        side_ref, rest = rest[0], rest[1:]
    o_refs, rest = rest[:n_out], rest[n_out:]
    if n_side_steps:
        side_out_ref, rest = rest[0], rest[1:]
    if normed:
        wb_refs, inv_ref = rest[:2], rest[2]
    else:
        wb_refs = rest
    jj, i = pl.program_id(0), pl.program_id(1)
    chunk = wchunk_ref.shape[0]

    def cast_jobs(dst_ref):
        row0 = pl.multiple_of(i * chunk, chunk)
        wc = wchunk_ref[...]
        if normed:
            wc = wc * gcol_ref[...]
        dst_ref[pl.ds(row0, chunk), :] = wc.astype(BF16)
        if n_side_steps:
            side_out_ref[...] = side_ref[...].astype(BF16)

    @pl.when(jj == 0)
    def _():
        cast_jobs(wb_refs[0])

    if normed:
        tile_lane = lax.broadcasted_iota(jnp.int32, inv_ref.shape, 1) == i

        @pl.when((jj == 0) & (i == 0))
        def _():
            inv_ref[...] = jnp.zeros_like(inv_ref)

        @pl.when(jj == 1)
        def _():
            inv_ref[...] = jnp.where(tile_lane, _row_inv_rms(x_ref), inv_ref[...])

    def multiply(cur_ref, next_ref):
        cast_jobs(next_ref)
        acc = jnp.dot(x_ref[...], cur_ref[...], preferred_element_type=F32)
        if normed:
            acc = acc * jnp.sum(jnp.where(tile_lane, inv_ref[...], 0.0), axis=-1, keepdims=True)
        vals = epilogue(acc, slice(None), *extra_refs)
        vals = vals if isinstance(vals, tuple) else (vals,)
        for o_ref, val in zip(o_refs, vals, strict=True):
            o_ref[...] = val.astype(o_ref.dtype)

    @pl.when(jj % 2 == 1)
    def _():
        multiply(wb_refs[0], wb_refs[1])

    @pl.when((jj > 0) & (jj % 2 == 0))
    def _():
        multiply(wb_refs[1], wb_refs[0])


def fused_matmul(x, w, layer, col0, n, epilogue, *, out_dtype, tm=1024, tn=1024, extras=(), side=None,
                 norm_gain=None, name):
    m, k = x.shape
    tm, tn = min(tm, m), min(tn, n)
    n_slabs, n_rows = n // tn, m // tm
    assert w.shape[1] == k and m % tm == 0 and n % tn == 0 and col0 % tn == 0 and k % n_rows == 0, (
        w.shape, m, n, tm, tn, col0)
    jb = col0 // tn
    chunk = k // n_rows
    out_dtypes = out_dtype if isinstance(out_dtype, tuple) else (out_dtype,)

    def lag(im):
        return lambda jj, i: im(jnp.maximum(jj - 1, 0), jnp.where(jj > 0, i, 0))

    chunk_row = lambda jj, i: jnp.where(jj < n_slabs, i, n_rows - 1)
    in_specs = [pl.BlockSpec((tm, k), lag(lambda j, i: (i, 0))),
                pl.BlockSpec((None, chunk, tn),
                             lambda jj, i: (layer, chunk_row(jj, i), jb + jnp.minimum(jj, n_slabs - 1)))]
    operands = [x, w]
    if norm_gain is not None:
        assert n_rows <= LANES
        in_specs.append(pl.BlockSpec((chunk, 1), lambda jj, i: (chunk_row(jj, i), 0)))
        operands.append(norm_gain.reshape(k, 1))
    in_specs += [pl.BlockSpec(bs, lag(im)) for _, bs, im in extras]
    operands += [a for a, _, _ in extras]
    out_shape = [jax.ShapeDtypeStruct((m, n), dt) for dt in out_dtypes]
    out_specs = [pl.BlockSpec((tm, tn), lag(lambda j, i: (i, j))) for _ in out_dtypes]
    n_side_steps = 0
    if side is not None:
        side_w, side_layer = side
        side_rows, side_cols = side_w.shape[1:]
        rows_per_step = SIDE_CAST_ROWS
        while side_rows // rows_per_step > (n_slabs + 1) * n_rows:
            rows_per_step *= 2
        n_side_steps = side_rows // rows_per_step
        assert side_rows % rows_per_step == 0
        side_block = lambda jj, i: jnp.minimum(jj * n_rows + i, n_side_steps - 1)
        in_specs.append(pl.BlockSpec((None, rows_per_step, side_cols),
                                     lambda jj, i: (side_layer, side_block(jj, i), 0)))
        operands.append(side_w)
        out_shape.append(jax.ShapeDtypeStruct((side_rows, side_cols), BF16))
        out_specs.append(pl.BlockSpec((rows_per_step, side_cols), lambda jj, i: (side_block(jj, i), 0)))
    outs = pl.pallas_call(
        functools.partial(_mm_kernel, epilogue=epilogue, n_extra=len(extras), n_out=len(out_dtypes),
                          n_slabs=n_slabs, n_side_steps=n_side_steps, normed=norm_gain is not None),
        out_shape=out_shape,
        grid=(n_slabs + 1, n_rows),
        in_specs=in_specs,
        out_specs=out_specs,
        scratch_shapes=[pltpu.VMEM((k, tn), BF16), pltpu.VMEM((k, tn), BF16)] + (
            [pltpu.VMEM((tm, LANES), F32)] if norm_gain is not None else []),
        compiler_params=_params(2),
        name=name,
    )(*operands)
    return outs[0] if len(outs) == 1 else tuple(outs)


def _ep_identity(acc, cs):
    return acc


def _ep_relu_sq(acc, cs):
    r = jnp.maximum(acc, 0.0)
    return r * r


def _ep_residual(acc, cs, res_ref):
    h = res_ref[:, cs] + acc
    return h, h


def matmul_residual(x, w, layer, res, *, tm=512, tn=1024, name):
    n = w.shape[2]
    tm, tn = min(tm, x.shape[0]), min(tn, n)
    return fused_matmul(x, w, layer, 0, n, _ep_residual, out_dtype=(F32, BF16), tm=tm, tn=tn,
                        extras=[(res, (tm, tn), lambda j, i: (i, j))], name=name)


def _mm_kgrid_kernel(x_ref, w_ref, res_ref, o_ref, ob_ref, acc_ref):
    kk = pl.program_id(2)
    band = res_ref.shape[0]

    @pl.when(kk == 0)
    def _():
        acc_ref[...] = jnp.zeros_like(acc_ref)

    acc_ref[...] += jnp.dot(x_ref[...], w_ref[...], preferred_element_type=F32)
    row0 = pl.multiple_of(kk * band, band)
    acc_ref[pl.ds(row0, band), :] += res_ref[...]

    @pl.when(kk == pl.num_programs(2) - 1)
    def _():
        h = acc_ref[...]
        o_ref[...] = h
        ob_ref[...] = h.astype(ob_ref.dtype)


def matmul_kgrid_residual(x, w, res, *, tm=1024, tn=1024, tk=4096, name):
    m, k = x.shape
    n = w.shape[1]
    tm, tn, tk = min(tm, m), min(tn, n), min(tk, k)
    nk = k // tk
    assert m % tm == 0 and n % tn == 0 and k % tk == 0 and tm % (8 * nk) == 0
    return pl.pallas_call(
        _mm_kgrid_kernel,
        out_shape=(jax.ShapeDtypeStruct((m, n), F32), jax.ShapeDtypeStruct((m, n), BF16)),
        grid=(m // tm, n // tn, k // tk),
        in_specs=[pl.BlockSpec((tm, tk), lambda i, j, kk: (i, kk)),
                  pl.BlockSpec((tk, tn), lambda i, j, kk: (kk, j)),
                  pl.BlockSpec((tm // nk, tn), lambda i, j, kk: (i * nk + kk, j))],
        out_specs=(pl.BlockSpec((tm, tn), lambda i, j, kk: (i, j)),
                   pl.BlockSpec((tm, tn), lambda i, j, kk: (i, j))),
        scratch_shapes=[pltpu.VMEM((tm, tn), F32)],
        compiler_params=_params(3),
        name=name,
    )(x, w, res)


def _ep_ple(acc, cs, p_ref, wp_ref, res_ref):
    emb = jnp.dot(p_ref[...], wp_ref[:, cs], preferred_element_type=F32)
    h = res_ref[:, cs] + _sigmoid(acc) * emb
    return h, h


def ple_update(h, hb, norm_gain, w_gate, layer, p, w_ple, *, tm=512, tn=1024):
    n = w_gate.shape[2]
    tm, tn = min(tm, h.shape[0]), min(tn, n)
    kp = p.shape[1]
    return fused_matmul(
        hb, w_gate, layer, 0, n, _ep_ple, out_dtype=(F32, BF16), tm=tm, tn=tn, norm_gain=norm_gain,
        extras=[(p, (tm, kp), lambda j, i: (i, 0)),
                (w_ple, (kp, tn), lambda j, i: (0, j)),
                (h, (tm, tn), lambda j, i: (i, j))],
        name="ple_gate")


def _rope_table_kernel(cos_ref, sin_ref, *, rows_per_residue):
    tm = cos_ref.shape[0]
    half = A_HEAD_DIM // 2
    row0 = pl.program_id(0) * tm
    residue = row0 // rows_per_residue
    n0 = row0 % rows_per_residue
    n = n0 + lax.broadcasted_iota(jnp.int32, (tm, A_HEAD_DIM), 0)
    pos = (n * A_RESIDUES + residue).astype(F32)
    lane = lax.broadcasted_iota(jnp.int32, (tm, A_HEAD_DIM), 1)
    j = jnp.where(lane >= half, lane - half, lane).astype(F32)
    inv_freq = jnp.exp(j * (-2.0 * math.log(ROPE_THETA) / A_HEAD_DIM))
    ang = pos * inv_freq
    cos_ref[...] = jnp.cos(ang)
    sin_ref[...] = jnp.where(lane >= half, 1.0, -1.0) * jnp.sin(ang)


def rope_tables(seq, tm=256):
    rows_per_residue = seq // A_RESIDUES
    tm = min(tm, rows_per_residue)
    assert rows_per_residue % tm == 0
    spec = pl.BlockSpec((tm, A_HEAD_DIM), lambda i: (i, 0))
    shape = jax.ShapeDtypeStruct((seq, A_HEAD_DIM), F32)
    return pl.pallas_call(
        functools.partial(_rope_table_kernel, rows_per_residue=rows_per_residue),
        out_shape=(shape, shape), grid=(seq // tm,),
        in_specs=[], out_specs=(spec, spec), compiler_params=_params(1),
        name="rope_tables")()


def _ep_qk_norm_rope(acc, cs, gain_ref, gain_swapped_ref, cos_ref, sin_ref):
    d = A_HEAD_DIM
    cos_g = cos_ref[...] * gain_ref[:, :d]
    sin_g = sin_ref[...] * gain_swapped_ref[:, :d]
    mean_mat = jnp.full((d, d), 1.0 / d, BF16)
    r_i = lax.broadcasted_iota(jnp.int32, (d, d), 0)
    c_i = lax.broadcasted_iota(jnp.int32, (d, d), 1)
    swap_halves = (r_i == (c_i + d // 2) % d).astype(BF16)
    outs = []
    for hh in range(acc.shape[1] // d):
        a = acc[:, hh * d:(hh + 1) * d]
        mean_sq = jnp.dot((a * a).astype(BF16), mean_mat, preferred_element_type=F32)
        a_swapped = jnp.dot(a.astype(BF16), swap_halves, preferred_element_type=F32)
        outs.append((a * cos_g + a_swapped * sin_g) * lax.rsqrt(mean_sq + RMS_EPS))
    return jnp.concatenate(outs, axis=1)


def _attn_kernel(q_ref, kc_ref, vc_ref, o_ref, lse_ref, kp_ref, vp_ref, *, parts):
    rows = q_ref.shape[-2]
    qb = parts * rows
    n = pl.program_id(1)

    @pl.when(n == 0)
    def _():
        kp_ref[...] = jnp.zeros_like(kp_ref)
        vp_ref[...] = jnp.zeros_like(vp_ref)

    def load(ref, hs):
        return ref[..., hs].reshape(qb, A_HEAD_DIM)

    def step_of(idx):
        return idx if parts == 1 else (idx % rows) * parts + idx // rows

    lq = step_of(lax.broadcasted_iota(jnp.int32, (qb, 2 * qb), 0))
    col = lax.broadcasted_iota(jnp.int32, (qb, 2 * qb), 1)
    is_prev = col < qb
    lk = step_of(jnp.where(is_prev, col, col - qb))
    dist = lq - lk + jnp.where(is_prev, qb, 0)
    ok = (dist >= 0) & (dist <= A_STEPS) & ((n > 0) | jnp.logical_not(is_prev))
    lane = lax.broadcasted_iota(jnp.int32, (qb, LANES), 1)
    scale = A_HEAD_DIM ** -0.5
    nt = (((1,), (1,)), ((), ()))
    ones = jnp.ones((2 * qb, A_HEAD_DIM), BF16)
    lse_tile = jnp.zeros((qb, LANES), F32)
    for hh in range(A_HEADS_PER_GROUP):
        hs = slice(hh * A_HEAD_DIM, (hh + 1) * A_HEAD_DIM)
        q = load(q_ref, hs)
        k_cat = jnp.concatenate([load(kp_ref, hs), load(kc_ref, hs)], axis=0)
        v_cat = jnp.concatenate([load(vp_ref, hs), load(vc_ref, hs)], axis=0)
        s = lax.dot_general(q, k_cat, nt, preferred_element_type=F32) * scale
        s = jnp.where(ok, s, -jnp.inf)
        mx = jnp.max(jnp.maximum(s[:, :qb], s[:, qb:]), axis=-1, keepdims=True)
        e = jnp.exp(s - mx).astype(BF16)
        pv = jnp.dot(e, jnp.concatenate([v_cat, ones], axis=1), preferred_element_type=F32)
        den = pv[:, A_HEAD_DIM:]
        o_ref[..., hs] = (pv[:, :A_HEAD_DIM] / den).reshape(o_ref.shape[:-1] + (A_HEAD_DIM,)).astype(o_ref.dtype)
        lse_tile = jnp.where(lane == hh, mx + jnp.log(den), lse_tile)
    lse_ref[...] = lse_tile.reshape(lse_ref.shape)
    kp_ref[...] = kc_ref[...]
    vp_ref[...] = vc_ref[...]


def dilated_attention(qk, v, group, dilation):
    seq = qk.shape[0]
    gw = A_GROUP_WIDTH
    n_groups = len(A_PATTERNS)
    parts = A_RESIDUES // dilation
    rpr = seq // A_RESIDUES
    rows = max(A_STEPS // parts, 16)
    nb = rpr // rows
    qk_w, v_w = qk.shape[1], v.shape[1]
    if parts == 1:
        vshape = lambda w: (seq, w)
        blk = lambda w: (rows, w)
        at = lambda r, n, c: (r * nb + n, c)
    else:
        vshape = lambda w: (parts, dilation, rpr, w)
        blk = lambda w: (parts, None, rows, w)
        at = lambda r, n, c: (0, r, n, c)
    view = lambda a: a.reshape(vshape(a.shape[1]))
    o, lse = pl.pallas_call(
        functools.partial(_attn_kernel, parts=parts),
        out_shape=(jax.ShapeDtypeStruct(vshape(gw), BF16),
                   jax.ShapeDtypeStruct(vshape(LANES), F32)),
        grid=(dilation, nb),
        in_specs=[pl.BlockSpec(blk(gw), lambda r, n: at(r, n, group)),
                  pl.BlockSpec(blk(gw), lambda r, n: at(r, n, n_groups + group)),
                  pl.BlockSpec(blk(gw), lambda r, n: at(r, n, group))],
        out_specs=(pl.BlockSpec(blk(gw), lambda r, n: at(r, n, 0)),
                   pl.BlockSpec(blk(LANES), lambda r, n: at(r, n, 0))),
        scratch_shapes=[pltpu.VMEM(tuple(d for d in blk(gw) if d is not None), BF16)] * 2,
        compiler_params=_params(2),
        name=f"dilated_attn_g{group}",
    )(view(qk), view(qk), view(v))
    return o.reshape(seq, gw), lse.reshape(seq, LANES)


def _attn_merge_kernel(o0_ref, o1_ref, o2_ref, l0_ref, l1_ref, l2_ref, out_ref):
    rows = o0_ref.shape[1]
    tm = A_RESIDUES * rows

    def flat(ref, sl=slice(None)):
        return ref[:, :, sl].reshape(tm, -1)

    l0, l1, l2 = flat(l0_ref), flat(l1_ref), flat(l2_ref)
    mx = jnp.maximum(jnp.maximum(l0, l1), l2)
    w0, w1, w2 = jnp.exp(l0 - mx), jnp.exp(l1 - mx), jnp.exp(l2 - mx)
    tot = w0 + w1 + w2
    w0, w1, w2 = w0 / tot, w1 / tot, w2 / tot
    to_sequence_order = _row_permutation(tm, rows, False)
    for hh in range(A_HEADS_PER_GROUP):
        hs = slice(hh * A_HEAD_DIM, (hh + 1) * A_HEAD_DIM)
        merged = (w0[:, hh:hh + 1] * flat(o0_ref, hs).astype(F32)
                  + w1[:, hh:hh + 1] * flat(o1_ref, hs).astype(F32)
                  + w2[:, hh:hh + 1] * flat(o2_ref, hs).astype(F32)).astype(BF16)
        out_ref[:, hs] = jnp.dot(to_sequence_order, merged, preferred_element_type=F32).astype(out_ref.dtype)


def attention_merge(outs, lses, rows=32):
    seq, gw = outs[0].shape
    rpr = seq // A_RESIDUES
    rows = min(rows, rpr)
    tm = A_RESIDUES * rows
    o_spec = pl.BlockSpec((A_RESIDUES, rows, gw), lambda i: (0, i, 0))
    l_spec = pl.BlockSpec((A_RESIDUES, rows, LANES), lambda i: (0, i, 0))
    return pl.pallas_call(
        _attn_merge_kernel,
        out_shape=jax.ShapeDtypeStruct((seq, gw), BF16),
        grid=(rpr // rows,),
        in_specs=[o_spec] * 3 + [l_spec] * 3,
        out_specs=pl.BlockSpec((tm, gw), lambda i: (i, 0)),
        compiler_params=_params(1),
        name="attn_merge",
    )(*[o.reshape(A_RESIDUES, rpr, gw) for o in outs],
      *[l.reshape(A_RESIDUES, rpr, LANES) for l in lses])


def dilated_attention_mixer(h, norm_gain, w_in, layer, q_gain, k_gain, w_out, rope):
    seq = h.shape[0]
    n_heads = len(A_PATTERNS) * A_HEADS_PER_GROUP
    qk_width = 2 * n_heads * A_HEAD_DIM
    v_width = n_heads * A_HEAD_DIM
    assert seq % (A_RESIDUES * A_STEPS) == 0
    assert all(w // d == A_STEPS and A_RESIDUES % d == 0 for w, d in A_PATTERNS)
    xn = rmsnorm_residue_major(h, norm_gain)
    cos, sin = rope
    per_head = lambda g_q, g_k: jnp.concatenate([jnp.tile(g_q, n_heads), jnp.tile(g_k, n_heads)]).reshape(1, qk_width)
    gain = per_head(q_gain, k_gain)
    gain_swapped = per_head(jnp.roll(q_gain, A_HEAD_DIM // 2), jnp.roll(k_gain, A_HEAD_DIM // 2))
    tm, tn = min(1024, seq), 1024
    assert (qk_width // 2) % tn == 0
    qk = fused_matmul(
        xn, w_in, layer, 0, qk_width, _ep_qk_norm_rope, out_dtype=BF16, tm=tm, tn=tn,
        extras=[(gain, (1, tn), lambda j, i: (0, j)),
                (gain_swapped, (1, tn), lambda j, i: (0, j)),
                (cos, (tm, A_HEAD_DIM), lambda j, i: (i, 0)),
                (sin, (tm, A_HEAD_DIM), lambda j, i: (i, 0))],
        name="attn_qk_proj")
    v = fused_matmul(xn, w_in, layer, qk_width, v_width, _ep_identity, out_dtype=BF16,
                     tm=tm, tn=tn, name="attn_v_proj")
    outs, lses = [], []
    for g, (_, dilation) in enumerate(A_PATTERNS):
        o, lse = dilated_attention(qk, v, g, dilation)
        outs.append(o)
        lses.append(lse)
    merged = attention_merge(outs, lses)
    return matmul_residual(merged, w_out, layer, h, tm=1024, name="attn_out_proj")


def _ep_gates(acc, cs, bias_ref):
    lane = lax.broadcasted_iota(jnp.int32, acc.shape, 1)
    g = jnp.where(lane < 2 * B_HEADS, acc + bias_ref[...], 0.0)
    return jnp.where(lane >= B_HEADS, jax.nn.log_sigmoid(g), g)


def _mlstm_kernel(q_ref, k_ref, v_ref, og_ref, gcol_ref, grow_ref, gain_ref, o_ref,
                  c_ref, n_ref, m_ref):
    @pl.when(pl.program_id(1) == 0)
    def _():
        c_ref[...] = jnp.zeros_like(c_ref)
        n_ref[...] = jnp.zeros_like(n_ref)
        m_ref[...] = jnp.zeros_like(m_ref)

    for hh in range(B_HEADS_PER_STEP):
        qs = slice(hh * B_QK_DIM, (hh + 1) * B_QK_DIM)
        vs = slice(hh * B_V_DIM, (hh + 1) * B_V_DIM)
        _mlstm_head(pl.program_id(0) * B_HEADS_PER_STEP + hh,
                    q_ref.at[:, qs], k_ref.at[:, qs], v_ref.at[:, vs], og_ref.at[:, vs], gcol_ref, grow_ref,
                    gain_ref.at[:, vs], o_ref.at[:, vs], c_ref.at[hh], n_ref.at[hh], m_ref.at[hh])


def _mlstm_head(hd, q_ref, k_ref, v_ref, og_ref, gcol_ref, grow_ref, gain_ref, o_ref, c_ref, n_ref, m_ref):
    L = q_ref.shape[0]
    gcol = gcol_ref[...]
    lane = lax.broadcasted_iota(jnp.int32, gcol.shape, 1)
    ig_col = jnp.sum(jnp.where(lane == hd, gcol, 0.0), axis=-1, keepdims=True)
    lf_col = jnp.sum(jnp.where(lane == hd + B_HEADS, gcol, 0.0), axis=-1, keepdims=True)
    ig_row = grow_ref[pl.ds(hd, 1), :]
    lf_row = grow_ref[pl.ds(hd + B_HEADS, 1), :]

    t_i = lax.broadcasted_iota(jnp.int32, (L, L), 0)
    s_i = lax.broadcasted_iota(jnp.int32, (L, L), 1)
    causal = s_i <= t_i
    lower = causal.astype(BF16)
    upper = (t_i <= s_i).astype(BF16)

    def split3(a):
        hi = a.astype(BF16).astype(F32)
        r1 = a - hi
        mid = r1.astype(BF16).astype(F32)
        return hi, mid, (r1 - mid).astype(BF16).astype(F32)

    hi, mid, lo = split3(lf_col)
    parts_col = jnp.where(lane == 0, hi, jnp.where(lane == 1, mid, jnp.where(lane == 2, lo, 0.0))).astype(BF16)
    cum = jnp.dot(lower, parts_col, preferred_element_type=F32)
    b_col = cum[:, 0:1] + cum[:, 1:2] + cum[:, 2:3]
    hi, mid, lo = split3(lf_row)
    sub = lax.broadcasted_iota(jnp.int32, (8, L), 0)
    parts_row = jnp.where(sub == 0, hi, jnp.where(sub == 1, mid, jnp.where(sub == 2, lo, 0.0))).astype(BF16)
    cum = jnp.dot(parts_row, upper, preferred_element_type=F32)
    b_row = cum[0:1, :] + cum[1:2, :] + cum[2:3, :]

    m_prev = m_ref[:1, :1]
    c_prev = c_ref[...]
    n_prev = n_ref[...]

    q = q_ref[...] * jnp.asarray(B_QK_DIM ** -0.5, q_ref.dtype)
    k = k_ref[...]
    v = v_ref[...]

    log_d = jnp.where(causal, b_col - b_row + ig_row, -jnp.inf)
    log_inter = b_col + m_prev
    m_t = jnp.maximum(log_inter, jnp.max(log_d, axis=-1, keepdims=True))
    scores = lax.dot_general(q, k, (((1,), (1,)), ((), ())), preferred_element_type=F32)
    w = scores * jnp.exp(log_d - m_t)
    decay = jnp.exp(log_inter - m_t)
    num = (decay * jnp.dot(q, c_prev.astype(BF16), preferred_element_type=F32)
           + jnp.dot(w.astype(BF16), v, preferred_element_type=F32))
    qn = (decay * jnp.sum(q.astype(F32) * n_prev, axis=-1, keepdims=True)
          + jnp.sum(w, axis=-1, keepdims=True))
    hval = num / jnp.maximum(jnp.abs(qn), jnp.exp(-m_t))

    inv = lax.rsqrt(jnp.mean(hval * hval, axis=-1, keepdims=True) + RMS_EPS)
    o_ref[...] = (hval * inv * gain_ref[...] * _sigmoid(og_ref[...].astype(F32))).astype(o_ref.dtype)

    b_last = b_col[L - 1:L, :]
    log_w = b_last - b_col + ig_col
    m_new = jnp.maximum(b_last + m_prev, jnp.max(log_w, axis=0, keepdims=True))
    w_s = jnp.exp(log_w - m_new)
    carry = jnp.exp(b_last + m_prev - m_new)
    k_w = k.astype(F32) * w_s
    c_ref[...] = carry * c_prev + lax.dot_general(
        k_w.astype(BF16), v, (((0,), (0,)), ((), ())), preferred_element_type=F32)
    n_ref[...] = carry * n_prev + jnp.sum(k_w, axis=0, keepdims=True)
    m_ref[...] = jnp.broadcast_to(m_new, m_ref.shape)


def mlstm_mixer(h, hb, norm_gain, w_in, layer, gate_bias, h_gain, w_out):
    seq, d = h.shape
    qk_w = B_HEADS * B_QK_DIM
    v_w = B_HEADS * B_V_DIM
    main_w = 2 * qk_w + v_w + d
    proj = fused_matmul(hb, w_in, layer, 0, main_w, _ep_identity, out_dtype=BF16, norm_gain=norm_gain,
                        name="mlstm_in_proj")
    n_gate = 2 * B_HEADS
    assert w_in.shape[2] == main_w + n_gate
    bias = jnp.pad(gate_bias, (0, B_GATE_PAD - n_gate)).reshape(1, B_GATE_PAD)
    gates = fused_matmul(hb, w_in, layer, main_w, B_GATE_PAD, _ep_gates, out_dtype=F32, tn=B_GATE_PAD,
                         norm_gain=norm_gain, extras=[(bias, (1, B_GATE_PAD), lambda j, i: (0, 0))],
                         name="mlstm_gate_proj")
    gates_t = gates[:, :n_gate].T
    L = min(B_CHUNK, seq)
    hps = B_HEADS_PER_STEP
    qk_blk, v_blk = hps * B_QK_DIM, hps * B_V_DIM
    nq = qk_w // qk_blk
    nv = v_w // v_blk
    mixed = pl.pallas_call(
        _mlstm_kernel,
        out_shape=jax.ShapeDtypeStruct((seq, v_w), BF16),
        grid=(B_HEADS // hps, seq // L),
        in_specs=[pl.BlockSpec((L, qk_blk), lambda g, c: (c, g)),
                  pl.BlockSpec((L, qk_blk), lambda g, c: (c, nq + g)),
                  pl.BlockSpec((L, v_blk), lambda g, c: (c, (2 * qk_w) // v_blk + g)),
                  pl.BlockSpec((L, v_blk), lambda g, c: (c, (2 * qk_w) // v_blk + nv + g)),
                  pl.BlockSpec((L, B_GATE_PAD), lambda g, c: (c, 0)),
                  pl.BlockSpec((n_gate, L), lambda g, c: (0, c)),
                  pl.BlockSpec((1, v_blk), lambda g, c: (0, g))],
        out_specs=pl.BlockSpec((L, v_blk), lambda g, c: (c, g)),
        scratch_shapes=[pltpu.VMEM((hps, B_QK_DIM, B_V_DIM), F32),
                        pltpu.VMEM((hps, 1, B_QK_DIM), F32),
                        pltpu.VMEM((hps, 8, LANES), F32)],
        compiler_params=_params(2),
        name="mlstm_chunks",
    )(proj, proj, proj, proj, gates, gates_t, h_gain.reshape(1, v_w))
    return matmul_residual(mixed, w_out, layer, h, name="mlstm_out_proj")


def _pool_kernel(u_ref, halo_ref, wg_ref, scale_ref, o_ref, wb_ref):
    g = pl.program_id(0)
    i = pl.program_id(1)
    tm = u_ref.shape[0]

    @pl.when(i == 0)
    def _():
        wb_ref[...] = wg_ref[...].astype(BF16)

    win = jnp.left_shift(jnp.int32(C_WINDOWS[0]), g)
    t_i = lax.broadcasted_iota(jnp.int32, (tm, tm), 0)
    s_i = lax.broadcasted_iota(jnp.int32, (tm, tm), 1)
    dist = t_i - s_i
    band = ((dist >= 0) & (dist < win)).astype(BF16)
    t_h = lax.broadcasted_iota(jnp.int32, (tm, C_HALO), 0)
    s_h = lax.broadcasted_iota(jnp.int32, (tm, C_HALO), 1)
    band_halo = ((t_h + C_HALO - s_h < win) & (i > 0)).astype(BF16)
    u = u_ref[...]
    total = (jnp.dot(band, u, preferred_element_type=F32)
             + jnp.dot(band_halo, halo_ref[...], preferred_element_type=F32))
    t_glob = i * tm + lax.broadcasted_iota(jnp.int32, (tm, 1), 0)
    count = jnp.minimum(t_glob + 1, win).astype(F32)
    pooled = total / count - u.astype(F32)
    y = jnp.dot(pooled.astype(BF16), wb_ref[...], preferred_element_type=F32) * scale_ref[...]
    o_ref[...] = y.astype(o_ref.dtype)


def pooling_mixer(h, hb, norm_gain, w_in, layer, w_group, scale, w_out, tm=512):
    seq, d = h.shape
    assert all(C_WINDOWS[g] == C_WINDOWS[0] << g for g in range(len(C_WINDOWS)))
    assert max(C_WINDOWS) <= C_HALO
    u = fused_matmul(hb, w_in, layer, 0, d, _ep_identity, out_dtype=BF16, norm_gain=norm_gain,
                     name="pool_in_proj")
    tm = min(tm, seq)
    gd = C_GROUP_DIM
    halo_per_tile = tm // C_HALO
    y = pl.pallas_call(
        _pool_kernel,
        out_shape=jax.ShapeDtypeStruct((seq, d), BF16),
        grid=(len(C_WINDOWS), seq // tm),
        in_specs=[pl.BlockSpec((tm, gd), lambda g, i: (i, g)),
                  pl.BlockSpec((C_HALO, gd), lambda g, i: (jnp.maximum(i * halo_per_tile - 1, 0), g)),
                  pl.BlockSpec((None, None, gd, gd), lambda g, i: (layer, g, 0, 0)),
                  pl.BlockSpec((1, gd), lambda g, i: (0, g))],
        out_specs=pl.BlockSpec((tm, gd), lambda g, i: (i, g)),
        scratch_shapes=[pltpu.VMEM((gd, gd), BF16)],
        compiler_params=_params(2),
        name="pool_group",
    )(u, u, w_group, scale.reshape(1, d))
    return matmul_residual(y, w_out, layer, h, name="pool_out_proj")


def squared_relu_mlp(h, hb, norm_gain, w_in, w_out, layer):
    hidden, w_out_bf16 = fused_matmul(hb, w_in, layer, 0, w_in.shape[2], _ep_relu_sq, out_dtype=BF16,
                                      norm_gain=norm_gain, side=(w_out, layer), name="mlp_up")
    return matmul_kgrid_residual(hidden, w_out_bf16, h, name="mlp_down")


def kernel(x, p, norm_mix, norm_mlp, norm_ple, w_ple, w_ple_gate, w_mlp_in, w_mlp_out, a_w_in, a_q_norm, a_k_norm, a_w_out, b_w_in, b_gate_bias, b_h_norm, b_w_out, c_w_in, c_w_group, c_scale, c_w_out):
    bsz, seq, d = x.shape
    depth = p.shape[0]
    rope = rope_tables(seq)
    outs = []
    for b in range(bsz):
        h = x.reshape(seq, d) if bsz == 1 else x[b]
        hb = None
        for i in range(depth):
            kind, j = i % 3, i // 3
            if kind == 0:
                h, hb = dilated_attention_mixer(h, norm_mix[i], a_w_in, j, a_q_norm[j], a_k_norm[j], a_w_out, rope)
            elif kind == 1:
                h, hb = mlstm_mixer(h, hb, norm_mix[i], b_w_in, j, b_gate_bias[j], b_h_norm[j], b_w_out)
            else:
                h, hb = pooling_mixer(h, hb, norm_mix[i], c_w_in, j, c_w_group, c_scale[j], c_w_out)
            h, hb = squared_relu_mlp(h, hb, norm_mlp[i], w_mlp_in, w_mlp_out, i)
            h, hb = ple_update(h, hb, norm_ple[i], w_ple_gate, i,
                               p[i, b].astype(BF16), w_ple[i].astype(BF16))
        outs.append(h)
    return outs[0][None] if bsz == 1 else jnp.stack(outs, axis=0)
```

```python
import functools
import math

import jax
import jax.numpy as jnp
from jax import lax
from jax.experimental import pallas as pl
from jax.experimental.pallas import tpu as pltpu

F32 = jnp.float32
BF16 = jnp.bfloat16

RMS_EPS = 1e-6
ROPE_THETA = 10000.0

A_HEAD_DIM = 128
A_HEADS_PER_GROUP = 16
A_PATTERNS = ((128, 1), (512, 4), (2048, 16))
A_STEPS = 128
A_RESIDUES = 16
A_GROUP_WIDTH = A_HEADS_PER_GROUP * A_HEAD_DIM

B_HEADS = 8
B_QK_DIM = 256
B_V_DIM = 512
B_CHUNK = 256
B_HEADS_PER_STEP = 4
B_GATE_PAD = 128

C_WINDOWS = (2, 4, 8, 16)
C_GROUP_DIM = 1024
C_HALO = 128

LANES = 128
VMEM_LIMIT_BYTES = 60 * 1024 * 1024


def _sigmoid(x):
    return 0.5 * jnp.tanh(0.5 * x) + 0.5


def _params(n_grid_axes):
    return pltpu.CompilerParams(
        dimension_semantics=("arbitrary",) * n_grid_axes,
        vmem_limit_bytes=VMEM_LIMIT_BYTES)


def _rmsnorm_rows(x, gain):
    inv = lax.rsqrt(jnp.mean(x * x, axis=-1, keepdims=True) + RMS_EPS)
    return x * inv * gain


def _row_permutation(tm, rows, to_residue_major):
    a = lax.broadcasted_iota(jnp.int32, (tm, tm), 0)
    b = lax.broadcasted_iota(jnp.int32, (tm, tm), 1)
    major, seq_order = (a, b) if to_residue_major else (b, a)
    return (seq_order == (major % rows) * A_RESIDUES + major // rows).astype(BF16)


def _rmsnorm_residue_major_kernel(x_ref, g_ref, o_ref):
    tm = x_ref.shape[0]
    rows = tm // A_RESIDUES
    y = _rmsnorm_rows(x_ref[...].astype(F32), g_ref[...]).astype(BF16)
    y = jnp.dot(_row_permutation(tm, rows, True), y, preferred_element_type=F32)
    o_ref[...] = y.astype(o_ref.dtype).reshape(o_ref.shape)


def rmsnorm_residue_major(h, gain, tm=512):
    m, d = h.shape
    tm = min(tm, m)
    in_specs = [pl.BlockSpec((tm, d), lambda i: (i, 0)),
                pl.BlockSpec((1, d), lambda i: (0, 0))]
    out = pl.pallas_call(
        _rmsnorm_residue_major_kernel,
        out_shape=jax.ShapeDtypeStruct((A_RESIDUES, m // A_RESIDUES, d), BF16),
        grid=(m // tm,), in_specs=in_specs,
        out_specs=pl.BlockSpec((A_RESIDUES, tm // A_RESIDUES, d), lambda i: (0, i, 0)),
        compiler_params=_params(1), name="rmsnorm_residue_major",
    )(h, gain.reshape(1, d))
    return out.reshape(m, d)


SIDE_CAST_ROWS = 64
NORM_ROWS = 64


def _row_inv_rms(x_ref):
    tm, k = x_ref.shape
    rows = min(NORM_ROWS, tm)
    pieces = []
    for r in range(tm // rows):
        part = None
        for c in range(k // LANES):
            xc = x_ref[r * rows:(r + 1) * rows, c * LANES:(c + 1) * LANES].astype(F32)
            part = xc * xc if part is None else part + xc * xc
        pieces.append(jnp.sum(part, axis=-1, keepdims=True))
    return lax.rsqrt(jnp.concatenate(pieces, axis=0) * (1.0 / k) + RMS_EPS)


def _mm_kernel(x_ref, wchunk_ref, *rest, epilogue, n_extra, n_out, n_slabs, n_side_steps, normed):
    if normed:
        gcol_ref, rest = rest[0], rest[1:]
    extra_refs, rest = rest[:n_extra], rest[n_extra:]
    if n_side_steps:
        side_ref, rest = rest[0], rest[1:]
    o_refs, rest = rest[:n_out], rest[n_out:]
    if n_side_steps:
        side_out_ref, rest = rest[0], rest[1:]
    if normed:
        wb_refs, inv_ref = rest[:2], rest[2]
    else:
        wb_refs = rest
    jj, i = pl.program_id(0), pl.program_id(1)
    chunk = wchunk_ref.shape[0]

    def cast_jobs(dst_ref):
        row0 = pl.multiple_of(i * chunk, chunk)
        wc = wchunk_ref[...]
        if normed:
            wc = wc * gcol_ref[...]
        dst_ref[pl.ds(row0, chunk), :] = wc.astype(BF16)
        if n_side_steps:
            side_out_ref[...] = side_ref[...].astype(BF16)

    @pl.when(jj == 0)
    def _():
        cast_jobs(wb_refs[0])

    if normed:
        tile_lane = lax.broadcasted_iota(jnp.int32, inv_ref.shape, 1) == i

        @pl.when((jj == 0) & (i == 0))
        def _():
            inv_ref[...] = jnp.zeros_like(inv_ref)

        @pl.when(jj == 1)
        def _():
            inv_ref[...] = jnp.where(tile_lane, _row_inv_rms(x_ref), inv_ref[...])

    def multiply(cur_ref, next_ref):
        cast_jobs(next_ref)
        acc = jnp.dot(x_ref[...], cur_ref[...], preferred_element_type=F32)
        if normed:
            acc = acc * jnp.sum(jnp.where(tile_lane, inv_ref[...], 0.0), axis=-1, keepdims=True)
        vals = epilogue(acc, slice(None), *extra_refs)
        vals = vals if isinstance(vals, tuple) else (vals,)
        for o_ref, val in zip(o_refs, vals, strict=True):
            o_ref[...] = val.astype(o_ref.dtype)

    @pl.when(jj % 2 == 1)
    def _():
        multiply(wb_refs[0], wb_refs[1])

    @pl.when((jj > 0) & (jj % 2 == 0))
    def _():
        multiply(wb_refs[1], wb_refs[0])


def fused_matmul(x, w, layer, col0, n, epilogue, *, out_dtype, tm=1024, tn=1024, extras=(), side=None,
                 norm_gain=None, name):
    m, k = x.shape
    tm, tn = min(tm, m), min(tn, n)
    n_slabs, n_rows = n // tn, m // tm
    assert w.shape[1] == k and m % tm == 0 and n % tn == 0 and col0 % tn == 0 and k % n_rows == 0, (
        w.shape, m, n, tm, tn, col0)
    jb = col0 // tn
    chunk = k // n_rows
    out_dtypes = out_dtype if isinstance(out_dtype, tuple) else (out_dtype,)

    def lag(im):
        return lambda jj, i: im(jnp.maximum(jj - 1, 0), jnp.where(jj > 0, i, 0))

    chunk_row = lambda jj, i: jnp.where(jj < n_slabs, i, n_rows - 1)
    in_specs = [pl.BlockSpec((tm, k), lag(lambda j, i: (i, 0))),
                pl.BlockSpec((None, chunk, tn),
                             lambda jj, i: (layer, chunk_row(jj, i), jb + jnp.minimum(jj, n_slabs - 1)))]
    operands = [x, w]
    if norm_gain is not None:
        assert n_rows <= LANES
        in_specs.append(pl.BlockSpec((chunk, 1), lambda jj, i: (chunk_row(jj, i), 0)))
        operands.append(norm_gain.reshape(k, 1))
    in_specs += [pl.BlockSpec(bs, lag(im)) for _, bs, im in extras]
    operands += [a for a, _, _ in extras]
    out_shape = [jax.ShapeDtypeStruct((m, n), dt) for dt in out_dtypes]
    out_specs = [pl.BlockSpec((tm, tn), lag(lambda j, i: (i, j))) for _ in out_dtypes]
    n_side_steps = 0
    if side is not None:
        side_w, side_layer = side
        side_rows, side_cols = side_w.shape[1:]
        rows_per_step = SIDE_CAST_ROWS
        while side_rows // rows_per_step > (n_slabs + 1) * n_rows:
            rows_per_step *= 2
        n_side_steps = side_rows // rows_per_step
        assert side_rows % rows_per_step == 0
        side_block = lambda jj, i: jnp.minimum(jj * n_rows + i, n_side_steps - 1)
        in_specs.append(pl.BlockSpec((None, rows_per_step, side_cols),
                                     lambda jj, i: (side_layer, side_block(jj, i), 0)))
        operands.append(side_w)
        out_shape.append(jax.ShapeDtypeStruct((side_rows, side_cols), BF16))
        out_specs.append(pl.BlockSpec((rows_per_step, side_cols), lambda jj, i: (side_block(jj, i), 0)))
    outs = pl.pallas_call(
        functools.partial(_mm_kernel, epilogue=epilogue, n_extra=len(extras), n_out=len(out_dtypes),
                          n_slabs=n_slabs, n_side_steps=n_side_steps, normed=norm_gain is not None),
        out_shape=out_shape,
        grid=(n_slabs + 1, n_rows),
        in_specs=in_specs,
        out_specs=out_specs,
        scratch_shapes=[pltpu.VMEM((k, tn), BF16), pltpu.VMEM((k, tn), BF16)] + (
            [pltpu.VMEM((tm, LANES), F32)] if norm_gain is not None else []),
        compiler_params=_params(2),
        name=name,
    )(*operands)
    return outs[0] if len(outs) == 1 else tuple(outs)


def _ep_identity(acc, cs):
    return acc


def _ep_relu_sq(acc, cs):
    r = jnp.maximum(acc, 0.0)
    return r * r


def _ep_residual(acc, cs, res_ref):
    h = res_ref[:, cs] + acc
    return h, h


def matmul_residual(x, w, layer, res, *, tm=512, tn=1024, name):
    n = w.shape[2]
    tm, tn = min(tm, x.shape[0]), min(tn, n)
    return fused_matmul(x, w, layer, 0, n, _ep_residual, out_dtype=(F32, BF16), tm=tm, tn=tn,
                        extras=[(res, (tm, tn), lambda j, i: (i, j))], name=name)


def _mm_kgrid_kernel(x_ref, w_ref, res_ref, o_ref, ob_ref, acc_ref):
    kk = pl.program_id(2)
    band = res_ref.shape[0]

    @pl.when(kk == 0)
    def _():
        acc_ref[...] = jnp.zeros_like(acc_ref)

    acc_ref[...] += jnp.dot(x_ref[...], w_ref[...], preferred_element_type=F32)
    row0 = pl.multiple_of(kk * band, band)
    acc_ref[pl.ds(row0, band), :] += res_ref[...]

    @pl.when(kk == pl.num_programs(2) - 1)
    def _():
        h = acc_ref[...]
        o_ref[...] = h
        ob_ref[...] = h.astype(ob_ref.dtype)


def matmul_kgrid_residual(x, w, res, *, tm=1024, tn=1024, tk=4096, name):
    m, k = x.shape
    n = w.shape[1]
    tm, tn, tk = min(tm, m), min(tn, n), min(tk, k)
    nk = k // tk
    assert m % tm == 0 and n % tn == 0 and k % tk == 0 and tm % (8 * nk) == 0
    return pl.pallas_call(
        _mm_kgrid_kernel,
        out_shape=(jax.ShapeDtypeStruct((m, n), F32), jax.ShapeDtypeStruct((m, n), BF16)),
        grid=(m // tm, n // tn, k // tk),
        in_specs=[pl.BlockSpec((tm, tk), lambda i, j, kk: (i, kk)),
                  pl.BlockSpec((tk, tn), lambda i, j, kk: (kk, j)),
                  pl.BlockSpec((tm // nk, tn), lambda i, j, kk: (i * nk + kk, j))],
        out_specs=(pl.BlockSpec((tm, tn), lambda i, j, kk: (i, j)),
                   pl.BlockSpec((tm, tn), lambda i, j, kk: (i, j))),
        scratch_shapes=[pltpu.VMEM((tm, tn), F32)],
        compiler_params=_params(3),
        name=name,
    )(x, w, res)


def _ep_ple(acc, cs, p_ref, wp_ref, res_ref):
    emb = jnp.dot(p_ref[...], wp_ref[:, cs], preferred_element_type=F32)
    h = res_ref[:, cs] + _sigmoid(acc) * emb
    return h, h


def ple_update(h, hb, norm_gain, w_gate, layer, p, w_ple, *, tm=512, tn=1024):
    n = w_gate.shape[2]
    tm, tn = min(tm, h.shape[0]), min(tn, n)
    kp = p.shape[1]
    return fused_matmul(
        hb, w_gate, layer, 0, n, _ep_ple, out_dtype=(F32, BF16), tm=tm, tn=tn, norm_gain=norm_gain,
        extras=[(p, (tm, kp), lambda j, i: (i, 0)),
                (w_ple, (kp, tn), lambda j, i: (0, j)),
                (h, (tm, tn), lambda j, i: (i, j))],
        name="ple_gate")


def _rope_table_kernel(cos_ref, sin_ref, *, rows_per_residue):
    tm = cos_ref.shape[0]
    half = A_HEAD_DIM // 2
    row0 = pl.program_id(0) * tm
    residue = row0 // rows_per_residue
    n0 = row0 % rows_per_residue
    n = n0 + lax.broadcasted_iota(jnp.int32, (tm, A_HEAD_DIM), 0)
    pos = (n * A_RESIDUES + residue).astype(F32)
    lane = lax.broadcasted_iota(jnp.int32, (tm, A_HEAD_DIM), 1)
    j = jnp.where(lane >= half, lane - half, lane).astype(F32)
    inv_freq = jnp.exp(j * (-2.0 * math.log(ROPE_THETA) / A_HEAD_DIM))
    ang = pos * inv_freq
    cos_ref[...] = jnp.cos(ang)
    sin_ref[...] = jnp.where(lane >= half, 1.0, -1.0) * jnp.sin(ang)


def rope_tables(seq, tm=256):
    rows_per_residue = seq // A_RESIDUES
    tm = min(tm, rows_per_residue)
    assert rows_per_residue % tm == 0
    spec = pl.BlockSpec((tm, A_HEAD_DIM), lambda i: (i, 0))
    shape = jax.ShapeDtypeStruct((seq, A_HEAD_DIM), F32)
    return pl.pallas_call(
        functools.partial(_rope_table_kernel, rows_per_residue=rows_per_residue),
        out_shape=(shape, shape), grid=(seq // tm,),
        in_specs=[], out_specs=(spec, spec), compiler_params=_params(1),
        name="rope_tables")()


def _ep_qk_norm_rope(acc, cs, gain_ref, gain_swapped_ref, cos_ref, sin_ref):
    d = A_HEAD_DIM
    cos_g = cos_ref[...] * gain_ref[:, :d]
    sin_g = sin_ref[...] * gain_swapped_ref[:, :d]
    mean_mat = jnp.full((d, d), 1.0 / d, BF16)
    r_i = lax.broadcasted_iota(jnp.int32, (d, d), 0)
    c_i = lax.broadcasted_iota(jnp.int32, (d, d), 1)
    swap_halves = (r_i == (c_i + d // 2) % d).astype(BF16)
    outs = []
    for hh in range(acc.shape[1] // d):
        a = acc[:, hh * d:(hh + 1) * d]
        mean_sq = jnp.dot((a * a).astype(BF16), mean_mat, preferred_element_type=F32)
        a_swapped = jnp.dot(a.astype(BF16), swap_halves, preferred_element_type=F32)
        outs.append((a * cos_g + a_swapped * sin_g) * lax.rsqrt(mean_sq + RMS_EPS))
    return jnp.concatenate(outs, axis=1)


def _attn_kernel(q_ref, kc_ref, vc_ref, o_ref, lse_ref, kp_ref, vp_ref, *, parts):
    rows = q_ref.shape[-2]
    qb = parts * rows
    n = pl.program_id(1)

    @pl.when(n == 0)
    def _():
        kp_ref[...] = jnp.zeros_like(kp_ref)
        vp_ref[...] = jnp.zeros_like(vp_ref)

    def load(ref, hs):
        return ref[..., hs].reshape(qb, A_HEAD_DIM)

    def step_of(idx):
        return idx if parts == 1 else (idx % rows) * parts + idx // rows

    lq = step_of(lax.broadcasted_iota(jnp.int32, (qb, 2 * qb), 0))
    col = lax.broadcasted_iota(jnp.int32, (qb, 2 * qb), 1)
    is_prev = col < qb
    lk = step_of(jnp.where(is_prev, col, col - qb))
    dist = lq - lk + jnp.where(is_prev, qb, 0)
    ok = (dist >= 0) & (dist <= A_STEPS) & ((n > 0) | jnp.logical_not(is_prev))
    lane = lax.broadcasted_iota(jnp.int32, (qb, LANES), 1)
    scale = A_HEAD_DIM ** -0.5
    nt = (((1,), (1,)), ((), ()))
    ones = jnp.ones((2 * qb, A_HEAD_DIM), BF16)
    lse_tile = jnp.zeros((qb, LANES), F32)
    for hh in range(A_HEADS_PER_GROUP):
        hs = slice(hh * A_HEAD_DIM, (hh + 1) * A_HEAD_DIM)
        q = load(q_ref, hs)
        k_cat = jnp.concatenate([load(kp_ref, hs), load(kc_ref, hs)], axis=0)
        v_cat = jnp.concatenate([load(vp_ref, hs), load(vc_ref, hs)], axis=0)
        s = lax.dot_general(q, k_cat, nt, preferred_element_type=F32) * scale
        s = jnp.where(ok, s, -jnp.inf)
        mx = jnp.max(jnp.maximum(s[:, :qb], s[:, qb:]), axis=-1, keepdims=True)
        e = jnp.exp(s - mx).astype(BF16)
        pv = jnp.dot(e, jnp.concatenate([v_cat, ones], axis=1), preferred_element_type=F32)
        den = pv[:, A_HEAD_DIM:]
        o_ref[..., hs] = (pv[:, :A_HEAD_DIM] / den).reshape(o_ref.shape[:-1] + (A_HEAD_DIM,)).astype(o_ref.dtype)
        lse_tile = jnp.where(lane == hh, mx + jnp.log(den), lse_tile)
    lse_ref[...] = lse_tile.reshape(lse_ref.shape)
    kp_ref[...] = kc_ref[...]
    vp_ref[...] = vc_ref[...]


def dilated_attention(qk, v, group, dilation):
    seq = qk.shape[0]
    gw = A_GROUP_WIDTH
    n_groups = len(A_PATTERNS)
    parts = A_RESIDUES // dilation
    rpr = seq // A_RESIDUES
    rows = max(A_STEPS // parts, 16)
    nb = rpr // rows
    qk_w, v_w = qk.shape[1], v.shape[1]
    if parts == 1:
        vshape = lambda w: (seq, w)
        blk = lambda w: (rows, w)
        at = lambda r, n, c: (r * nb + n, c)
    else:
        vshape = lambda w: (parts, dilation, rpr, w)
        blk = lambda w: (parts, None, rows, w)
        at = lambda r, n, c: (0, r, n, c)
    view = lambda a: a.reshape(vshape(a.shape[1]))
    o, lse = pl.pallas_call(
        functools.partial(_attn_kernel, parts=parts),
        out_shape=(jax.ShapeDtypeStruct(vshape(gw), BF16),
                   jax.ShapeDtypeStruct(vshape(LANES), F32)),
        grid=(dilation, nb),
        in_specs=[pl.BlockSpec(blk(gw), lambda r, n: at(r, n, group)),
                  pl.BlockSpec(blk(gw), lambda r, n: at(r, n, n_groups + group)),
                  pl.BlockSpec(blk(gw), lambda r, n: at(r, n, group))],
        out_specs=(pl.BlockSpec(blk(gw), lambda r, n: at(r, n, 0)),
                   pl.BlockSpec(blk(LANES), lambda r, n: at(r, n, 0))),
        scratch_shapes=[pltpu.VMEM(tuple(d for d in blk(gw) if d is not None), BF16)] * 2,
        compiler_params=_params(2),
        name=f"dilated_attn_g{group}",
    )(view(qk), view(qk), view(v))
    return o.reshape(seq, gw), lse.reshape(seq, LANES)


def _attn_merge_kernel(o0_ref, o1_ref, o2_ref, l0_ref, l1_ref, l2_ref, out_ref):
    rows = o0_ref.shape[1]
    tm = A_RESIDUES * rows

    def flat(ref, sl=slice(None)):
        return ref[:, :, sl].reshape(tm, -1)

    l0, l1, l2 = flat(l0_ref), flat(l1_ref), flat(l2_ref)
    mx = jnp.maximum(jnp.maximum(l0, l1), l2)
    w0, w1, w2 = jnp.exp(l0 - mx), jnp.exp(l1 - mx), jnp.exp(l2 - mx)
    tot = w0 + w1 + w2
    w0, w1, w2 = w0 / tot, w1 / tot, w2 / tot
    to_sequence_order = _row_permutation(tm, rows, False)
    for hh in range(A_HEADS_PER_GROUP):
        hs = slice(hh * A_HEAD_DIM, (hh + 1) * A_HEAD_DIM)
        merged = (w0[:, hh:hh + 1] * flat(o0_ref, hs).astype(F32)
                  + w1[:, hh:hh + 1] * flat(o1_ref, hs).astype(F32)
                  + w2[:, hh:hh + 1] * flat(o2_ref, hs).astype(F32)).astype(BF16)
        out_ref[:, hs] = jnp.dot(to_sequence_order, merged, preferred_element_type=F32).astype(out_ref.dtype)


def attention_merge(outs, lses, rows=32):
    seq, gw = outs[0].shape
    rpr = seq // A_RESIDUES
    rows = min(rows, rpr)
    tm = A_RESIDUES * rows
    o_spec = pl.BlockSpec((A_RESIDUES, rows, gw), lambda i: (0, i, 0))
    l_spec = pl.BlockSpec((A_RESIDUES, rows, LANES), lambda i: (0, i, 0))
    return pl.pallas_call(
        _attn_merge_kernel,
        out_shape=jax.ShapeDtypeStruct((seq, gw), BF16),
        grid=(rpr // rows,),
        in_specs=[o_spec] * 3 + [l_spec] * 3,
        out_specs=pl.BlockSpec((tm, gw), lambda i: (i, 0)),
        compiler_params=_params(1),
        name="attn_merge",
    )(*[o.reshape(A_RESIDUES, rpr, gw) for o in outs],
      *[l.reshape(A_RESIDUES, rpr, LANES) for l in lses])


def dilated_attention_mixer(h, hb, norm_gain, w_in, layer, q_gain, k_gain, w_out, rope):
    seq = h.shape[0]
    n_heads = len(A_PATTERNS) * A_HEADS_PER_GROUP
    qk_width = 2 * n_heads * A_HEAD_DIM
    v_width = n_heads * A_HEAD_DIM
    assert seq % (A_RESIDUES * A_STEPS) == 0
    assert all(w // d == A_STEPS and A_RESIDUES % d == 0 for w, d in A_PATTERNS)
    xn = rmsnorm_residue_major(h if hb is None else hb, norm_gain)
    cos, sin = rope
    per_head = lambda g_q, g_k: jnp.concatenate([jnp.tile(g_q, n_heads), jnp.tile(g_k, n_heads)]).reshape(1, qk_width)
    gain = per_head(q_gain, k_gain)
    gain_swapped = per_head(jnp.roll(q_gain, A_HEAD_DIM // 2), jnp.roll(k_gain, A_HEAD_DIM // 2))
    tm, tn = min(1024, seq), 1024
    assert (qk_width // 2) % tn == 0
    qk = fused_matmul(
        xn, w_in, layer, 0, qk_width, _ep_qk_norm_rope, out_dtype=BF16, tm=tm, tn=tn,
        extras=[(gain, (1, tn), lambda j, i: (0, j)),
                (gain_swapped, (1, tn), lambda j, i: (0, j)),
                (cos, (tm, A_HEAD_DIM), lambda j, i: (i, 0)),
                (sin, (tm, A_HEAD_DIM), lambda j, i: (i, 0))],
        name="attn_qk_proj")
    v = fused_matmul(xn, w_in, layer, qk_width, v_width, _ep_identity, out_dtype=BF16,
                     tm=tm, tn=tn, name="attn_v_proj")
    outs, lses = [], []
    for g, (_, dilation) in enumerate(A_PATTERNS):
        o, lse = dilated_attention(qk, v, g, dilation)
        outs.append(o)
        lses.append(lse)
    merged = attention_merge(outs, lses)
    return matmul_residual(merged, w_out, layer, h, tm=1024, name="attn_out_proj")


def _ep_gates(acc, cs, bias_ref):
    lane = lax.broadcasted_iota(jnp.int32, acc.shape, 1)
    g = jnp.where(lane < 2 * B_HEADS, acc + bias_ref[...], 0.0)
    return jnp.where(lane >= B_HEADS, jax.nn.log_sigmoid(g), g)


def _mlstm_kernel(q_ref, k_ref, v_ref, og_ref, gcol_ref, grow_ref, gain_ref, o_ref,
                  c_ref, n_ref, m_ref):
    @pl.when(pl.program_id(1) == 0)
    def _():
        c_ref[...] = jnp.zeros_like(c_ref)
        n_ref[...] = jnp.zeros_like(n_ref)
        m_ref[...] = jnp.zeros_like(m_ref)

    for hh in range(B_HEADS_PER_STEP):
        qs = slice(hh * B_QK_DIM, (hh + 1) * B_QK_DIM)
        vs = slice(hh * B_V_DIM, (hh + 1) * B_V_DIM)
        _mlstm_head(pl.program_id(0) * B_HEADS_PER_STEP + hh,
                    q_ref.at[:, qs], k_ref.at[:, qs], v_ref.at[:, vs], og_ref.at[:, vs], gcol_ref, grow_ref,
                    gain_ref.at[:, vs], o_ref.at[:, vs], c_ref.at[hh], n_ref.at[hh], m_ref.at[hh])


def _mlstm_head(hd, q_ref, k_ref, v_ref, og_ref, gcol_ref, grow_ref, gain_ref, o_ref, c_ref, n_ref, m_ref):
    L = q_ref.shape[0]
    gcol = gcol_ref[...]
    lane = lax.broadcasted_iota(jnp.int32, gcol.shape, 1)
    ig_col = jnp.sum(jnp.where(lane == hd, gcol, 0.0), axis=-1, keepdims=True)
    lf_col = jnp.sum(jnp.where(lane == hd + B_HEADS, gcol, 0.0), axis=-1, keepdims=True)
    ig_row = grow_ref[pl.ds(hd, 1), :]
    lf_row = grow_ref[pl.ds(hd + B_HEADS, 1), :]

    t_i = lax.broadcasted_iota(jnp.int32, (L, L), 0)
    s_i = lax.broadcasted_iota(jnp.int32, (L, L), 1)
    causal = s_i <= t_i
    lower = causal.astype(BF16)
    upper = (t_i <= s_i).astype(BF16)

    def split3(a):
        hi = a.astype(BF16).astype(F32)
        r1 = a - hi
        mid = r1.astype(BF16).astype(F32)
        return hi, mid, (r1 - mid).astype(BF16).astype(F32)

    hi, mid, lo = split3(lf_col)
    parts_col = jnp.where(lane == 0, hi, jnp.where(lane == 1, mid, jnp.where(lane == 2, lo, 0.0))).astype(BF16)
    cum = jnp.dot(lower, parts_col, preferred_element_type=F32)
    b_col = cum[:, 0:1] + cum[:, 1:2] + cum[:, 2:3]
    hi, mid, lo = split3(lf_row)
    sub = lax.broadcasted_iota(jnp.int32, (8, L), 0)
    parts_row = jnp.where(sub == 0, hi, jnp.where(sub == 1, mid, jnp.where(sub == 2, lo, 0.0))).astype(BF16)
    cum = jnp.dot(parts_row, upper, preferred_element_type=F32)
    b_row = cum[0:1, :] + cum[1:2, :] + cum[2:3, :]

    m_prev = m_ref[:1, :1]
    c_prev = c_ref[...]
    n_prev = n_ref[...]

    q = q_ref[...] * jnp.asarray(B_QK_DIM ** -0.5, q_ref.dtype)
    k = k_ref[...]
    v = v_ref[...]

    log_d = jnp.where(causal, b_col - b_row + ig_row, -jnp.inf)
    log_inter = b_col + m_prev
    m_t = jnp.maximum(log_inter, jnp.max(log_d, axis=-1, keepdims=True))
    scores = lax.dot_general(q, k, (((1,), (1,)), ((), ())), preferred_element_type=F32)
    w = scores * jnp.exp(log_d - m_t)
    decay = jnp.exp(log_inter - m_t)
    num = (decay * jnp.dot(q, c_prev.astype(BF16), preferred_element_type=F32)
           + jnp.dot(w.astype(BF16), v, preferred_element_type=F32))
    qn = (decay * jnp.sum(q.astype(F32) * n_prev, axis=-1, keepdims=True)
          + jnp.sum(w, axis=-1, keepdims=True))
    hval = num / jnp.maximum(jnp.abs(qn), jnp.exp(-m_t))

    inv = lax.rsqrt(jnp.mean(hval * hval, axis=-1, keepdims=True) + RMS_EPS)
    o_ref[...] = (hval * inv * gain_ref[...] * _sigmoid(og_ref[...].astype(F32))).astype(o_ref.dtype)

    b_last = b_col[L - 1:L, :]
    log_w = b_last - b_col + ig_col
    m_new = jnp.maximum(b_last + m_prev, jnp.max(log_w, axis=0, keepdims=True))
    w_s = jnp.exp(log_w - m_new)
    carry = jnp.exp(b_last + m_prev - m_new)
    k_w = k.astype(F32) * w_s
    c_ref[...] = carry * c_prev + lax.dot_general(
        k_w.astype(BF16), v, (((0,), (0,)), ((), ())), preferred_element_type=F32)
    n_ref[...] = carry * n_prev + jnp.sum(k_w, axis=0, keepdims=True)
    m_ref[...] = jnp.broadcast_to(m_new, m_ref.shape)


def mlstm_mixer(h, hb, norm_gain, w_in, layer, gate_bias, h_gain, w_out):
    seq, d = h.shape
    qk_w = B_HEADS * B_QK_DIM
    v_w = B_HEADS * B_V_DIM
    main_w = 2 * qk_w + v_w + d
    proj = fused_matmul(hb, w_in, layer, 0, main_w, _ep_identity, out_dtype=BF16, norm_gain=norm_gain,
                        name="mlstm_in_proj")
    n_gate = 2 * B_HEADS
    assert w_in.shape[2] == main_w + n_gate
    bias = jnp.pad(gate_bias, (0, B_GATE_PAD - n_gate)).reshape(1, B_GATE_PAD)
    gates = fused_matmul(hb, w_in, layer, main_w, B_GATE_PAD, _ep_gates, out_dtype=F32, tn=B_GATE_PAD,
                         norm_gain=norm_gain, extras=[(bias, (1, B_GATE_PAD), lambda j, i: (0, 0))],
                         name="mlstm_gate_proj")
    gates_t = gates[:, :n_gate].T
    L = min(B_CHUNK, seq)
    hps = B_HEADS_PER_STEP
    qk_blk, v_blk = hps * B_QK_DIM, hps * B_V_DIM
    nq = qk_w // qk_blk
    nv = v_w // v_blk
    mixed = pl.pallas_call(
        _mlstm_kernel,
        out_shape=jax.ShapeDtypeStruct((seq, v_w), BF16),
        grid=(B_HEADS // hps, seq // L),
        in_specs=[pl.BlockSpec((L, qk_blk), lambda g, c: (c, g)),
                  pl.BlockSpec((L, qk_blk), lambda g, c: (c, nq + g)),
                  pl.BlockSpec((L, v_blk), lambda g, c: (c, (2 * qk_w) // v_blk + g)),
                  pl.BlockSpec((L, v_blk), lambda g, c: (c, (2 * qk_w) // v_blk + nv + g)),
                  pl.BlockSpec((L, B_GATE_PAD), lambda g, c: (c, 0)),
                  pl.BlockSpec((n_gate, L), lambda g, c: (0, c)),
                  pl.BlockSpec((1, v_blk), lambda g, c: (0, g))],
        out_specs=pl.BlockSpec((L, v_blk), lambda g, c: (c, g)),
        scratch_shapes=[pltpu.VMEM((hps, B_QK_DIM, B_V_DIM), F32),
                        pltpu.VMEM((hps, 1, B_QK_DIM), F32),
                        pltpu.VMEM((hps, 8, LANES), F32)],
        compiler_params=_params(2),
        name="mlstm_chunks",
    )(proj, proj, proj, proj, gates, gates_t, h_gain.reshape(1, v_w))
    return matmul_residual(mixed, w_out, layer, h, name="mlstm_out_proj")


def _pool_kernel(u_ref, halo_ref, wg_ref, scale_ref, o_ref, wb_ref):
    g = pl.program_id(0)
    i = pl.program_id(1)
    tm = u_ref.shape[0]

    @pl.when(i == 0)
    def _():
        wb_ref[...] = wg_ref[...].astype(BF16)

    win = jnp.left_shift(jnp.int32(C_WINDOWS[0]), g)
    t_i = lax.broadcasted_iota(jnp.int32, (tm, tm), 0)
    s_i = lax.broadcasted_iota(jnp.int32, (tm, tm), 1)
    dist = t_i - s_i
    band = ((dist >= 0) & (dist < win)).astype(BF16)
    t_h = lax.broadcasted_iota(jnp.int32, (tm, C_HALO), 0)
    s_h = lax.broadcasted_iota(jnp.int32, (tm, C_HALO), 1)
    band_halo = ((t_h + C_HALO - s_h < win) & (i > 0)).astype(BF16)
    u = u_ref[...]
    total = (jnp.dot(band, u, preferred_element_type=F32)
             + jnp.dot(band_halo, halo_ref[...], preferred_element_type=F32))
    t_glob = i * tm + lax.broadcasted_iota(jnp.int32, (tm, 1), 0)
    count = jnp.minimum(t_glob + 1, win).astype(F32)
    pooled = total / count - u.astype(F32)
    y = jnp.dot(pooled.astype(BF16), wb_ref[...], preferred_element_type=F32) * scale_ref[...]
    o_ref[...] = y.astype(o_ref.dtype)


def pooling_mixer(h, hb, norm_gain, w_in, layer, w_group, scale, w_out, tm=512):
    seq, d = h.shape
    assert all(C_WINDOWS[g] == C_WINDOWS[0] << g for g in range(len(C_WINDOWS)))
    assert max(C_WINDOWS) <= C_HALO
    u = fused_matmul(hb, w_in, layer, 0, d, _ep_identity, out_dtype=BF16, norm_gain=norm_gain,
                     name="pool_in_proj")
    tm = min(tm, seq)
    gd = C_GROUP_DIM
    halo_per_tile = tm // C_HALO
    y = pl.pallas_call(
        _pool_kernel,
        out_shape=jax.ShapeDtypeStruct((seq, d), BF16),
        grid=(len(C_WINDOWS), seq // tm),
        in_specs=[pl.BlockSpec((tm, gd), lambda g, i: (i, g)),
                  pl.BlockSpec((C_HALO, gd), lambda g, i: (jnp.maximum(i * halo_per_tile - 1, 0), g)),
                  pl.BlockSpec((None, None, gd, gd), lambda g, i: (layer, g, 0, 0)),
                  pl.BlockSpec((1, gd), lambda g, i: (0, g))],
        out_specs=pl.BlockSpec((tm, gd), lambda g, i: (i, g)),
        scratch_shapes=[pltpu.VMEM((gd, gd), BF16)],
        compiler_params=_params(2),
        name="pool_group",
    )(u, u, w_group, scale.reshape(1, d))
    return matmul_residual(y, w_out, layer, h, name="pool_out_proj")


def squared_relu_mlp(h, hb, norm_gain, w_in, w_out, layer):
    hidden, w_out_bf16 = fused_matmul(hb, w_in, layer, 0, w_in.shape[2], _ep_relu_sq, out_dtype=BF16,
                                      norm_gain=norm_gain, side=(w_out, layer), name="mlp_up")
    return matmul_kgrid_residual(hidden, w_out_bf16, h, name="mlp_down")


def kernel(x, p, norm_mix, norm_mlp, norm_ple, w_ple, w_ple_gate, w_mlp_in, w_mlp_out, a_w_in, a_q_norm, a_k_norm, a_w_out, b_w_in, b_gate_bias, b_h_norm, b_w_out, c_w_in, c_w_group, c_scale, c_w_out):
    bsz, seq, d = x.shape
    depth = p.shape[0]
    rope = rope_tables(seq)
    outs = []
    for b in range(bsz):
        h = x.reshape(seq, d) if bsz == 1 else x[b]
        hb = None
        for i in range(depth):
            kind, j = i % 3, i // 3
            if kind == 0:
                h, hb = dilated_attention_mixer(h, hb, norm_mix[i], a_w_in, j, a_q_norm[j], a_k_norm[j], a_w_out, rope)
            elif kind == 1:
                h, hb = mlstm_mixer(h, hb, norm_mix[i], b_w_in, j, b_gate_bias[j], b_h_norm[j], b_w_out)
            else:
                h, hb = pooling_mixer(h, hb, norm_mix[i], c_w_in, j, c_w_group, c_scale[j], c_w_out)
            h, hb = squared_relu_mlp(h, hb, norm_mlp[i], w_mlp_in, w_mlp_out, i)
            h, hb = ple_update(h, hb, norm_ple[i], w_ple_gate, i,
                               p[i, b].astype(BF16), w_ple[i].astype(BF16))
        outs.append(h)
    return outs[0][None] if bsz == 1 else jnp.stack(outs, axis=0)
```
